```python
import jax
import jax.numpy as jnp
from jax import lax
import numpy as np

D_MODEL = 4096
BATCH = 4
SEQ = 2048
DEPTH = 2

D_FF = 4 * D_MODEL
NORM_EPS = 1e-6
LRU_WIDTH = D_MODEL // 2
LRU_HEADS = 8
LRU_BLOCK = LRU_WIDTH // LRU_HEADS
CONV_WIDTH = 4
LRU_C = 8.0
RWKV_WIDTH = D_MODEL // 2
RWKV_HEAD = 64
RWKV_HEADS = RWKV_WIDTH // RWKV_HEAD
DECAY_LORA = 96
AAA_LORA = 96
GATE_LORA = 256
RWKV_COLS = 3 * RWKV_WIDTH + DECAY_LORA + AAA_LORA + GATE_LORA
GN_EPS = 64e-5
HY_IN = 2 * LRU_WIDTH + RWKV_COLS
ML_HEADS = 8
ML_QK = D_MODEL // 2
ML_V = D_MODEL
ML_DQK = ML_QK // ML_HEADS
ML_DV = ML_V // ML_HEADS
ML_CHUNK = 64
ML_IN = 2 * ML_QK + 2 * ML_V + 2 * ML_HEADS
N_EVEN = (DEPTH + 1) // 2
N_ODD = DEPTH // 2

kernel_name = "hybrid_rglru_rwkv7_mlstm_block"

F32 = jnp.float32


def rmsnorm(x, g, eps=NORM_EPS):
    xf = x.astype(F32)
    y = xf * lax.rsqrt(jnp.mean(xf * xf, axis=-1, keepdims=True) + eps)
    return (y * g.astype(F32)).astype(x.dtype)


def shift_right(z):
    return jnp.pad(z, ((0, 0), (1, 0), (0, 0)))[:, :-1]


def causal_depthwise_conv(u, w, b):
    seq = u.shape[1]
    up = jnp.pad(u, ((0, 0), (CONV_WIDTH - 1, 0), (0, 0)))
    out = b
    for j in range(CONV_WIDTH):
        out = out + w[j] * up[:, j:j + seq]
    return out


def linear_recurrence_combine(c1, c2):
    a1, b1 = c1
    a2, b2 = c2
    return a1 * a2, a2 * b1 + b2


def rglru_branch(z, conv_w, conv_b, w_a, b_a, w_x, b_x, lam):
    bsz, seq, _ = z.shape
    u = z[..., :LRU_WIDTH].astype(F32)
    gate = jax.nn.gelu(z[..., LRU_WIDTH:].astype(F32))
    u = causal_depthwise_conv(u, conv_w.astype(F32), conv_b.astype(F32))
    ub = u.reshape(bsz, seq, LRU_HEADS, LRU_BLOCK)
    r = jax.nn.sigmoid(jnp.einsum("bshi,hij->bshj", ub, w_a.astype(F32)).reshape(bsz, seq, LRU_WIDTH) + b_a.astype(F32))
    i = jax.nn.sigmoid(jnp.einsum("bshi,hij->bshj", ub, w_x.astype(F32)).reshape(bsz, seq, LRU_WIDTH) + b_x.astype(F32))
    log_a = -LRU_C * r * jax.nn.softplus(-lam.astype(F32))
    a = jnp.exp(log_a)
    b = jnp.sqrt(-jnp.expm1(2.0 * log_a)) * (i * u)
    _, hseq = lax.associative_scan(linear_recurrence_combine, (a, b), axis=1)
    return hseq * gate


def rwkv7_branch(z, mu, w0, w2, a0, a2, g2, k_k, k_a, r_k, ln_w, ln_b):
    bsz, seq, _ = z.shape
    w0, w2, a0, a2, g2, k_k, k_a, r_k, ln_w, ln_b = (
        p.astype(F32) for p in (w0, w2, a0, a2, g2, k_k, k_a, r_k, ln_w, ln_b))
    z = z.astype(F32)
    z = z + (shift_right(z) - z) * mu.astype(F32)
    W = RWKV_WIDTH
    r = z[..., :W]
    k = z[..., W:2 * W]
    v = z[..., 2 * W:3 * W]
    o = 3 * W
    wl = z[..., o:o + DECAY_LORA]
    o += DECAY_LORA
    al = z[..., o:o + AAA_LORA]
    o += AAA_LORA
    gl = z[..., o:o + GATE_LORA]
    w = -jax.nn.softplus(-(w0 + jnp.tanh(wl) @ w2)) - 0.5
    decay = jnp.exp(-jnp.exp(w))
    a = jax.nn.sigmoid(a0 + al @ a2)
    g = jax.nn.sigmoid(gl) @ g2

    def hd(t):
        return t.reshape(bsz, seq, RWKV_HEADS, RWKV_HEAD)

    kk = hd(k * k_k)
    kk = kk / jnp.maximum(jnp.linalg.norm(kk, axis=-1, keepdims=True), 1e-12)
    k = k * (1.0 + (a - 1.0) * k_a)
    r, k, v, decay, a = (hd(t) for t in (r, k, v, decay, a))

    def tm(t):
        return jnp.swapaxes(t, 0, 1)

    def step(state, inp):
        r_t, w_t, k_t, v_t, kk_t, kka_t = inp
        sa = jnp.einsum("bhvk,bhk->bhv", state, kk_t)
        state = (state * w_t[:, :, None, :]
                 - sa[..., :, None] * kka_t[..., None, :]
                 + v_t[..., :, None] * k_t[..., None, :])
        return state, jnp.einsum("bhvk,bhk->bhv", state, r_t)

    s0 = jnp.zeros((bsz, RWKV_HEADS, RWKV_HEAD, RWKV_HEAD), F32)
    _, y = lax.scan(step, s0, (tm(r), tm(decay), tm(k), tm(v), tm(kk), tm(kk * a)))
    y = tm(y)
    mean = jnp.mean(y, axis=-1, keepdims=True)
    var = jnp.mean(jnp.square(y - mean), axis=-1, keepdims=True)
    y = (y - mean) * lax.rsqrt(var + GN_EPS) * ln_w.reshape(RWKV_HEADS, RWKV_HEAD) + ln_b.reshape(RWKV_HEADS, RWKV_HEAD)
    bonus = jnp.sum(r * k * r_k.reshape(RWKV_HEADS, RWKV_HEAD), axis=-1, keepdims=True) * v
    return (y + bonus).reshape(bsz, seq, W) * g


def mlstm_chunk_step(carry, inp):
    c_mat, n_vec, m_st = carry
    q, k, v, li, lf = inp
    L = q.shape[2]
    b = jnp.cumsum(lf, axis=-1)
    causal = jnp.tril(jnp.ones((L, L), dtype=bool))
    log_d = jnp.where(causal, b[..., :, None] - b[..., None, :] + li[..., None, :], -jnp.inf)
    log_inter = b + m_st[..., None]
    m_t = jnp.maximum(jnp.max(log_d, axis=-1), log_inter)
    p = jnp.einsum("bhld,bhsd->bhls", q, k) * jnp.exp(log_d - m_t[..., None])
    inter = jnp.exp(log_inter - m_t)
    num = jnp.einsum("bhls,bhsv->bhlv", p, v) + inter[..., None] * jnp.einsum("bhld,bhdv->bhlv", q, c_mat)
    den = jnp.sum(p, axis=-1) + inter * jnp.einsum("bhld,bhd->bhl", q, n_vec)
    h = num / jnp.maximum(jnp.abs(den), jnp.exp(-m_t))[..., None]
    b_last = b[..., -1]
    log_g = b_last[..., None] - b + li
    m_new = jnp.maximum(b_last + m_st, jnp.max(log_g, axis=-1))
    carry_decay = jnp.exp(b_last + m_st - m_new)
    wk = k * jnp.exp(log_g - m_new[..., None])[..., None]
    c_new = carry_decay[..., None, None] * c_mat + jnp.einsum("bhsd,bhsv->bhdv", wk, v)
    n_new = carry_decay[..., None] * n_vec + jnp.sum(wk, axis=2)
    return (c_new, n_new, m_new), h


def mlstm_mixer(hn, w_in, b_i, b_f, norm_w, w_out):
    bsz, seq, _ = hn.shape
    nc = seq // ML_CHUNK
    z = (hn @ w_in).astype(F32)
    o = 0
    zq = z[..., o:o + ML_QK]
    o += ML_QK
    zk = z[..., o:o + ML_QK]
    o += ML_QK
    zv = z[..., o:o + ML_V]
    o += ML_V
    zo = z[..., o:o + ML_V]
    o += ML_V
    zi = z[..., o:o + ML_HEADS]
    o += ML_HEADS
    zf = z[..., o:o + ML_HEADS]

    def chunk_heads(t, d):
        return t.reshape(bsz, nc, ML_CHUNK, ML_HEADS, d).transpose(1, 0, 3, 2, 4)

    def chunk_gates(t):
        return t.reshape(bsz, nc, ML_CHUNK, ML_HEADS).transpose(1, 0, 3, 2)

    q = chunk_heads(zq * (ML_DQK ** -0.5), ML_DQK)
    k = chunk_heads(zk, ML_DQK)
    v = chunk_heads(zv, ML_DV)
    li = chunk_gates(zi + b_i.astype(F32))
    lf = chunk_gates(jax.nn.log_sigmoid(zf + b_f.astype(F32)))
    init = (jnp.zeros((bsz, ML_HEADS, ML_DQK, ML_DV), F32),
            jnp.zeros((bsz, ML_HEADS, ML_DQK), F32),
            jnp.zeros((bsz, ML_HEADS), F32))
    _, hs = lax.scan(mlstm_chunk_step, init, (q, k, v, li, lf))
    hs = hs.transpose(1, 0, 3, 2, 4).reshape(bsz, seq, ML_HEADS, ML_DV)
    hs = hs * lax.rsqrt(jnp.mean(hs * hs, axis=-1, keepdims=True) + NORM_EPS)
    hs = hs.reshape(bsz, seq, ML_V) * norm_w.astype(F32) * jax.nn.sigmoid(zo)
    return hs.astype(hn.dtype) @ w_out


def sq_relu_mlp(h, w_up, w_down):
    u = jax.nn.relu(h @ w_up)
    return (u * u) @ w_down


def setup_inputs(seed: int = 0) -> dict:
    key = jax.random.key(seed)
    ks = iter(jax.random.split(key, 40))

    def nrm(shape, scale):
        return jax.random.normal(next(ks), shape, F32) * scale

    def unif(shape, lo, hi):
        return jax.random.uniform(next(ks), shape, F32, lo, hi)

    a_base = unif((N_EVEN, LRU_WIDTH), 0.9, 0.999)
    return {
        "x": nrm((BATCH, SEQ, D_MODEL), 1.0),
        "norm_mix": 1.0 + nrm((DEPTH, D_MODEL), 0.02),
        "norm_mlp": 1.0 + nrm((DEPTH, D_MODEL), 0.02),
        "norm_final": 1.0 + nrm((D_MODEL,), 0.02),
        "mlp_up": nrm((DEPTH, D_MODEL, D_FF), D_MODEL ** -0.5),
        "mlp_down": nrm((DEPTH, D_FF, D_MODEL), D_FF ** -0.5),
        "hy_in": nrm((N_EVEN, D_MODEL, HY_IN), D_MODEL ** -0.5),
        "lru_conv_w": nrm((N_EVEN, CONV_WIDTH, LRU_WIDTH), CONV_WIDTH ** -0.5),
        "lru_conv_b": nrm((N_EVEN, LRU_WIDTH), 0.02),
        "lru_wa": nrm((N_EVEN, LRU_HEADS, LRU_BLOCK, LRU_BLOCK), LRU_BLOCK ** -0.5),
        "lru_ba": nrm((N_EVEN, LRU_WIDTH), 0.1),
        "lru_wx": nrm((N_EVEN, LRU_HEADS, LRU_BLOCK, LRU_BLOCK), LRU_BLOCK ** -0.5),
        "lru_bx": nrm((N_EVEN, LRU_WIDTH), 0.1),
        "lru_lam": jnp.log(a_base) - jnp.log1p(-a_base),
        "rwkv_mu": unif((N_EVEN, RWKV_COLS), 0.0, 1.0),
        "rwkv_w0": unif((N_EVEN, RWKV_WIDTH), -6.0, -1.0),
        "rwkv_w2": nrm((N_EVEN, DECAY_LORA, RWKV_WIDTH), 0.5 * DECAY_LORA ** -0.5),
        "rwkv_a0": nrm((N_EVEN, RWKV_WIDTH), 0.1),
        "rwkv_a2": nrm((N_EVEN, AAA_LORA, RWKV_WIDTH), 0.5 * AAA_LORA ** -0.5),
        "rwkv_g2": nrm((N_EVEN, GATE_LORA, RWKV_WIDTH), GATE_LORA ** -0.5),
        "rwkv_kk": 0.85 + nrm((N_EVEN, RWKV_WIDTH), 0.05),
        "rwkv_ka": 1.0 + nrm((N_EVEN, RWKV_WIDTH), 0.05),
        "rwkv_rk": nrm((N_EVEN, RWKV_WIDTH), 0.1),
        "rwkv_ln_w": 1.0 + nrm((N_EVEN, RWKV_WIDTH), 0.02),
        "rwkv_ln_b": nrm((N_EVEN, RWKV_WIDTH), 0.02),
        "hy_out": nrm((N_EVEN, LRU_WIDTH + RWKV_WIDTH, D_MODEL), (LRU_WIDTH + RWKV_WIDTH) ** -0.5),
        "ml_in": nrm((N_ODD, D_MODEL, ML_IN), D_MODEL ** -0.5),
        "ml_bi": -2.0 + nrm((N_ODD, ML_HEADS), 0.1),
        "ml_bf": unif((N_ODD, ML_HEADS), 3.0, 6.0),
        "ml_norm": 1.0 + nrm((N_ODD, ML_V), 0.02),
        "ml_out": nrm((N_ODD, ML_V, D_MODEL), ML_V ** -0.5),
    }


def reference(x, norm_mix, norm_mlp, norm_final, mlp_up, mlp_down, hy_in, lru_conv_w, lru_conv_b,
              lru_wa, lru_ba, lru_wx, lru_bx, lru_lam, rwkv_mu, rwkv_w0, rwkv_w2, rwkv_a0, rwkv_a2,
              rwkv_g2, rwkv_kk, rwkv_ka, rwkv_rk, rwkv_ln_w, rwkv_ln_b, hy_out, ml_in, ml_bi, ml_bf,
              ml_norm, ml_out):
    for layer in range(DEPTH):
        h = rmsnorm(x, norm_mix[layer])
        if layer % 2 == 0:
            e = layer // 2
            z = h @ hy_in[e]
            ya = rglru_branch(z[..., :2 * LRU_WIDTH], lru_conv_w[e], lru_conv_b[e], lru_wa[e], lru_ba[e],
                              lru_wx[e], lru_bx[e], lru_lam[e])
            yb = rwkv7_branch(z[..., 2 * LRU_WIDTH:], rwkv_mu[e], rwkv_w0[e], rwkv_w2[e], rwkv_a0[e],
                              rwkv_a2[e], rwkv_g2[e], rwkv_kk[e], rwkv_ka[e], rwkv_rk[e],
                              rwkv_ln_w[e], rwkv_ln_b[e])
            mix = jnp.concatenate([ya, yb], axis=-1).astype(x.dtype) @ hy_out[e]
        else:
            o = layer // 2
            mix = mlstm_mixer(h, ml_in[o], ml_bi[o], ml_bf[o], ml_norm[o], ml_out[o])
        x = x + mix
        x = x + sq_relu_mlp(rmsnorm(x, norm_mlp[layer]), mlp_up[layer], mlp_down[layer])
    return rmsnorm(x, norm_final)
```

```python
import functools

import jax
import jax.numpy as jnp
from jax import lax
from jax.experimental import pallas as pl
from jax.experimental.pallas import tpu as pltpu

F32 = jnp.float32
BF16 = jnp.bfloat16
HIGHEST = lax.Precision.HIGHEST

NORM_EPS = 1e-6
GN_EPS = 64e-5
LRU_C = 8.0
CONV_WIDTH = 4
RWKV_HEAD = 64
LANES = 128
SUBLANES = 8
MXU_WIDTH = 256
VMEM_LIMIT_BYTES = 60 * 1024 * 1024
RWKV_CHUNK = 64
ML_CHUNK_ROWS = 128
NORM_ROWS = 128


def _cparams(semantics):
    return pltpu.CompilerParams(dimension_semantics=semantics, vmem_limit_bytes=VMEM_LIMIT_BYTES)


def _softplus(x):
    return jnp.maximum(x, 0.0) + jnp.log1p(jnp.exp(-jnp.abs(x)))


def _sigmoid(x):
    return 1.0 / (1.0 + jnp.exp(-x))


def _dot(a, b, dims=(((1,), (0,)), ((), ())), precision=None):
    return lax.dot_general(a, b, dims, precision=precision, preferred_element_type=F32)


_NN = (((1,), (0,)), ((), ()))
_NT = (((1,), (1,)), ((), ()))
_TN = (((0,), (0,)), ((), ()))


def _mm(a, b, dims=_NN, *, exact):
    if exact:
        return _dot(a, b, dims, precision=HIGHEST)
    return _dot(a.astype(BF16), b.astype(BF16), dims)


_RWKV_EXACT = dict(gram=False, inv=False, state=False, apply=False)


def _rms_matmul_kernel(x_ref, g_ref, w_ref, o_ref, hn_ref, *, relu2):
    @pl.when(pl.program_id(1) == 0)
    def _():
        rows = min(NORM_ROWS, x_ref.shape[0])

        def norm_rows(c, carry):
            sl = pl.ds(pl.multiple_of(c * rows, rows), rows)
            x = x_ref[sl, :]
            ms = jnp.mean(x * x, axis=-1, keepdims=True)
            hn_ref[sl, :] = (x * lax.rsqrt(ms + NORM_EPS) * g_ref[...]).astype(BF16)
            return carry

        lax.fori_loop(0, x_ref.shape[0] // rows, norm_rows, 0)

    acc = jnp.dot(hn_ref[...], w_ref[...], preferred_element_type=F32)
    if relu2:
        acc = jnp.maximum(acc, 0.0)
        acc = acc * acc
    o_ref[...] = acc.astype(o_ref.dtype)


def _rms_matmul(x, g, w, *, out_dtype, relu2=False, tm, tn):
    t, d = x.shape
    n = w.shape[1]
    assert t % tm == 0 and n % tn == 0
    return pl.pallas_call(
        functools.partial(_rms_matmul_kernel, relu2=relu2),
        grid=(t // tm, n // tn),
        in_specs=[
            pl.BlockSpec((tm, d), lambda i, j: (i, 0), pipeline_mode=pl.Buffered(1)),
            pl.BlockSpec((1, d), lambda i, j: (0, 0)),
            pl.BlockSpec((d, tn), lambda i, j: (0, j)),
        ],
        out_specs=pl.BlockSpec((tm, tn), lambda i, j: (i, j)),
        out_shape=jax.ShapeDtypeStruct((t, n), out_dtype),
        scratch_shapes=[pltpu.VMEM((tm, d), BF16)],
        compiler_params=_cparams(("parallel", "arbitrary")),
        name="rms_matmul",
    )(x, g.reshape(1, d), w)


def _matmul_res_kernel(a_ref, w_ref, r_ref, o_ref):
    @pl.when(pl.program_id(2) == 0)
    def _():
        o_ref[...] = r_ref[...]

    o_ref[...] += jnp.dot(a_ref[...], w_ref[...], preferred_element_type=F32)


def _matmul_res(a, w, res, *, tm, tn, tk):
    t, k = a.shape
    n = w.shape[1]
    assert t % tm == 0 and n % tn == 0 and k % tk == 0
    return pl.pallas_call(
        _matmul_res_kernel,
        grid=(t // tm, n // tn, k // tk),
        in_specs=[
            pl.BlockSpec((tm, tk), lambda i, j, kk: (i, kk)),
            pl.BlockSpec((tk, tn), lambda i, j, kk: (kk, j)),
            pl.BlockSpec((tm, tn), lambda i, j, kk: (i, j)),
        ],
        out_specs=pl.BlockSpec((tm, tn), lambda i, j, kk: (i, j)),
        out_shape=jax.ShapeDtypeStruct((t, n), F32),
        compiler_params=_cparams(("parallel", "parallel", "arbitrary")),
        name="matmul_res",
    )(a, w, res)


def _rmsnorm_kernel(x_ref, g_ref, o_ref):
    x = x_ref[...]
    ms = jnp.mean(x * x, axis=-1, keepdims=True)
    o_ref[...] = x * lax.rsqrt(ms + NORM_EPS) * g_ref[...]


def _rmsnorm(x, g, *, tm):
    t, d = x.shape
    return pl.pallas_call(
        _rmsnorm_kernel,
        grid=(t // tm,),
        in_specs=[pl.BlockSpec((tm, d), lambda i: (i, 0)), pl.BlockSpec((1, d), lambda i: (0, 0))],
        out_specs=pl.BlockSpec((tm, d), lambda i: (i, 0)),
        out_shape=jax.ShapeDtypeStruct((t, d), F32),
        compiler_params=_cparams(("parallel",)),
        name="final_rmsnorm",
    )(x, g.reshape(1, d))


def _lru_kernel(u_ref, gate_ref, cw_ref, cb_ref, wa_ref, ba_ref, wx_ref, bx_ref, lam_ref,
                o_ref, xbuf, hcar, *, tc):
    @pl.when(pl.program_id(2) == 0)
    def _():
        xbuf[0:SUBLANES, :] = jnp.zeros((SUBLANES, xbuf.shape[1]), F32)
        hcar[...] = jnp.zeros(hcar.shape, F32)

    u = u_ref[...]
    xbuf[SUBLANES:, :] = u
    cw = cw_ref[...]
    conv = cb_ref[...] + cw[3:4] * u
    for j in range(CONV_WIDTH - 1):
        conv = conv + cw[j:j + 1] * xbuf[pl.ds(SUBLANES - (CONV_WIDTH - 1 - j), tc), :]
    xbuf[0:SUBLANES, :] = u[tc - SUBLANES:, :]

    cbf = conv.astype(BF16)
    r = _sigmoid(jnp.dot(cbf, wa_ref[...], preferred_element_type=F32) + ba_ref[...])
    i = _sigmoid(jnp.dot(cbf, wx_ref[...], preferred_element_type=F32) + bx_ref[...])
    log_a = (-LRU_C) * r * _softplus(-lam_ref[...])
    a = jnp.exp(log_a)
    b = jnp.sqrt(-jnp.tanh(log_a) * (a * a + 1.0)) * (i * conv)

    row = lax.broadcasted_iota(jnp.int32, a.shape, 0)
    d = 1
    while d < tc:
        a_sh = pltpu.roll(a, d, 0)
        b_sh = pltpu.roll(b, d, 0)
        keep = row >= d
        b = jnp.where(keep, a * b_sh + b, b)
        a = jnp.where(keep, a * a_sh, a)
        d *= 2
    h = a * hcar[...] + b
    hcar[...] = h[tc - 1:tc, :]
    o_ref[...] = (h * jax.nn.gelu(gate_ref[...])).astype(o_ref.dtype)


def _lru_branch(z, conv_w, conv_b, w_a, b_a, w_x, b_x, lam, *, tc):
    bsz, seq, _ = z.shape
    heads, blk, _ = w_a.shape
    width = heads * blk
    assert blk % LANES == 0 and seq % tc == 0
    row = lambda v: v.reshape(1, width)
    vec_spec = pl.BlockSpec((1, blk), lambda b, h, t: (0, h))
    mat_spec = pl.BlockSpec((None, blk, blk), lambda b, h, t: (h, 0, 0))
    return pl.pallas_call(
        functools.partial(_lru_kernel, tc=tc),
        grid=(bsz, heads, seq // tc),
        in_specs=[
            pl.BlockSpec((None, tc, blk), lambda b, h, t: (b, t, h)),
            pl.BlockSpec((None, tc, blk), lambda b, h, t: (b, t, heads + h)),
            pl.BlockSpec((CONV_WIDTH, blk), lambda b, h, t: (0, h)),
            vec_spec, mat_spec, vec_spec, mat_spec, vec_spec, vec_spec,
        ],
        out_specs=pl.BlockSpec((None, tc, blk), lambda b, h, t: (b, t, h)),
        out_shape=jax.ShapeDtypeStruct((bsz, seq, width), BF16),
        scratch_shapes=[pltpu.VMEM((tc + SUBLANES, blk), F32), pltpu.VMEM((1, blk), F32)],
        compiler_params=_cparams(("parallel", "parallel", "arbitrary")),
        name="rglru",
    )(z, z, conv_w, row(conv_b), w_a.astype(BF16), row(b_a), w_x.astype(BF16), row(b_x), row(lam))


def _rwkv_kernel(r_ref, k_ref, v_ref, lora_ref, mu_r, mu_k, mu_v, mu_l, w0_ref, w2_ref, a0_ref, a2_ref,
                 g2_ref, kk_ref, ka_ref, rk_ref, lnw_ref, lnb_ref, o_ref,
                 rbuf, kbuf, vbuf, lbuf, state, *, tc):
    L = RWKV_CHUNK
    P2 = 2 * L
    assert P2 == LANES and RWKV_HEAD == L
    ex = _RWKV_EXACT

    @pl.when(pl.program_id(2) == 0)
    def _():
        for buf in (rbuf, kbuf, vbuf, lbuf):
            buf[0:SUBLANES, :] = jnp.zeros((SUBLANES, buf.shape[1]), F32)
        state[...] = jnp.zeros(state.shape, F32)

    def token_shift(ref, buf, mu_ref):
        cur = ref[...]
        buf[SUBLANES:, :] = cur
        prev = buf[pl.ds(SUBLANES - 1, tc), :]
        buf[0:SUBLANES, :] = cur[tc - SUBLANES:, :]
        return cur + (prev - cur) * mu_ref[...]

    r = token_shift(r_ref, rbuf, mu_r)
    k = token_shift(k_ref, kbuf, mu_k)
    v = token_shift(v_ref, vbuf, mu_v)
    lora = token_shift(lora_ref, lbuf, mu_l)

    dw = _dot(jnp.tanh(lora).astype(BF16), w2_ref[...])
    w = -_softplus(-(w0_ref[...] + dw)) - 0.5
    logw = -jnp.exp(w)
    a = _sigmoid(a0_ref[...] + _dot(lora.astype(BF16), a2_ref[...]))
    g = _dot(_sigmoid(lora).astype(BF16), g2_ref[...])

    ri = lax.broadcasted_iota(jnp.int32, (P2, P2), 0)
    ci = lax.broadcasted_iota(jnp.int32, (P2, P2), 1)
    blk = jnp.where(ri >= L, 1, 0) == jnp.where(ci >= L, 1, 0)
    head_ones = jnp.where(blk, 1.0, 0.0).astype(F32)

    def head_sum(x):
        return _dot(x, head_ones, precision=HIGHEST)

    kk = k * kk_ref[...]
    kk = kk / jnp.maximum(jnp.sqrt(head_sum(kk * kk)), 1e-12)
    kmod = k * (1.0 + (a - 1.0) * ka_ref[...])
    kka = kk * a
    bonus = head_sum(r * kmod * rk_ref[...]) * v

    strict = blk & (ci < ri)
    incl = blk & (ci <= ri)
    eye = jnp.where(ri == ci, 1.0, 0.0).astype(F32)
    tril_ones = jnp.where((lax.broadcasted_iota(jnp.int32, (L, L), 1)
                           <= lax.broadcasted_iota(jnp.int32, (L, L), 0)), 1.0, 0.0).astype(F32)
    lane_row = lax.broadcasted_iota(jnp.int32, (1, LANES), 1)
    m0 = jnp.where(lane_row < RWKV_HEAD, 1.0, 0.0).astype(F32)
    m1 = 1.0 - m0

    def stack(x):
        return jnp.concatenate([x * m0, x * m1], axis=0)

    s = state[...]
    ys = []
    for c in range(tc // L):
        sl = slice(c * L, (c + 1) * L)
        lw = logw[sl]
        cum = _dot(tril_ones, lw, precision=HIGHEST)
        cum_last = cum[L - 1:L, :]
        p_incl = jnp.exp(cum)
        p_excl = jnp.exp(cum - lw)
        p_inv = jnp.exp(-cum)
        p_tail = jnp.exp(cum_last - cum)
        r_s = stack(r[sl] * p_incl)
        kk_s = stack(kk[sl] * p_excl)
        k_s = stack(kmod[sl] * p_inv)
        a_s = stack(kka[sl] * p_inv)
        v_s = stack(v[sl])
        kt_s = stack(kmod[sl] * p_tail)
        at_s = stack(kka[sl] * p_tail)

        gram = _mm(jnp.concatenate([kk_s, r_s], axis=0), jnp.concatenate([k_s, a_s], axis=0), _NT,
                   exact=ex["gram"])
        a_kk_k = jnp.where(strict, gram[0:P2, 0:P2], 0.0)
        n_mat = jnp.where(strict, gram[0:P2, P2:], 0.0)
        a_r_k = jnp.where(incl, gram[P2:, 0:P2], 0.0)
        a_r_a = jnp.where(incl, gram[P2:, P2:], 0.0)

        inv = eye - n_mat
        pw = _mm(n_mat, n_mat, exact=ex["inv"])
        steps = 1
        while True:
            inv = inv + _mm(inv, pw, exact=ex["inv"])
            steps *= 2
            if 2 * steps >= L:
                break
            pw = _mm(pw, pw, exact=ex["inv"])

        rhs = _mm(kk_s, s, _NT, exact=ex["state"]) + _mm(a_kk_k, v_s, exact=ex["apply"])
        sa = _mm(inv, rhs, exact=ex["apply"])
        y_s = (_mm(r_s, s, _NT, exact=ex["state"]) + _mm(a_r_k, v_s, exact=ex["apply"])
               - _mm(a_r_a, sa, exact=ex["apply"]))
        ys.append(y_s[0:L] + y_s[L:])
        s = s * jnp.exp(cum_last) + _mm(jnp.concatenate([v_s, sa], axis=0),
                                        jnp.concatenate([kt_s, -at_s], axis=0), _TN, exact=ex["state"])
    state[...] = s

    y = jnp.concatenate(ys, axis=0)
    inv_n = 1.0 / RWKV_HEAD
    mean = head_sum(y) * inv_n
    yc = y - mean
    var = head_sum(yc * yc) * inv_n
    yn = yc * lax.rsqrt(var + GN_EPS) * lnw_ref[...] + lnb_ref[...]
    o_ref[...] = ((yn + bonus) * g).astype(o_ref.dtype)


def _rwkv_branch(z, col0, mu, w0, w2, a0, a2, g2, k_k, k_a, r_k, ln_w, ln_b, *, tc):
    bsz, seq, npad = z.shape
    width = w0.shape[0]
    pairs = width // LANES
    d_lora, a_lora, g_lora = w2.shape[0], a2.shape[0], g2.shape[0]
    lora_w = npad - col0 - 3 * width
    assert width % LANES == 0 and col0 % LANES == 0 and lora_w % LANES == 0
    assert lora_w >= d_lora + a_lora + g_lora and (col0 + 3 * width) % lora_w == 0
    assert seq % tc == 0 and tc % RWKV_CHUNK == 0
    cb = col0 // LANES

    def pad_rows(wm, start):
        return jnp.zeros((lora_w, width), F32).at[start:start + wm.shape[0]].set(wm).astype(BF16)

    w2p = pad_rows(w2, 0)
    a2p = pad_rows(a2, d_lora)
    g2p = pad_rows(g2, d_lora + a_lora)
    mu_pad = jnp.zeros((1, 3 * width + lora_w), F32).at[0, :mu.shape[0]].set(mu)
    mu_r, mu_k, mu_v = (mu_pad[:, i * width:(i + 1) * width] for i in range(3))
    mu_l = mu_pad[:, 3 * width:]
    row = lambda x: x.reshape(1, width)

    act = lambda off: pl.BlockSpec((None, tc, LANES), lambda b, p, t: (b, t, cb + off * pairs + p))
    vec = pl.BlockSpec((1, LANES), lambda b, p, t: (0, p))
    lmat = pl.BlockSpec((lora_w, LANES), lambda b, p, t: (0, p))
    lora_blk = (col0 + 3 * width) // lora_w
    return pl.pallas_call(
        functools.partial(_rwkv_kernel, tc=tc),
        grid=(bsz, pairs, seq // tc),
        in_specs=[
            act(0), act(1), act(2),
            pl.BlockSpec((None, tc, lora_w), lambda b, p, t: (b, t, lora_blk)),
            vec, vec, vec,
            pl.BlockSpec((1, lora_w), lambda b, p, t: (0, 0)),
            vec, lmat, vec, lmat, lmat, vec, vec, vec, vec, vec,
        ],
        out_specs=pl.BlockSpec((None, tc, LANES), lambda b, p, t: (b, t, p)),
        out_shape=jax.ShapeDtypeStruct((bsz, seq, width), BF16),
        scratch_shapes=[
            pltpu.VMEM((tc + SUBLANES, LANES), F32), pltpu.VMEM((tc + SUBLANES, LANES), F32),
            pltpu.VMEM((tc + SUBLANES, LANES), F32), pltpu.VMEM((tc + SUBLANES, lora_w), F32),
            pltpu.VMEM((LANES, LANES), F32),
        ],
        compiler_params=_cparams(("parallel", "parallel", "arbitrary")),
        name="rwkv7",
    )(z, z, z, z, mu_r, mu_k, mu_v, mu_l, row(w0), w2p, row(a0), a2p, g2p, row(k_k), row(k_a), row(r_k),
      row(ln_w), row(ln_b))


def _mlstm_kernel(q_ref, k_ref, v_ref, zo_ref, gcol_ref, grow_ref, bcol_ref, brow_ref, nw_ref, o_ref,
                  c_st, n_st, m_st, *, heads, dqk, dv):
    L = q_ref.shape[0]

    @pl.when(pl.program_id(1) == 0)
    def _():
        c_st[...] = jnp.zeros(c_st.shape, F32)
        n_st[...] = jnp.zeros(n_st.shape, F32)
        m_st[...] = jnp.zeros(m_st.shape, F32)

    ri = lax.broadcasted_iota(jnp.int32, (L, L), 0)
    ci = lax.broadcasted_iota(jnp.int32, (L, L), 1)
    causal = ci <= ri
    tril_ones = jnp.where(causal, 1.0, 0.0).astype(F32)
    triu_ones = jnp.where(ci >= ri, 1.0, 0.0).astype(F32)

    gc = gcol_ref[...] + bcol_ref[...]
    gr = grow_ref[...] + brow_ref[...]
    b_c = _dot(tril_ones, -_softplus(-gc), precision=HIGHEST)
    b_r = _dot(-_softplus(-gr), triu_ones, precision=HIGHEST)

    scale = dqk ** -0.5
    for h in range(heads):
        q = (q_ref[:, h * dqk:(h + 1) * dqk] * scale).astype(BF16)
        k = k_ref[:, h * dqk:(h + 1) * dqk]
        v = v_ref[:, h * dv:(h + 1) * dv].astype(BF16)
        bc, lic = b_c[:, heads + h:heads + h + 1], gc[:, h:h + 1]
        br, lir = b_r[heads + h:heads + h + 1, :], gr[h:h + 1, :]
        m_prev = m_st[h][0:1, 0:1]
        c_mat = c_st[h]
        n_vec = n_st[h]

        log_d = jnp.where(causal, bc - br + lir, -jnp.inf)
        log_inter = bc + m_prev
        m_t = jnp.maximum(jnp.max(log_d, axis=-1, keepdims=True), log_inter)
        p = _dot(q, k.astype(BF16), _NT) * jnp.exp(log_d - m_t)
        inter = jnp.exp(log_inter - m_t)
        num = _dot(p.astype(BF16), v) + inter * _dot(q, c_mat.astype(BF16))
        den = (jnp.sum(p, axis=-1, keepdims=True)
               + inter * jnp.sum(q.astype(F32) * n_vec, axis=-1, keepdims=True))
        hs = num / jnp.maximum(jnp.abs(den), jnp.exp(-m_t))

        b_last = bc[L - 1:L, :]
        log_g = b_last - bc + lic
        m_new = jnp.maximum(b_last + m_prev, jnp.max(log_g, axis=0, keepdims=True))
        carry = jnp.exp(b_last + m_prev - m_new)
        wk = k * jnp.exp(log_g - m_new)
        c_st[h] = carry * c_mat + _dot(wk.astype(BF16), v, _TN)
        n_st[h] = carry * n_vec + jnp.sum(wk, axis=0, keepdims=True)
        m_st[h] = jnp.broadcast_to(m_new, m_st.shape[1:])

        hs = hs * lax.rsqrt(jnp.mean(hs * hs, axis=-1, keepdims=True) + NORM_EPS)
        sl = slice(h * dv, (h + 1) * dv)
        o_ref[:, sl] = (hs * nw_ref[:, sl] * _sigmoid(zo_ref[:, sl])).astype(o_ref.dtype)


def _mlstm_branch(z, gates, b_i, b_f, norm_w, *, heads, qk, vdim):
    bsz, seq, _ = z.shape
    L = ML_CHUNK_ROWS
    dqk, dv = qk // heads, vdim // heads
    assert seq % L == 0 and vdim % qk == 0 and qk % LANES == 0
    ratio = vdim // qk
    bias = jnp.concatenate([b_i, b_f]).astype(F32)
    bias_col = jnp.zeros((1, LANES), F32).at[0, :2 * heads].set(bias)
    gates_row = jnp.swapaxes(gates[:, :, :2 * heads], 1, 2)
    return pl.pallas_call(
        functools.partial(_mlstm_kernel, heads=heads, dqk=dqk, dv=dv),
        grid=(bsz, seq // L),
        in_specs=[
            pl.BlockSpec((None, L, qk), lambda b, c: (b, c, 0)),
            pl.BlockSpec((None, L, qk), lambda b, c: (b, c, 1)),
            pl.BlockSpec((None, L, vdim), lambda b, c: (b, c, 2 // ratio)),
            pl.BlockSpec((None, L, vdim), lambda b, c: (b, c, 2 // ratio + 1)),
            pl.BlockSpec((None, L, LANES), lambda b, c: (b, c, 0)),
            pl.BlockSpec((None, 2 * heads, L), lambda b, c: (b, 0, c)),
            pl.BlockSpec((1, LANES), lambda b, c: (0, 0)),
            pl.BlockSpec((2 * heads, 1), lambda b, c: (0, 0)),
            pl.BlockSpec((1, vdim), lambda b, c: (0, 0)),
        ],
        out_specs=pl.BlockSpec((None, L, vdim), lambda b, c: (b, c, 0)),
        out_shape=jax.ShapeDtypeStruct((bsz, seq, vdim), BF16),
        scratch_shapes=[
            pltpu.VMEM((heads, dqk, dv), F32),
            pltpu.VMEM((heads, 1, dqk), F32),
            pltpu.VMEM((heads, SUBLANES, LANES), F32),
        ],
        compiler_params=_cparams(("parallel", "arbitrary")),
        name="mlstm",
    )(z, z, z, z, gates, gates_row, bias_col, bias.reshape(-1, 1), norm_w.reshape(1, vdim))


def _pad_cols_to(w, n):
    return jnp.pad(w, ((0, 0), (0, n - w.shape[1])))


def _col_tile(n, cap):
    for step in (MXU_WIDTH, LANES):
        fits = [c for c in range(step, cap + 1, step) if n % c == 0]
        if fits:
            return fits[-1]
    raise ValueError(f"no lane-aligned column tile divides {n}")


def _lora_cols(offset, need):
    w = -(-need // LANES) * LANES
    while offset % w:
        w += LANES
    return w


def kernel(x, norm_mix, norm_mlp, norm_final, mlp_up, mlp_down, hy_in, lru_conv_w, lru_conv_b, lru_wa, lru_ba, lru_wx, lru_bx, lru_lam, rwkv_mu, rwkv_w0, rwkv_w2, rwkv_a0, rwkv_a2, rwkv_g2, rwkv_kk, rwkv_ka, rwkv_rk, rwkv_ln_w, rwkv_ln_b, hy_out, ml_in, ml_bi, ml_bf, ml_norm, ml_out):
    bsz, seq, d = x.shape
    t = bsz * seq
    depth = norm_mix.shape[0]
    tm = min(1024, t)
    xf = x.reshape(t, d)

    for layer in range(depth):
        if layer % 2 == 0:
            e = layer // 2
            lru_w = lru_lam.shape[1]
            rw_w = rwkv_w0.shape[1]
            lora_off = 2 * lru_w + 3 * rw_w
            n_pad = lora_off + _lora_cols(lora_off, hy_in.shape[2] - lora_off)
            w_in = _pad_cols_to(hy_in[e].astype(BF16), n_pad)
            z = _rms_matmul(xf, norm_mix[layer], w_in, out_dtype=F32, tm=tm, tn=_col_tile(n_pad, 1024))
            z = z.reshape(bsz, seq, -1)
            tcl = min(256, seq)
            ya = _lru_branch(z, lru_conv_w[e], lru_conv_b[e], lru_wa[e], lru_ba[e], lru_wx[e], lru_bx[e],
                             lru_lam[e], tc=tcl)
            yb = _rwkv_branch(z, 2 * lru_w, rwkv_mu[e], rwkv_w0[e], rwkv_w2[e], rwkv_a0[e], rwkv_a2[e],
                              rwkv_g2[e], rwkv_kk[e], rwkv_ka[e], rwkv_rk[e], rwkv_ln_w[e], rwkv_ln_b[e],
                              tc=tcl)
            mix_in = jnp.concatenate([ya, yb], axis=-1).reshape(t, lru_w + rw_w)
            w_out = hy_out[e].astype(BF16)
        else:
            o = layer // 2
            heads = ml_bi.shape[1]
            vdim = ml_norm.shape[1]
            qk = (ml_in.shape[2] - 2 * vdim - 2 * heads) // 2
            g0 = 2 * qk + 2 * vdim
            w_all = ml_in[o].astype(BF16)
            z = _rms_matmul(xf, norm_mix[layer], w_all[:, :g0], out_dtype=F32, tm=tm, tn=_col_tile(g0, 1024))
            z = z.reshape(bsz, seq, g0)
            gates = _rms_matmul(xf, norm_mix[layer], _pad_cols_to(w_all[:, g0:], LANES), out_dtype=F32,
                                tm=tm, tn=LANES).reshape(bsz, seq, LANES)
            mix_in = _mlstm_branch(z, gates, ml_bi[o], ml_bf[o], ml_norm[o], heads=heads, qk=qk,
                                   vdim=vdim).reshape(t, vdim)
            w_out = ml_out[o].astype(BF16)
        tn_out = min(1024, d)
        xf = _matmul_res(mix_in, w_out, xf, tm=tm, tn=tn_out, tk=w_out.shape[0])
        u = _rms_matmul(xf, norm_mlp[layer], mlp_up[layer].astype(BF16), out_dtype=BF16, relu2=True,
                        tm=tm, tn=min(1024, mlp_up.shape[2]))
        xf = _matmul_res(u, mlp_down[layer].astype(BF16), xf, tm=tm, tn=tn_out,
                         tk=min(2048, mlp_down.shape[1]))
    return _rmsnorm(xf, norm_final, tm=min(256, t)).reshape(bsz, seq, d)
```

```python
import functools

import jax
import jax.numpy as jnp
from jax import lax
from jax.experimental import pallas as pl
from jax.experimental.pallas import tpu as pltpu

F32 = jnp.float32
BF16 = jnp.bfloat16
HIGHEST = lax.Precision.HIGHEST

NORM_EPS = 1e-6
GN_EPS = 64e-5
LRU_C = 8.0
CONV_WIDTH = 4
RWKV_HEAD = 64
LANES = 128
SUBLANES = 8
MXU_WIDTH = 256
VMEM_LIMIT_BYTES = 60 * 1024 * 1024
RWKV_CHUNK = 64
RWKV_ROWS = 64
RWKV_PAIRS = 16
LRU_ROWS = 256
ML_CHUNK_ROWS = 128
NORM_ROWS = 128
ROW_TILE = 1024
COL_TILE = 1024
K_TILE = 2048


def _cparams(semantics):
    return pltpu.CompilerParams(dimension_semantics=semantics, vmem_limit_bytes=VMEM_LIMIT_BYTES)


def _softplus(x):
    return jnp.maximum(x, 0.0) + jnp.log1p(jnp.exp(-jnp.abs(x)))


def _sigmoid(x):
    return 1.0 / (1.0 + jnp.exp(-x))


_NN = (((1,), (0,)), ((), ()))
_NT = (((1,), (1,)), ((), ()))
_TN = (((0,), (0,)), ((), ()))
_BNN = (((2,), (1,)), ((0,), (0,)))
_BNT = (((2,), (2,)), ((0,), (0,)))
_BTN = (((1,), (1,)), ((0,), (0,)))


def _dot(a, b, dims=_NN, precision=None):
    return lax.dot_general(a, b, dims, precision=precision, preferred_element_type=F32)


def _bdot(a, b, dims=_NN):
    return lax.dot_general(a.astype(BF16), b.astype(BF16), dims, preferred_element_type=F32)


def _rms_matmul_kernel(x_ref, g_ref, w_ref, o_ref, hn_ref, *, relu2):
    @pl.when(pl.program_id(1) == 0)
    def _():
        rows = min(NORM_ROWS, x_ref.shape[0])

        def norm_rows(c, carry):
            sl = pl.ds(pl.multiple_of(c * rows, rows), rows)
            x = x_ref[sl, :]
            ms = jnp.mean(x * x, axis=-1, keepdims=True)
            hn_ref[sl, :] = (x * lax.rsqrt(ms + NORM_EPS) * g_ref[...]).astype(BF16)
            return carry

        lax.fori_loop(0, x_ref.shape[0] // rows, norm_rows, 0)

    acc = jnp.dot(hn_ref[...], w_ref[...], preferred_element_type=F32)
    if relu2:
        acc = jnp.maximum(acc, 0.0)
        acc = acc * acc
    o_ref[...] = acc.astype(o_ref.dtype)


def _rms_matmul(x, g, w, layer, n, *, out_dtype, relu2=False):
    t, d = x.shape
    tm = min(ROW_TILE, t)
    tn = _col_tile(n, COL_TILE)
    assert t % tm == 0
    return pl.pallas_call(
        functools.partial(_rms_matmul_kernel, relu2=relu2),
        grid=(t // tm, n // tn),
        in_specs=[
            pl.BlockSpec((tm, d), lambda i, j: (i, 0), pipeline_mode=pl.Buffered(1)),
            pl.BlockSpec((1, d), lambda i, j: (0, 0)),
            pl.BlockSpec((None, d, tn), lambda i, j: (layer, 0, j)),
        ],
        out_specs=pl.BlockSpec((tm, tn), lambda i, j: (i, j)),
        out_shape=jax.ShapeDtypeStruct((t, n), out_dtype),
        scratch_shapes=[pltpu.VMEM((tm, d), BF16)],
        compiler_params=_cparams(("parallel", "arbitrary")),
        name="rms_matmul",
    )(x, g.reshape(1, d), w)


def _matmul_res_kernel(*refs, n_lhs):
    a_refs, (w_ref, r_ref, o_ref) = refs[:n_lhs], refs[n_lhs:]
    kk = pl.program_id(2)

    @pl.when(kk == 0)
    def _():
        o_ref[...] = r_ref[...]

    if n_lhs == 1:
        o_ref[...] += jnp.dot(a_refs[0][...], w_ref[...], preferred_element_type=F32)
    else:
        for idx, a_ref in enumerate(a_refs):
            @pl.when(kk == idx)
            def _(a_ref=a_ref):
                o_ref[...] += jnp.dot(a_ref[...], w_ref[...], preferred_element_type=F32)


def _matmul_res(lhs, w, layer, res, *, tk):
    t, n = res.shape
    k = w.shape[1]
    tm = min(ROW_TILE, t)
    tn = _col_tile(n, COL_TILE)
    assert t % tm == 0 and k % tk == 0
    if len(lhs) == 1:
        a_specs = [pl.BlockSpec((tm, tk), lambda i, j, kk: (i, kk))]
    else:
        assert all(a.shape[1] == tk for a in lhs) and len(lhs) * tk == k
        a_specs = [pl.BlockSpec((tm, tk), lambda i, j, kk: (i, 0)) for _ in lhs]
    return pl.pallas_call(
        functools.partial(_matmul_res_kernel, n_lhs=len(lhs)),
        grid=(t // tm, n // tn, k // tk),
        in_specs=a_specs + [
            pl.BlockSpec((None, tk, tn), lambda i, j, kk: (layer, kk, j)),
            pl.BlockSpec((tm, tn), lambda i, j, kk: (i, j)),
        ],
        out_specs=pl.BlockSpec((tm, tn), lambda i, j, kk: (i, j)),
        out_shape=jax.ShapeDtypeStruct((t, n), F32),
        compiler_params=_cparams(("parallel", "parallel", "arbitrary")),
        name="matmul_res",
    )(*lhs, w, res)


def _rmsnorm_kernel(x_ref, g_ref, o_ref):
    x = x_ref[...]
    ms = jnp.mean(x * x, axis=-1, keepdims=True)
    o_ref[...] = x * lax.rsqrt(ms + NORM_EPS) * g_ref[...]


def _rmsnorm(x, g, *, tm):
    t, d = x.shape
    return pl.pallas_call(
        _rmsnorm_kernel,
        grid=(t // tm,),
        in_specs=[pl.BlockSpec((tm, d), lambda i: (i, 0)), pl.BlockSpec((1, d), lambda i: (0, 0))],
        out_specs=pl.BlockSpec((tm, d), lambda i: (i, 0)),
        out_shape=jax.ShapeDtypeStruct((t, d), F32),
        compiler_params=_cparams(("parallel",)),
        name="final_rmsnorm",
    )(x, g.reshape(1, d))


def _lru_kernel(u_ref, gate_ref, cw_ref, cb_ref, wa_ref, ba_ref, wx_ref, bx_ref, lam_ref,
                o_ref, xbuf, hcar, *, tc):
    @pl.when(pl.program_id(2) == 0)
    def _():
        xbuf[0:SUBLANES, :] = jnp.zeros((SUBLANES, xbuf.shape[1]), F32)
        hcar[...] = jnp.zeros(hcar.shape, F32)

    u = u_ref[...]
    xbuf[SUBLANES:, :] = u
    cw = cw_ref[...]
    conv = cb_ref[...] + cw[CONV_WIDTH - 1:CONV_WIDTH] * u
    for j in range(CONV_WIDTH - 1):
        back = CONV_WIDTH - 1 - j
        conv = conv + cw[j:j + 1] * xbuf[SUBLANES - back:SUBLANES - back + tc, :]
    xbuf[0:SUBLANES, :] = u[tc - SUBLANES:, :]

    cbf = conv.astype(BF16)
    r = _sigmoid(jnp.dot(cbf, wa_ref[...], preferred_element_type=F32) + ba_ref[...])
    i = _sigmoid(jnp.dot(cbf, wx_ref[...], preferred_element_type=F32) + bx_ref[...])
    log_a = (-LRU_C) * r * _softplus(-lam_ref[...])
    a = jnp.exp(log_a)
    b = jnp.sqrt(-jnp.tanh(log_a) * (a * a + 1.0)) * (i * conv)

    row = lax.broadcasted_iota(jnp.int32, a.shape, 0)
    d = 1
    while d < tc:
        a_sh = pltpu.roll(a, d, 0)
        b_sh = pltpu.roll(b, d, 0)
        keep = row >= d
        b = jnp.where(keep, a * b_sh + b, b)
        a = jnp.where(keep, a * a_sh, a)
        d *= 2
    h = a * hcar[...] + b
    hcar[...] = h[tc - 1:tc, :]
    o_ref[...] = (h * jax.nn.gelu(gate_ref[...])).astype(o_ref.dtype)


def _lru_branch(z, conv_w, conv_b, w_a, b_a, w_x, b_x, lam, *, tc):
    bsz, seq, _ = z.shape
    heads, blk, _ = w_a.shape
    width = heads * blk
    assert blk % LANES == 0 and seq % tc == 0
    row = lambda v: v.reshape(1, width)
    vec_spec = pl.BlockSpec((1, blk), lambda b, h, t: (0, h))
    mat_spec = pl.BlockSpec((None, blk, blk), lambda b, h, t: (h, 0, 0))
    return pl.pallas_call(
        functools.partial(_lru_kernel, tc=tc),
        grid=(bsz, heads, seq // tc),
        in_specs=[
            pl.BlockSpec((None, tc, blk), lambda b, h, t: (b, t, h)),
            pl.BlockSpec((None, tc, blk), lambda b, h, t: (b, t, heads + h)),
            pl.BlockSpec((CONV_WIDTH, blk), lambda b, h, t: (0, h)),
            vec_spec, mat_spec, vec_spec, mat_spec, vec_spec, vec_spec,
        ],
        out_specs=pl.BlockSpec((None, tc, blk), lambda b, h, t: (b, t, h)),
        out_shape=jax.ShapeDtypeStruct((bsz, seq, width), BF16),
        scratch_shapes=[pltpu.VMEM((tc + SUBLANES, blk), F32), pltpu.VMEM((1, blk), F32)],
        compiler_params=_cparams(("parallel", "parallel", "arbitrary")),
        name="rglru",
    )(z, z, conv_w, row(conv_b), w_a.astype(BF16), row(b_a), w_x.astype(BF16), row(b_x), row(lam))


def _rwkv_kernel(r_ref, k_ref, v_ref, lora_ref, mu_r, mu_k, mu_v, mu_l, w0_ref, w2_ref, a0_ref, a2_ref,
                 g2_ref, kk_ref, ka_ref, rk_ref, lnw_ref, lnb_ref, o_ref,
                 rbuf, kbuf, vbuf, lbuf, state, *, tc, pp):
    L = RWKV_CHUNK
    P2 = 2 * L
    nc = tc // L
    assert P2 == LANES and RWKV_HEAD == L

    @pl.when(pl.program_id(2) == 0)
    def _():
        for buf in (rbuf, kbuf, vbuf, lbuf):
            buf[0:SUBLANES, :] = jnp.zeros((SUBLANES, buf.shape[1]), F32)
        state[...] = jnp.zeros(state.shape, F32)

    def token_shift(ref, buf, mu_ref):
        cur = ref[...]
        buf[SUBLANES:, :] = cur
        prev = buf[SUBLANES - 1:SUBLANES - 1 + tc, :]
        buf[0:SUBLANES, :] = cur[tc - SUBLANES:, :]
        return cur + (prev - cur) * mu_ref[...]

    r = token_shift(r_ref, rbuf, mu_r)
    k = token_shift(k_ref, kbuf, mu_k)
    v = token_shift(v_ref, vbuf, mu_v)
    lora = token_shift(lora_ref, lbuf, mu_l)

    dw = _dot(jnp.tanh(lora).astype(BF16), w2_ref[...])
    w = -_softplus(-(w0_ref[...] + dw)) - 0.5
    logw = -jnp.exp(w)
    a = _sigmoid(a0_ref[...] + _dot(lora.astype(BF16), a2_ref[...]))
    g = _dot(_sigmoid(lora).astype(BF16), g2_ref[...])

    ri = lax.broadcasted_iota(jnp.int32, (P2, P2), 0)
    ci = lax.broadcasted_iota(jnp.int32, (P2, P2), 1)
    blk = jnp.where(ri >= L, 1, 0) == jnp.where(ci >= L, 1, 0)
    head_ones = jnp.where(blk, 1.0, 0.0).astype(BF16)

    def to_rows(x):
        return jnp.concatenate([x[:, j * LANES:(j + 1) * LANES] for j in range(pp)], axis=0)

    def from_rows(x):
        return jnp.concatenate([x[j * tc:(j + 1) * tc, :] for j in range(pp)], axis=1)

    def head_sum(x):
        rows = to_rows(x)
        hi = rows.astype(BF16)
        lo = (rows - hi.astype(F32)).astype(BF16)
        return from_rows(_dot(hi, head_ones) + _dot(lo, head_ones))

    kk = k * kk_ref[...]
    kk = kk / jnp.maximum(jnp.sqrt(head_sum(kk * kk)), 1e-12)
    kmod = k * (1.0 + (a - 1.0) * ka_ref[...])
    kka = kk * a
    bonus = head_sum(r * kmod * rk_ref[...]) * v

    rt = lax.broadcasted_iota(jnp.int32, (tc, tc), 0)
    ct = lax.broadcasted_iota(jnp.int32, (tc, tc), 1)
    shift = L.bit_length() - 1
    same_chunk = lax.shift_right_logical(rt, shift) == lax.shift_right_logical(ct, shift)
    cum = _dot(jnp.where(same_chunk & (ct <= rt), 1.0, 0.0).astype(F32), logw, precision=HIGHEST)
    p_inv = jnp.exp(-cum)
    r_d = r * jnp.exp(cum)
    kk_d = kk * jnp.exp(cum - logw)
    k_d = kmod * p_inv
    a_d = kka * p_inv
    cum_last = jnp.concatenate(
        [jnp.broadcast_to(cum[(c + 1) * L - 1:(c + 1) * L, :], (L, cum.shape[1])) for c in range(nc)], axis=0)
    p_tail = jnp.exp(cum_last - cum)
    k_t = kmod * p_tail
    a_t = kka * p_tail
    p_last = jnp.exp(cum_last)

    strict = (blk & (ci < ri))[None]
    incl = (blk & (ci <= ri))[None]
    eye = jnp.where(ri == ci, 1.0, 0.0).astype(F32)[None]
    lane_row = lax.broadcasted_iota(jnp.int32, (1, LANES), 1)
    m0 = jnp.where(lane_row < RWKV_HEAD, 1.0, 0.0).astype(F32)
    m1 = 1.0 - m0

    def batch(x):
        out = []
        for c in range(nc):
            for j in range(pp):
                xs = x[c * L:(c + 1) * L, j * LANES:(j + 1) * LANES]
                out.append(jnp.concatenate([xs * m0, xs * m1], axis=0))
        return jnp.stack(out)

    r_b, kk_b, k_b, a_b, v_b, kt_b, at_b = (batch(x) for x in (r_d, kk_d, k_d, a_d, v, k_t, a_t))

    gram = _bdot(jnp.concatenate([kk_b, r_b], axis=1), jnp.concatenate([k_b, a_b], axis=1), _BNT)
    a_kk_k = jnp.where(strict, gram[:, 0:P2, 0:P2], 0.0)
    n_mat = jnp.where(strict, gram[:, 0:P2, P2:], 0.0)
    a_r_k = jnp.where(incl, gram[:, P2:, 0:P2], 0.0)
    a_r_a = jnp.where(incl, gram[:, P2:, P2:], 0.0)

    inv = eye - n_mat
    pw = _bdot(n_mat, n_mat, _BNN)
    steps = 1
    while True:
        inv = inv + _bdot(inv, pw, _BNN)
        steps *= 2
        if 2 * steps >= L:
            break
        pw = _bdot(pw, pw, _BNN)
    akk_v = _bdot(a_kk_k, v_b, _BNN)
    ark_v = _bdot(a_r_k, v_b, _BNN)

    s = state[...]
    y_chunks = []
    for c in range(nc):
        sl = slice(c * pp, (c + 1) * pp)
        sa = _bdot(inv[sl], _bdot(kk_b[sl], s, _BNT) + akk_v[sl], _BNN)
        y_s = _bdot(r_b[sl], s, _BNT) + ark_v[sl] - _bdot(a_r_a[sl], sa, _BNN)
        y_c = y_s[:, 0:L, :] + y_s[:, L:, :]
        y_chunks.append(jnp.concatenate([y_c[j] for j in range(pp)], axis=1))
        decay = jnp.stack([p_last[c * L:c * L + 1, j * LANES:(j + 1) * LANES] for j in range(pp)])
        s = s * decay + _bdot(jnp.concatenate([v_b[sl], sa], axis=1),
                              jnp.concatenate([kt_b[sl], -at_b[sl]], axis=1), _BTN)
    state[...] = s

    y = jnp.concatenate(y_chunks, axis=0)
    inv_n = 1.0 / RWKV_HEAD
    mean = head_sum(y) * inv_n
    yc = y - mean
    var = head_sum(yc * yc) * inv_n
    yn = yc * lax.rsqrt(var + GN_EPS) * lnw_ref[...] + lnb_ref[...]
    o_ref[...] = ((yn + bonus) * g).astype(o_ref.dtype)


def _rwkv_branch(z, col0, z_lora, mu, w0, w2, a0, a2, g2, k_k, k_a, r_k, ln_w, ln_b, *, tc, pp):
    bsz, seq, _ = z.shape
    width = w0.shape[0]
    lora_w = z_lora.shape[2]
    bw = pp * LANES
    d_lora, a_lora, g_lora = w2.shape[0], a2.shape[0], g2.shape[0]
    assert width % bw == 0 and col0 % bw == 0 and lora_w % LANES == 0
    assert lora_w >= d_lora + a_lora + g_lora
    assert seq % tc == 0 and tc % RWKV_CHUNK == 0 and tc % SUBLANES == 0
    groups = width // bw
    cb = col0 // bw

    def pad_rows(wm, start):
        return jnp.zeros((lora_w, width), F32).at[start:start + wm.shape[0]].set(wm).astype(BF16)

    w2p = pad_rows(w2, 0)
    a2p = pad_rows(a2, d_lora)
    g2p = pad_rows(g2, d_lora + a_lora)
    mu_pad = jnp.zeros((1, 3 * width + lora_w), F32).at[0, :mu.shape[0]].set(mu)
    mu_r, mu_k, mu_v = (mu_pad[:, i * width:(i + 1) * width] for i in range(3))
    mu_l = mu_pad[:, 3 * width:]
    row = lambda x: x.reshape(1, width)

    act = lambda off: pl.BlockSpec((None, tc, bw), lambda b, p, t: (b, t, cb + off * groups + p))
    vec = pl.BlockSpec((1, bw), lambda b, p, t: (0, p))
    lmat = pl.BlockSpec((lora_w, bw), lambda b, p, t: (0, p))
    return pl.pallas_call(
        functools.partial(_rwkv_kernel, tc=tc, pp=pp),
        grid=(bsz, groups, seq // tc),
        in_specs=[
            act(0), act(1), act(2),
            pl.BlockSpec((None, tc, lora_w), lambda b, p, t: (b, t, 0)),
            vec, vec, vec,
            pl.BlockSpec((1, lora_w), lambda b, p, t: (0, 0)),
            vec, lmat, vec, lmat, lmat, vec, vec, vec, vec, vec,
        ],
        out_specs=pl.BlockSpec((None, tc, bw), lambda b, p, t: (b, t, p)),
        out_shape=jax.ShapeDtypeStruct((bsz, seq, width), BF16),
        scratch_shapes=[
            pltpu.VMEM((tc + SUBLANES, bw), F32), pltpu.VMEM((tc + SUBLANES, bw), F32),
            pltpu.VMEM((tc + SUBLANES, bw), F32), pltpu.VMEM((tc + SUBLANES, lora_w), F32),
            pltpu.VMEM((pp, LANES, LANES), F32),
        ],
        compiler_params=_cparams(("parallel", "parallel", "arbitrary")),
        name="rwkv7",
    )(z, z, z, z_lora, mu_r, mu_k, mu_v, mu_l, row(w0), w2p, row(a0), a2p, g2p, row(k_k), row(k_a),
      row(r_k), row(ln_w), row(ln_b))


def _mlstm_kernel(q_ref, k_ref, v_ref, zo_ref, gcol_ref, grow_ref, bcol_ref, brow_ref, nw_ref, o_ref,
                  c_st, n_st, m_st, *, heads, dqk, dv):
    L = q_ref.shape[0]

    @pl.when(pl.program_id(1) == 0)
    def _():
        c_st[...] = jnp.zeros(c_st.shape, F32)
        n_st[...] = jnp.zeros(n_st.shape, F32)
        m_st[...] = jnp.zeros(m_st.shape, F32)

    ri = lax.broadcasted_iota(jnp.int32, (L, L), 0)
    ci = lax.broadcasted_iota(jnp.int32, (L, L), 1)
    causal = ci <= ri
    tril_ones = jnp.where(causal, 1.0, 0.0).astype(F32)
    triu_ones = jnp.where(ci >= ri, 1.0, 0.0).astype(F32)

    gc = gcol_ref[...] + bcol_ref[...]
    gr = grow_ref[...] + brow_ref[...]
    b_c = _dot(tril_ones, -_softplus(-gc), precision=HIGHEST)
    b_r = _dot(-_softplus(-gr), triu_ones, precision=HIGHEST)

    scale = dqk ** -0.5
    for h in range(heads):
        q = (q_ref[:, h * dqk:(h + 1) * dqk] * scale).astype(BF16)
        k = k_ref[:, h * dqk:(h + 1) * dqk]
        v = v_ref[:, h * dv:(h + 1) * dv].astype(BF16)
        bc, lic = b_c[:, heads + h:heads + h + 1], gc[:, h:h + 1]
        br, lir = b_r[heads + h:heads + h + 1, :], gr[h:h + 1, :]
        m_prev = m_st[h][0:1, 0:1]
        c_mat = c_st[h]
        n_vec = n_st[h]

        log_d = jnp.where(causal, bc - br + lir, -jnp.inf)
        log_inter = bc + m_prev
        m_t = jnp.maximum(jnp.max(log_d, axis=-1, keepdims=True), log_inter)
        p = _dot(q, k.astype(BF16), _NT) * jnp.exp(log_d - m_t)
        inter = jnp.exp(log_inter - m_t)
        num = _dot(p.astype(BF16), v) + inter * _dot(q, c_mat.astype(BF16))
        den = (jnp.sum(p, axis=-1, keepdims=True)
               + inter * jnp.sum(q.astype(F32) * n_vec, axis=-1, keepdims=True))
        hs = num / jnp.maximum(jnp.abs(den), jnp.exp(-m_t))

        b_last = bc[L - 1:L, :]
        log_g = b_last - bc + lic
        m_new = jnp.maximum(b_last + m_prev, jnp.max(log_g, axis=0, keepdims=True))
        carry = jnp.exp(b_last + m_prev - m_new)
        wk = k * jnp.exp(log_g - m_new)
        c_st[h] = carry * c_mat + _dot(wk.astype(BF16), v, _TN)
        n_st[h] = carry * n_vec + jnp.sum(wk, axis=0, keepdims=True)
        m_st[h] = jnp.broadcast_to(m_new, m_st.shape[1:])

        hs = hs * lax.rsqrt(jnp.mean(hs * hs, axis=-1, keepdims=True) + NORM_EPS)
        sl = slice(h * dv, (h + 1) * dv)
        o_ref[:, sl] = (hs * nw_ref[:, sl] * _sigmoid(zo_ref[:, sl])).astype(o_ref.dtype)


def _mlstm_branch(z, gates, b_i, b_f, norm_w, *, heads, qk, vdim):
    bsz, seq, _ = z.shape
    L = ML_CHUNK_ROWS
    dqk, dv = qk // heads, vdim // heads
    assert seq % L == 0 and vdim % qk == 0 and qk % LANES == 0 and (2 * qk) % vdim == 0
    vblk = 2 * qk // vdim
    bias = jnp.concatenate([b_i, b_f]).astype(F32)
    bias_col = jnp.zeros((1, LANES), F32).at[0, :2 * heads].set(bias)
    gates_row = jnp.swapaxes(gates[:, :, :2 * heads], 1, 2)
    return pl.pallas_call(
        functools.partial(_mlstm_kernel, heads=heads, dqk=dqk, dv=dv),
        grid=(bsz, seq // L),
        in_specs=[
            pl.BlockSpec((None, L, qk), lambda b, c: (b, c, 0)),
            pl.BlockSpec((None, L, qk), lambda b, c: (b, c, 1)),
            pl.BlockSpec((None, L, vdim), lambda b, c: (b, c, vblk)),
            pl.BlockSpec((None, L, vdim), lambda b, c: (b, c, vblk + 1)),
            pl.BlockSpec((None, L, LANES), lambda b, c: (b, c, 0)),
            pl.BlockSpec((None, 2 * heads, L), lambda b, c: (b, 0, c)),
            pl.BlockSpec((1, LANES), lambda b, c: (0, 0)),
            pl.BlockSpec((2 * heads, 1), lambda b, c: (0, 0)),
            pl.BlockSpec((1, vdim), lambda b, c: (0, 0)),
        ],
        out_specs=pl.BlockSpec((None, L, vdim), lambda b, c: (b, c, 0)),
        out_shape=jax.ShapeDtypeStruct((bsz, seq, vdim), BF16),
        scratch_shapes=[
            pltpu.VMEM((heads, dqk, dv), F32),
            pltpu.VMEM((heads, 1, dqk), F32),
            pltpu.VMEM((heads, SUBLANES, LANES), F32),
        ],
        compiler_params=_cparams(("parallel", "arbitrary")),
        name="mlstm",
    )(z, z, z, z, gates, gates_row, bias_col, bias.reshape(-1, 1), norm_w.reshape(1, vdim))


def _col_tile(n, cap):
    for step in (MXU_WIDTH, LANES):
        fits = [c for c in range(step, cap + 1, step) if n % c == 0]
        if fits:
            return fits[-1]
    raise ValueError(f"no lane-aligned column tile divides {n}")


def _tail_weight(w, start):
    tail = w[:, :, start:]
    pad = (-tail.shape[2]) % LANES
    return jnp.pad(tail, ((0, 0), (0, 0), (0, pad))).astype(BF16)


def kernel(x, norm_mix, norm_mlp, norm_final, mlp_up, mlp_down, hy_in, lru_conv_w, lru_conv_b, lru_wa, lru_ba, lru_wx, lru_bx, lru_lam, rwkv_mu, rwkv_w0, rwkv_w2, rwkv_a0, rwkv_a2, rwkv_g2, rwkv_kk, rwkv_ka, rwkv_rk, rwkv_ln_w, rwkv_ln_b, hy_out, ml_in, ml_bi, ml_bf, ml_norm, ml_out):
    bsz, seq, d = x.shape
    t = bsz * seq
    depth = norm_mix.shape[0]
    xf = x.reshape(t, d)

    lru_w = lru_lam.shape[1]
    rw_w = rwkv_w0.shape[1]
    hy_main = 2 * lru_w + 3 * rw_w
    heads = ml_bi.shape[1]
    vdim = ml_norm.shape[1]
    qk = (ml_in.shape[2] - 2 * vdim - 2 * heads) // 2
    ml_main = 2 * qk + 2 * vdim

    up_b, down_b = mlp_up.astype(BF16), mlp_down.astype(BF16)
    hy_in_b, hy_lora_b, hy_out_b = hy_in.astype(BF16), _tail_weight(hy_in, hy_main), hy_out.astype(BF16)
    ml_in_b, ml_gate_b, ml_out_b = ml_in.astype(BF16), _tail_weight(ml_in, ml_main), ml_out.astype(BF16)

    for layer in range(depth):
        g_mix = norm_mix[layer]
        if layer % 2 == 0:
            e = layer // 2
            z = _rms_matmul(xf, g_mix, hy_in_b, e, hy_main, out_dtype=F32).reshape(bsz, seq, hy_main)
            z_lora = _rms_matmul(xf, g_mix, hy_lora_b, e, hy_lora_b.shape[2], out_dtype=F32)
            z_lora = z_lora.reshape(bsz, seq, -1)
            ya = _lru_branch(z, lru_conv_w[e], lru_conv_b[e], lru_wa[e], lru_ba[e], lru_wx[e], lru_bx[e],
                             lru_lam[e], tc=min(LRU_ROWS, seq))
            yb = _rwkv_branch(z, 2 * lru_w, z_lora, rwkv_mu[e], rwkv_w0[e], rwkv_w2[e], rwkv_a0[e], rwkv_a2[e],
                              rwkv_g2[e], rwkv_kk[e], rwkv_ka[e], rwkv_rk[e], rwkv_ln_w[e], rwkv_ln_b[e],
                              tc=min(RWKV_ROWS, seq), pp=min(RWKV_PAIRS, rw_w // LANES))
            lhs = [ya.reshape(t, lru_w), yb.reshape(t, rw_w)]
            assert lru_w == rw_w
            xf = _matmul_res(lhs, hy_out_b, e, xf, tk=lru_w)
        else:
            o = layer // 2
            z = _rms_matmul(xf, g_mix, ml_in_b, o, ml_main, out_dtype=F32).reshape(bsz, seq, ml_main)
            gates = _rms_matmul(xf, g_mix, ml_gate_b, o, ml_gate_b.shape[2], out_dtype=F32)
            gates = gates.reshape(bsz, seq, -1)
            hs = _mlstm_branch(z, gates, ml_bi[o], ml_bf[o], ml_norm[o], heads=heads, qk=qk, vdim=vdim)
            xf = _matmul_res([hs.reshape(t, vdim)], ml_out_b, o, xf, tk=vdim)
        u = _rms_matmul(xf, norm_mlp[layer], up_b, layer, up_b.shape[2], out_dtype=BF16, relu2=True)
        xf = _matmul_res([u], down_b, layer, xf, tk=min(K_TILE, down_b.shape[1]))
    return _rmsnorm(xf, norm_final, tm=min(256, t)).reshape(bsz, seq, d)
```

```python
import functools

import jax
import jax.numpy as jnp
from jax import lax
from jax.experimental import pallas as pl
from jax.experimental.pallas import tpu as pltpu

F32 = jnp.float32
BF16 = jnp.bfloat16
HIGHEST = lax.Precision.HIGHEST

NORM_EPS = 1e-6
GN_EPS = 64e-5
LRU_C = 8.0
CONV_WIDTH = 4
RWKV_HEAD = 64
LANES = 128
SUBLANES = 8
MXU_WIDTH = 256
VMEM_LIMIT_BYTES = 60 * 1024 * 1024
RWKV_CHUNK = 64
RWKV_ROWS = 128
RWKV_PAIRS = 16
LRU_ROWS = 256
ML_CHUNK_ROWS = 128
NORM_ROWS = 256
ROW_TILE = 2048
COL_TILE = 512
K_TILE = 2048


def _cparams(semantics):
    return pltpu.CompilerParams(dimension_semantics=semantics, vmem_limit_bytes=VMEM_LIMIT_BYTES)


def _softplus(x):
    return jnp.maximum(x, 0.0) + jnp.log1p(jnp.exp(-jnp.abs(x)))


def _sigmoid(x):
    return 1.0 / (1.0 + jnp.exp(-x))


_NN = (((1,), (0,)), ((), ()))
_NT = (((1,), (1,)), ((), ()))
_TN = (((0,), (0,)), ((), ()))
_BNN = (((2,), (1,)), ((0,), (0,)))
_BNT = (((2,), (2,)), ((0,), (0,)))
_BTN = (((1,), (1,)), ((0,), (0,)))


def _dot(a, b, dims=_NN, precision=None):
    return lax.dot_general(a, b, dims, precision=precision, preferred_element_type=F32)


def _bdot(a, b, dims=_NN):
    return lax.dot_general(a.astype(BF16), b.astype(BF16), dims, preferred_element_type=F32)


def _norm_cast_kernel(x_ref, g_ref, o_ref):
    x = x_ref[...]
    ms = jnp.mean(x * x, axis=-1, keepdims=True)
    o_ref[...] = (x * lax.rsqrt(ms + NORM_EPS) * g_ref[...]).astype(o_ref.dtype)


def _norm_cast(x, g, out_dtype):
    t, d = x.shape
    tm = min(NORM_ROWS, t)
    return pl.pallas_call(
        _norm_cast_kernel,
        grid=(t // tm,),
        in_specs=[pl.BlockSpec((tm, d), lambda i: (i, 0)), pl.BlockSpec((1, d), lambda i: (0, 0))],
        out_specs=pl.BlockSpec((tm, d), lambda i: (i, 0)),
        out_shape=jax.ShapeDtypeStruct((t, d), out_dtype),
        compiler_params=_cparams(("parallel",)),
        name="rmsnorm",
    )(x, g.reshape(1, d))


def _matmul_kernel(*refs, n_lhs, nk, has_res, relu2):
    a_refs, w_ref = refs[:n_lhs], refs[n_lhs]
    r_ref = refs[n_lhs + 1] if has_res else None
    o_ref = refs[n_lhs + 1 + has_res]
    acc_ref = refs[-1] if (not has_res and nk > 1) else None
    kk = pl.program_id(2)

    def finish(acc):
        if relu2:
            acc = jnp.maximum(acc, 0.0)
            acc = acc * acc
        return acc.astype(o_ref.dtype)

    def accumulate(a_ref):
        def part():
            return jnp.dot(a_ref[...], w_ref[...].astype(BF16), preferred_element_type=F32)

        if has_res:
            @pl.when(kk == 0)
            def _():
                o_ref[...] = r_ref[...] + part()

            @pl.when(kk > 0)
            def _():
                o_ref[...] += part()
        elif nk == 1:
            o_ref[...] = finish(part())
        else:
            @pl.when(kk == 0)
            def _():
                acc_ref[...] = part()

            @pl.when((kk > 0) & (kk < nk - 1))
            def _():
                acc_ref[...] += part()

            @pl.when(kk == nk - 1)
            def _():
                o_ref[...] = finish(acc_ref[...] + part())

    if n_lhs == 1:
        accumulate(a_refs[0])
    else:
        for idx, a_ref in enumerate(a_refs):
            pl.when(kk == idx)(functools.partial(accumulate, a_ref))


def _matmul(lhs, w, layer, n, *, tk, tn, out_dtype=F32, res=None, relu2=False):
    t = lhs[0].shape[0]
    k = w.shape[1]
    tm = min(ROW_TILE, t)
    nk = k // tk
    assert t % tm == 0 and k % tk == 0 and n % tn == 0
    once = dict(pipeline_mode=pl.Buffered(1))
    if len(lhs) == 1:
        a_specs = [pl.BlockSpec((tm, tk), lambda i, j, kk: (i, kk), **(once if nk == 1 else {}))]
    else:
        assert all(a.shape[1] == tk for a in lhs) and len(lhs) == nk
        a_specs = [pl.BlockSpec((tm, tk), lambda i, j, kk: (i, 0), **once) for _ in lhs]
    has_res = res is not None
    assert not (has_res and (relu2 or out_dtype != F32))
    tile = pl.BlockSpec((tm, tn), lambda i, j, kk: (i, j))
    scratch = [pltpu.VMEM((tm, tn), F32)] if (not has_res and nk > 1) else []
    return pl.pallas_call(
        functools.partial(_matmul_kernel, n_lhs=len(lhs), nk=nk, has_res=has_res, relu2=relu2),
        grid=(t // tm, n // tn, nk),
        in_specs=a_specs + [pl.BlockSpec((None, tk, tn), lambda i, j, kk: (layer, kk, j))]
        + ([tile] if has_res else []),
        out_specs=tile,
        out_shape=jax.ShapeDtypeStruct((t, n), out_dtype),
        scratch_shapes=scratch,
        compiler_params=_cparams(("parallel", "parallel", "arbitrary")),
        name="matmul",
    )(*lhs, w, *([res] if has_res else []))


def _lru_kernel(u_ref, gate_ref, cw_ref, cb_ref, wa_ref, ba_ref, wx_ref, bx_ref, lam_ref,
                o_ref, xbuf, hcar, *, tc):
    @pl.when(pl.program_id(2) == 0)
    def _():
        xbuf[0:SUBLANES, :] = jnp.zeros((SUBLANES, xbuf.shape[1]), F32)
        hcar[...] = jnp.zeros(hcar.shape, F32)

    u = u_ref[...]
    xbuf[SUBLANES:, :] = u
    cw = cw_ref[...]
    conv = cb_ref[...] + cw[CONV_WIDTH - 1:CONV_WIDTH] * u
    for j in range(CONV_WIDTH - 1):
        back = CONV_WIDTH - 1 - j
        conv = conv + cw[j:j + 1] * xbuf[SUBLANES - back:SUBLANES - back + tc, :]
    xbuf[0:SUBLANES, :] = u[tc - SUBLANES:, :]

    cbf = conv.astype(BF16)
    r = _sigmoid(jnp.dot(cbf, wa_ref[...], preferred_element_type=F32) + ba_ref[...])
    i = _sigmoid(jnp.dot(cbf, wx_ref[...], preferred_element_type=F32) + bx_ref[...])
    log_a = (-LRU_C) * r * _softplus(-lam_ref[...])
    a = jnp.exp(log_a)
    b = jnp.sqrt(-jnp.tanh(log_a) * (a * a + 1.0)) * (i * conv)

    row = lax.broadcasted_iota(jnp.int32, a.shape, 0)
    d = 1
    while d < tc:
        a_sh = pltpu.roll(a, d, 0)
        b_sh = pltpu.roll(b, d, 0)
        keep = row >= d
        b = jnp.where(keep, a * b_sh + b, b)
        a = jnp.where(keep, a * a_sh, a)
        d *= 2
    h = a * hcar[...] + b
    hcar[...] = h[tc - 1:tc, :]
    o_ref[...] = (h * jax.nn.gelu(gate_ref[...])).astype(o_ref.dtype)


def _lru_branch(z, conv_w, conv_b, w_a, b_a, w_x, b_x, lam, *, tc):
    bsz, seq, _ = z.shape
    heads, blk, _ = w_a.shape
    width = heads * blk
    assert blk % LANES == 0 and seq % tc == 0
    row = lambda v: v.reshape(1, width)
    vec_spec = pl.BlockSpec((1, blk), lambda b, h, t: (0, h))
    mat_spec = pl.BlockSpec((None, blk, blk), lambda b, h, t: (h, 0, 0))
    return pl.pallas_call(
        functools.partial(_lru_kernel, tc=tc),
        grid=(bsz, heads, seq // tc),
        in_specs=[
            pl.BlockSpec((None, tc, blk), lambda b, h, t: (b, t, h)),
            pl.BlockSpec((None, tc, blk), lambda b, h, t: (b, t, heads + h)),
            pl.BlockSpec((CONV_WIDTH, blk), lambda b, h, t: (0, h)),
            vec_spec, mat_spec, vec_spec, mat_spec, vec_spec, vec_spec,
        ],
        out_specs=pl.BlockSpec((None, tc, blk), lambda b, h, t: (b, t, h)),
        out_shape=jax.ShapeDtypeStruct((bsz, seq, width), BF16),
        scratch_shapes=[pltpu.VMEM((tc + SUBLANES, blk), F32), pltpu.VMEM((1, blk), F32)],
        compiler_params=_cparams(("parallel", "parallel", "arbitrary")),
        name="rglru",
    )(z, z, conv_w, row(conv_b), w_a.astype(BF16), row(b_a), w_x.astype(BF16), row(b_x), row(lam))


def _rwkv_kernel(r_ref, k_ref, v_ref, lora_ref, mu_r, mu_k, mu_v, mu_l, w0_ref, w2_ref, a0_ref, a2_ref,
                 g2_ref, kk_ref, ka_ref, rk_ref, lnw_ref, lnb_ref, o_ref,
                 rbuf, kbuf, vbuf, lbuf, state, *, tc, pp):
    L = RWKV_CHUNK
    P2 = 2 * L
    nc = tc // L
    assert P2 == LANES and RWKV_HEAD == L

    @pl.when(pl.program_id(2) == 0)
    def _():
        for buf in (rbuf, kbuf, vbuf, lbuf):
            buf[0:SUBLANES, :] = jnp.zeros((SUBLANES, buf.shape[1]), F32)
        state[...] = jnp.zeros(state.shape, F32)

    def token_shift(ref, buf, mu_ref):
        cur = ref[...]
        buf[SUBLANES:, :] = cur
        prev = buf[SUBLANES - 1:SUBLANES - 1 + tc, :]
        buf[0:SUBLANES, :] = cur[tc - SUBLANES:, :]
        return cur + (prev - cur) * mu_ref[...]

    r = token_shift(r_ref, rbuf, mu_r)
    k = token_shift(k_ref, kbuf, mu_k)
    v = token_shift(v_ref, vbuf, mu_v)
    lora = token_shift(lora_ref, lbuf, mu_l)

    dw = _dot(jnp.tanh(lora).astype(BF16), w2_ref[...])
    w = -_softplus(-(w0_ref[...] + dw)) - 0.5
    logw = -jnp.exp(w)
    a = _sigmoid(a0_ref[...] + _dot(lora.astype(BF16), a2_ref[...]))
    g = _dot(_sigmoid(lora).astype(BF16), g2_ref[...])

    ri = lax.broadcasted_iota(jnp.int32, (P2, P2), 0)
    ci = lax.broadcasted_iota(jnp.int32, (P2, P2), 1)
    blk = jnp.where(ri >= L, 1, 0) == jnp.where(ci >= L, 1, 0)
    head_ones = jnp.where(blk, 1.0, 0.0).astype(BF16)

    def to_rows(x):
        return jnp.concatenate([x[:, j * LANES:(j + 1) * LANES] for j in range(pp)], axis=0)

    def from_rows(x):
        return jnp.concatenate([x[j * tc:(j + 1) * tc, :] for j in range(pp)], axis=1)

    def head_sum(x):
        rows = to_rows(x)
        hi = rows.astype(BF16)
        lo = (rows - hi.astype(F32)).astype(BF16)
        return from_rows(_dot(hi, head_ones) + _dot(lo, head_ones))

    kk = k * kk_ref[...]
    kk = kk / jnp.maximum(jnp.sqrt(head_sum(kk * kk)), 1e-12)
    kmod = k * (1.0 + (a - 1.0) * ka_ref[...])
    kka = kk * a
    bonus = head_sum(r * kmod * rk_ref[...]) * v

    rt = lax.broadcasted_iota(jnp.int32, (tc, tc), 0)
    ct = lax.broadcasted_iota(jnp.int32, (tc, tc), 1)
    shift = L.bit_length() - 1
    same_chunk = lax.shift_right_logical(rt, shift) == lax.shift_right_logical(ct, shift)
    cum = _dot(jnp.where(same_chunk & (ct <= rt), 1.0, 0.0).astype(F32), logw, precision=HIGHEST)
    p_inv = jnp.exp(-cum)
    r_d = r * jnp.exp(cum)
    kk_d = kk * jnp.exp(cum - logw)
    k_d = kmod * p_inv
    a_d = kka * p_inv
    cum_last = jnp.concatenate(
        [jnp.broadcast_to(cum[(c + 1) * L - 1:(c + 1) * L, :], (L, cum.shape[1])) for c in range(nc)], axis=0)
    p_tail = jnp.exp(cum_last - cum)
    k_t = kmod * p_tail
    a_t = kka * p_tail
    p_last = jnp.exp(cum_last)

    strict = (blk & (ci < ri))[None]
    incl = (blk & (ci <= ri))[None]
    eye = jnp.where(ri == ci, 1.0, 0.0).astype(F32)[None]
    lane_row = lax.broadcasted_iota(jnp.int32, (1, LANES), 1)
    m0 = jnp.where(lane_row < RWKV_HEAD, 1.0, 0.0).astype(F32)
    m1 = 1.0 - m0

    def batch(x):
        out = []
        for c in range(nc):
            for j in range(pp):
                xs = x[c * L:(c + 1) * L, j * LANES:(j + 1) * LANES]
                out.append(jnp.concatenate([xs * m0, xs * m1], axis=0))
        return jnp.stack(out)

    r_b, kk_b, k_b, a_b, v_b, kt_b, at_b = (batch(x) for x in (r_d, kk_d, k_d, a_d, v, k_t, a_t))

    gram = _bdot(jnp.concatenate([kk_b, r_b], axis=1), jnp.concatenate([k_b, a_b], axis=1), _BNT)
    a_kk_k = jnp.where(strict, gram[:, 0:P2, 0:P2], 0.0)
    n_mat = jnp.where(strict, gram[:, 0:P2, P2:], 0.0)
    a_r_k = jnp.where(incl, gram[:, P2:, 0:P2], 0.0)
    a_r_a = jnp.where(incl, gram[:, P2:, P2:], 0.0)

    inv = eye - n_mat
    pw = _bdot(n_mat, n_mat, _BNN)
    steps = 1
    while True:
        inv = inv + _bdot(inv, pw, _BNN)
        steps *= 2
        if 2 * steps >= L:
            break
        pw = _bdot(pw, pw, _BNN)
    akk_v = _bdot(a_kk_k, v_b, _BNN)
    ark_v = _bdot(a_r_k, v_b, _BNN)

    s = state[...]
    y_chunks = []
    for c in range(nc):
        sl = slice(c * pp, (c + 1) * pp)
        sa = _bdot(inv[sl], _bdot(kk_b[sl], s, _BNT) + akk_v[sl], _BNN)
        y_s = _bdot(r_b[sl], s, _BNT) + ark_v[sl] - _bdot(a_r_a[sl], sa, _BNN)
        y_c = y_s[:, 0:L, :] + y_s[:, L:, :]
        y_chunks.append(jnp.concatenate([y_c[j] for j in range(pp)], axis=1))
        decay = jnp.stack([p_last[c * L:c * L + 1, j * LANES:(j + 1) * LANES] for j in range(pp)])
        s = s * decay + _bdot(jnp.concatenate([v_b[sl], sa], axis=1),
                              jnp.concatenate([kt_b[sl], -at_b[sl]], axis=1), _BTN)
    state[...] = s

    y = jnp.concatenate(y_chunks, axis=0)
    inv_n = 1.0 / RWKV_HEAD
    mean = head_sum(y) * inv_n
    yc = y - mean
    var = head_sum(yc * yc) * inv_n
    yn = yc * lax.rsqrt(var + GN_EPS) * lnw_ref[...] + lnb_ref[...]
    o_ref[...] = ((yn + bonus) * g).astype(o_ref.dtype)


def _rwkv_branch(z, col0, z_lora, mu, w0, w2, a0, a2, g2, k_k, k_a, r_k, ln_w, ln_b, *, tc, pp):
    bsz, seq, _ = z.shape
    width = w0.shape[0]
    lora_w = z_lora.shape[2]
    bw = pp * LANES
    d_lora, a_lora, g_lora = w2.shape[0], a2.shape[0], g2.shape[0]
    assert width % bw == 0 and col0 % bw == 0 and lora_w % LANES == 0
    assert lora_w >= d_lora + a_lora + g_lora
    assert seq % tc == 0 and tc % RWKV_CHUNK == 0 and tc % SUBLANES == 0
    groups = width // bw
    cb = col0 // bw

    def pad_rows(wm, start):
        return jnp.zeros((lora_w, width), F32).at[start:start + wm.shape[0]].set(wm).astype(BF16)

    w2p = pad_rows(w2, 0)
    a2p = pad_rows(a2, d_lora)
    g2p = pad_rows(g2, d_lora + a_lora)
    mu_pad = jnp.zeros((1, 3 * width + lora_w), F32).at[0, :mu.shape[0]].set(mu)
    mu_r, mu_k, mu_v = (mu_pad[:, i * width:(i + 1) * width] for i in range(3))
    mu_l = mu_pad[:, 3 * width:]
    row = lambda x: x.reshape(1, width)

    act = lambda off: pl.BlockSpec((None, tc, bw), lambda b, p, t: (b, t, cb + off * groups + p))
    vec = pl.BlockSpec((1, bw), lambda b, p, t: (0, p))
    lmat = pl.BlockSpec((lora_w, bw), lambda b, p, t: (0, p))
    return pl.pallas_call(
        functools.partial(_rwkv_kernel, tc=tc, pp=pp),
        grid=(bsz, groups, seq // tc),
        in_specs=[
            act(0), act(1), act(2),
            pl.BlockSpec((None, tc, lora_w), lambda b, p, t: (b, t, 0)),
            vec, vec, vec,
            pl.BlockSpec((1, lora_w), lambda b, p, t: (0, 0)),
            vec, lmat, vec, lmat, lmat, vec, vec, vec, vec, vec,
        ],
        out_specs=pl.BlockSpec((None, tc, bw), lambda b, p, t: (b, t, p)),
        out_shape=jax.ShapeDtypeStruct((bsz, seq, width), BF16),
        scratch_shapes=[
            pltpu.VMEM((tc + SUBLANES, bw), F32), pltpu.VMEM((tc + SUBLANES, bw), F32),
            pltpu.VMEM((tc + SUBLANES, bw), F32), pltpu.VMEM((tc + SUBLANES, lora_w), F32),
            pltpu.VMEM((pp, LANES, LANES), F32),
        ],
        compiler_params=_cparams(("parallel", "parallel", "arbitrary")),
        name="rwkv7",
    )(z, z, z, z_lora, mu_r, mu_k, mu_v, mu_l, row(w0), w2p, row(a0), a2p, g2p, row(k_k), row(k_a),
      row(r_k), row(ln_w), row(ln_b))


def _mlstm_kernel(q_ref, k_ref, v_ref, zo_ref, gcol_ref, grow_ref, bcol_ref, brow_ref, nw_ref, o_ref,
                  c_st, n_st, m_st, *, heads, dqk, dv):
    L = q_ref.shape[0]

    @pl.when(pl.program_id(1) == 0)
    def _():
        c_st[...] = jnp.zeros(c_st.shape, F32)
        n_st[...] = jnp.zeros(n_st.shape, F32)
        m_st[...] = jnp.zeros(m_st.shape, F32)

    ri = lax.broadcasted_iota(jnp.int32, (L, L), 0)
    ci = lax.broadcasted_iota(jnp.int32, (L, L), 1)
    causal = ci <= ri
    tril_ones = jnp.where(causal, 1.0, 0.0).astype(F32)
    triu_ones = jnp.where(ci >= ri, 1.0, 0.0).astype(F32)

    gc = gcol_ref[...] + bcol_ref[...]
    gr = grow_ref[...] + brow_ref[...]
    b_c = _dot(tril_ones, -_softplus(-gc), precision=HIGHEST)
    b_r = _dot(-_softplus(-gr), triu_ones, precision=HIGHEST)

    scale = dqk ** -0.5
    for h in range(heads):
        q = (q_ref[:, h * dqk:(h + 1) * dqk] * scale).astype(BF16)
        k = k_ref[:, h * dqk:(h + 1) * dqk]
        v = v_ref[:, h * dv:(h + 1) * dv].astype(BF16)
        bc, lic = b_c[:, heads + h:heads + h + 1], gc[:, h:h + 1]
        br, lir = b_r[heads + h:heads + h + 1, :], gr[h:h + 1, :]
        m_prev = m_st[h][0:1, 0:1]
        c_mat = c_st[h]
        n_vec = n_st[h]

        log_d = jnp.where(causal, bc - br + lir, -jnp.inf)
        log_inter = bc + m_prev
        m_t = jnp.maximum(jnp.max(log_d, axis=-1, keepdims=True), log_inter)
        p = _dot(q, k.astype(BF16), _NT) * jnp.exp(log_d - m_t)
        inter = jnp.exp(log_inter - m_t)
        num = _dot(p.astype(BF16), v) + inter * _dot(q, c_mat.astype(BF16))
        den = (jnp.sum(p, axis=-1, keepdims=True)
               + inter * jnp.sum(q.astype(F32) * n_vec, axis=-1, keepdims=True))
        hs = num / jnp.maximum(jnp.abs(den), jnp.exp(-m_t))

        b_last = bc[L - 1:L, :]
        log_g = b_last - bc + lic
        m_new = jnp.maximum(b_last + m_prev, jnp.max(log_g, axis=0, keepdims=True))
        carry = jnp.exp(b_last + m_prev - m_new)
        wk = k * jnp.exp(log_g - m_new)
        c_st[h] = carry * c_mat + _dot(wk.astype(BF16), v, _TN)
        n_st[h] = carry * n_vec + jnp.sum(wk, axis=0, keepdims=True)
        m_st[h] = jnp.broadcast_to(m_new, m_st.shape[1:])

        hs = hs * lax.rsqrt(jnp.mean(hs * hs, axis=-1, keepdims=True) + NORM_EPS)
        sl = slice(h * dv, (h + 1) * dv)
        o_ref[:, sl] = (hs * nw_ref[:, sl] * _sigmoid(zo_ref[:, sl])).astype(o_ref.dtype)


def _mlstm_branch(z, gates, b_i, b_f, norm_w, *, heads, qk, vdim):
    bsz, seq, _ = z.shape
    L = ML_CHUNK_ROWS
    dqk, dv = qk // heads, vdim // heads
    assert seq % L == 0 and vdim % qk == 0 and qk % LANES == 0 and (2 * qk) % vdim == 0
    vblk = 2 * qk // vdim
    bias = jnp.concatenate([b_i, b_f]).astype(F32)
    bias_col = jnp.zeros((1, LANES), F32).at[0, :2 * heads].set(bias)
    gates_row = jnp.swapaxes(gates[:, :, :2 * heads], 1, 2)
    return pl.pallas_call(
        functools.partial(_mlstm_kernel, heads=heads, dqk=dqk, dv=dv),
        grid=(bsz, seq // L),
        in_specs=[
            pl.BlockSpec((None, L, qk), lambda b, c: (b, c, 0)),
            pl.BlockSpec((None, L, qk), lambda b, c: (b, c, 1)),
            pl.BlockSpec((None, L, vdim), lambda b, c: (b, c, vblk)),
            pl.BlockSpec((None, L, vdim), lambda b, c: (b, c, vblk + 1)),
            pl.BlockSpec((None, L, LANES), lambda b, c: (b, c, 0)),
            pl.BlockSpec((None, 2 * heads, L), lambda b, c: (b, 0, c)),
            pl.BlockSpec((1, LANES), lambda b, c: (0, 0)),
            pl.BlockSpec((2 * heads, 1), lambda b, c: (0, 0)),
            pl.BlockSpec((1, vdim), lambda b, c: (0, 0)),
        ],
        out_specs=pl.BlockSpec((None, L, vdim), lambda b, c: (b, c, 0)),
        out_shape=jax.ShapeDtypeStruct((bsz, seq, vdim), BF16),
        scratch_shapes=[
            pltpu.VMEM((heads, dqk, dv), F32),
            pltpu.VMEM((heads, 1, dqk), F32),
            pltpu.VMEM((heads, SUBLANES, LANES), F32),
        ],
        compiler_params=_cparams(("parallel", "arbitrary")),
        name="mlstm",
    )(z, z, z, z, gates, gates_row, bias_col, bias.reshape(-1, 1), norm_w.reshape(1, vdim))


def _col_tile(n, cap):
    for step in (MXU_WIDTH, LANES):
        fits = [c for c in range(step, cap + 1, step) if n % c == 0]
        if fits:
            return fits[-1]
    raise ValueError(f"no lane-aligned column tile divides {n}")


def _tail_weight(w, start):
    tail = w[:, :, start:]
    pad = (-tail.shape[2]) % LANES
    return jnp.pad(tail, ((0, 0), (0, 0), (0, pad)))


def kernel(x, norm_mix, norm_mlp, norm_final, mlp_up, mlp_down, hy_in, lru_conv_w, lru_conv_b, lru_wa, lru_ba, lru_wx, lru_bx, lru_lam, rwkv_mu, rwkv_w0, rwkv_w2, rwkv_a0, rwkv_a2, rwkv_g2, rwkv_kk, rwkv_ka, rwkv_rk, rwkv_ln_w, rwkv_ln_b, hy_out, ml_in, ml_bi, ml_bf, ml_norm, ml_out):
    bsz, seq, d = x.shape
    t = bsz * seq
    depth = norm_mix.shape[0]
    xf = x.reshape(t, d)

    lru_w = lru_lam.shape[1]
    rw_w = rwkv_w0.shape[1]
    hy_main = 2 * lru_w + 3 * rw_w
    heads = ml_bi.shape[1]
    vdim = ml_norm.shape[1]
    qk = (ml_in.shape[2] - 2 * vdim - 2 * heads) // 2
    ml_main = 2 * qk + 2 * vdim

    hy_lora, ml_gate = _tail_weight(hy_in, hy_main), _tail_weight(ml_in, ml_main)
    tn = lambda n: _col_tile(n, COL_TILE)
    tk_res = min(K_TILE, d)

    for layer in range(depth):
        hn = _norm_cast(xf, norm_mix[layer], BF16)
        if layer % 2 == 0:
            e = layer // 2
            z = _matmul([hn], hy_in, e, hy_main, tk=d, tn=tn(hy_main)).reshape(bsz, seq, hy_main)
            z_lora = _matmul([hn], hy_lora, e, hy_lora.shape[2], tk=d, tn=tn(hy_lora.shape[2]))
            z_lora = z_lora.reshape(bsz, seq, -1)
            ya = _lru_branch(z, lru_conv_w[e], lru_conv_b[e], lru_wa[e], lru_ba[e], lru_wx[e], lru_bx[e],
                             lru_lam[e], tc=min(LRU_ROWS, seq))
            yb = _rwkv_branch(z, 2 * lru_w, z_lora, rwkv_mu[e], rwkv_w0[e], rwkv_w2[e], rwkv_a0[e], rwkv_a2[e],
                              rwkv_g2[e], rwkv_kk[e], rwkv_ka[e], rwkv_rk[e], rwkv_ln_w[e], rwkv_ln_b[e],
                              tc=min(RWKV_ROWS, seq), pp=min(RWKV_PAIRS, rw_w // LANES))
            lhs = [ya.reshape(t, lru_w), yb.reshape(t, rw_w)]
            assert lru_w == rw_w
            xf = _matmul(lhs, hy_out, e, d, tk=lru_w, tn=tn(d), res=xf)
        else:
            o = layer // 2
            z = _matmul([hn], ml_in, o, ml_main, tk=d, tn=tn(ml_main)).reshape(bsz, seq, ml_main)
            gates = _matmul([hn], ml_gate, o, ml_gate.shape[2], tk=d, tn=tn(ml_gate.shape[2]))
            gates = gates.reshape(bsz, seq, -1)
            hs = _mlstm_branch(z, gates, ml_bi[o], ml_bf[o], ml_norm[o], heads=heads, qk=qk, vdim=vdim)
            xf = _matmul([hs.reshape(t, vdim)], ml_out, o, d, tk=min(K_TILE, vdim), tn=tn(d), res=xf)
        hn = _norm_cast(xf, norm_mlp[layer], BF16)
        u = _matmul([hn], mlp_up, layer, mlp_up.shape[2], tk=d, tn=tn(mlp_up.shape[2]), out_dtype=BF16,
                    relu2=True)
        xf = _matmul([u], mlp_down, layer, d, tk=tk_res, tn=tn(d), res=xf)
    return _norm_cast(xf, norm_final, F32).reshape(bsz, seq, d)
```

```python
import functools

import jax
import jax.numpy as jnp
from jax import lax
from jax.experimental import pallas as pl
from jax.experimental.pallas import tpu as pltpu

F32 = jnp.float32
BF16 = jnp.bfloat16
HIGHEST = lax.Precision.HIGHEST

NORM_EPS = 1e-6
GN_EPS = 64e-5
LRU_C = 8.0
CONV_WIDTH = 4
RWKV_HEAD = 64
LANES = 128
SUBLANES = 8
MXU_WIDTH = 256
VMEM_LIMIT_BYTES = 60 * 1024 * 1024
RWKV_CHUNK = 64
RWKV_ROWS = 128
RWKV_PAIRS = 16
LRU_ROWS = 256
ML_CHUNK_ROWS = 128
NORM_ROWS = 256
ROW_TILE = 2048
COL_TILE = 512
RES_COL_TILE = 1024
K_TILE = 1024


def _cparams(semantics):
    return pltpu.CompilerParams(dimension_semantics=semantics, vmem_limit_bytes=VMEM_LIMIT_BYTES)


def _softplus(x):
    return jnp.maximum(x, 0.0) + jnp.log1p(jnp.exp(-jnp.abs(x)))


def _sigmoid(x):
    return 1.0 / (1.0 + jnp.exp(-x))


_NN = (((1,), (0,)), ((), ()))
_NT = (((1,), (1,)), ((), ()))
_TN = (((0,), (0,)), ((), ()))
_BNN = (((2,), (1,)), ((0,), (0,)))
_BNT = (((2,), (2,)), ((0,), (0,)))
_BTN = (((1,), (1,)), ((0,), (0,)))


def _dot(a, b, dims=_NN, precision=None):
    return lax.dot_general(a, b, dims, precision=precision, preferred_element_type=F32)


def _bdot(a, b, dims=_NN):
    return lax.dot_general(a.astype(BF16), b.astype(BF16), dims, preferred_element_type=F32)


def _norm_cast_kernel(x_ref, g_ref, o_ref):
    x = x_ref[...]
    ms = jnp.mean(x * x, axis=-1, keepdims=True)
    o_ref[...] = (x * lax.rsqrt(ms + NORM_EPS) * g_ref[...]).astype(o_ref.dtype)


def _norm_cast(x, g, out_dtype):
    t, d = x.shape
    tm = min(NORM_ROWS, t)
    return pl.pallas_call(
        _norm_cast_kernel,
        grid=(t // tm,),
        in_specs=[pl.BlockSpec((tm, d), lambda i: (i, 0)), pl.BlockSpec((1, d), lambda i: (0, 0))],
        out_specs=pl.BlockSpec((tm, d), lambda i: (i, 0)),
        out_shape=jax.ShapeDtypeStruct((t, d), out_dtype),
        compiler_params=_cparams(("parallel",)),
        name="rmsnorm",
    )(x, g.reshape(1, d))


def _matmul_kernel(*refs, n_lhs, nk, has_res, relu2):
    a_refs, w_ref = refs[:n_lhs], refs[n_lhs]
    r_ref = refs[n_lhs + 1] if has_res else None
    o_ref = refs[n_lhs + 1 + has_res]
    acc_ref = refs[-1] if (not has_res and nk > 1) else None
    kk = pl.program_id(2)

    def finish(acc):
        if relu2:
            acc = jnp.maximum(acc, 0.0)
            acc = acc * acc
        return acc.astype(o_ref.dtype)

    def accumulate(a_ref):
        def part():
            return jnp.dot(a_ref[...], w_ref[...].astype(BF16), preferred_element_type=F32)

        if has_res:
            @pl.when(kk == 0)
            def _():
                o_ref[...] = r_ref[...] + part()

            @pl.when(kk > 0)
            def _():
                o_ref[...] += part()
        elif nk == 1:
            o_ref[...] = finish(part())
        else:
            @pl.when(kk == 0)
            def _():
                acc_ref[...] = part()

            @pl.when((kk > 0) & (kk < nk - 1))
            def _():
                acc_ref[...] += part()

            @pl.when(kk == nk - 1)
            def _():
                o_ref[...] = finish(acc_ref[...] + part())

    if n_lhs == 1:
        accumulate(a_refs[0])
    else:
        per = nk // n_lhs
        for idx, a_ref in enumerate(a_refs):
            pl.when((kk >= idx * per) & (kk < (idx + 1) * per))(functools.partial(accumulate, a_ref))


def _matmul(lhs, w, layer, n, *, tk, tn, tm=ROW_TILE, out_dtype=F32, res=None, relu2=False):
    t = lhs[0].shape[0]
    k = w.shape[1]
    tm = min(tm, t)
    nk = k // tk
    assert t % tm == 0 and k % tk == 0 and n % tn == 0
    if len(lhs) == 1:
        once = dict(pipeline_mode=pl.Buffered(1)) if nk == 1 else {}
        a_specs = [pl.BlockSpec((tm, tk), lambda i, j, kk: (i, kk), **once)]
    else:
        per = nk // len(lhs)
        assert all(a.shape[1] == per * tk for a in lhs) and per * len(lhs) == nk

        def lhs_spec(idx):
            return pl.BlockSpec((tm, tk), lambda i, j, kk: (i, jnp.clip(kk - idx * per, 0, per - 1)))

        a_specs = [lhs_spec(idx) for idx in range(len(lhs))]
    has_res = res is not None
    assert not (has_res and (relu2 or out_dtype != F32))
    tile = pl.BlockSpec((tm, tn), lambda i, j, kk: (i, j))
    scratch = [pltpu.VMEM((tm, tn), F32)] if (not has_res and nk > 1) else []
    return pl.pallas_call(
        functools.partial(_matmul_kernel, n_lhs=len(lhs), nk=nk, has_res=has_res, relu2=relu2),
        grid=(t // tm, n // tn, nk),
        in_specs=a_specs + [pl.BlockSpec((None, tk, tn), lambda i, j, kk: (layer, kk, j))]
        + ([tile] if has_res else []),
        out_specs=tile,
        out_shape=jax.ShapeDtypeStruct((t, n), out_dtype),
        scratch_shapes=scratch,
        compiler_params=_cparams(("parallel", "parallel", "arbitrary")),
        name="matmul",
    )(*lhs, w, *([res] if has_res else []))


def _lru_kernel(u_ref, gate_ref, cw_ref, cb_ref, wa_ref, ba_ref, wx_ref, bx_ref, lam_ref,
                o_ref, xbuf, hcar, *, tc):
    @pl.when(pl.program_id(2) == 0)
    def _():
        xbuf[0:SUBLANES, :] = jnp.zeros((SUBLANES, xbuf.shape[1]), F32)
        hcar[...] = jnp.zeros(hcar.shape, F32)

    u = u_ref[...].astype(F32)
    xbuf[SUBLANES:, :] = u
    cw = cw_ref[...]
    conv = cb_ref[...] + cw[CONV_WIDTH - 1:CONV_WIDTH] * u
    for j in range(CONV_WIDTH - 1):
        back = CONV_WIDTH - 1 - j
        conv = conv + cw[j:j + 1] * xbuf[SUBLANES - back:SUBLANES - back + tc, :]
    xbuf[0:SUBLANES, :] = u[tc - SUBLANES:, :]

    cbf = conv.astype(BF16)
    r = _sigmoid(jnp.dot(cbf, wa_ref[...], preferred_element_type=F32) + ba_ref[...])
    i = _sigmoid(jnp.dot(cbf, wx_ref[...], preferred_element_type=F32) + bx_ref[...])
    log_a = (-LRU_C) * r * _softplus(-lam_ref[...])
    a = jnp.exp(log_a)
    b = jnp.sqrt(-jnp.tanh(log_a) * (a * a + 1.0)) * (i * conv)

    row = lax.broadcasted_iota(jnp.int32, a.shape, 0)
    d = 1
    while d < tc:
        a_sh = pltpu.roll(a, d, 0)
        b_sh = pltpu.roll(b, d, 0)
        keep = row >= d
        b = jnp.where(keep, a * b_sh + b, b)
        a = jnp.where(keep, a * a_sh, a)
        d *= 2
    h = a * hcar[...] + b
    hcar[...] = h[tc - 1:tc, :]
    o_ref[...] = (h * jax.nn.gelu(gate_ref[...].astype(F32))).astype(o_ref.dtype)


def _lru_branch(z, conv_w, conv_b, w_a, b_a, w_x, b_x, lam, *, tc):
    bsz, seq, _ = z.shape
    heads, blk, _ = w_a.shape
    width = heads * blk
    assert blk % LANES == 0 and seq % tc == 0
    row = lambda v: v.reshape(1, width)
    vec_spec = pl.BlockSpec((1, blk), lambda b, h, t: (0, h))
    mat_spec = pl.BlockSpec((None, blk, blk), lambda b, h, t: (h, 0, 0))
    return pl.pallas_call(
        functools.partial(_lru_kernel, tc=tc),
        grid=(bsz, heads, seq // tc),
        in_specs=[
            pl.BlockSpec((None, tc, blk), lambda b, h, t: (b, t, h)),
            pl.BlockSpec((None, tc, blk), lambda b, h, t: (b, t, heads + h)),
            pl.BlockSpec((CONV_WIDTH, blk), lambda b, h, t: (0, h)),
            vec_spec, mat_spec, vec_spec, mat_spec, vec_spec, vec_spec,
        ],
        out_specs=pl.BlockSpec((None, tc, blk), lambda b, h, t: (b, t, h)),
        out_shape=jax.ShapeDtypeStruct((bsz, seq, width), BF16),
        scratch_shapes=[pltpu.VMEM((tc + SUBLANES, blk), F32), pltpu.VMEM((1, blk), F32)],
        compiler_params=_cparams(("parallel", "parallel", "arbitrary")),
        name="rglru",
    )(z, z, conv_w, row(conv_b), w_a.astype(BF16), row(b_a), w_x.astype(BF16), row(b_x), row(lam))


def _rwkv_kernel(r_ref, k_ref, v_ref, lora_ref, mu_r, mu_k, mu_v, mu_l, w0_ref, w2_ref, a0_ref, a2_ref,
                 g2_ref, kk_ref, ka_ref, rk_ref, lnw_ref, lnb_ref, o_ref,
                 rbuf, kbuf, vbuf, lbuf, state, *, tc, pp):
    L = RWKV_CHUNK
    P2 = 2 * L
    nc = tc // L
    assert P2 == LANES and RWKV_HEAD == L

    @pl.when(pl.program_id(2) == 0)
    def _():
        for buf in (rbuf, kbuf, vbuf, lbuf):
            buf[0:SUBLANES, :] = jnp.zeros((SUBLANES, buf.shape[1]), F32)
        state[...] = jnp.zeros(state.shape, F32)

    def token_shift(ref, buf, mu_ref):
        cur = ref[...].astype(F32)
        buf[SUBLANES:, :] = cur
        prev = buf[SUBLANES - 1:SUBLANES - 1 + tc, :]
        buf[0:SUBLANES, :] = cur[tc - SUBLANES:, :]
        return cur + (prev - cur) * mu_ref[...]

    r = token_shift(r_ref, rbuf, mu_r)
    k = token_shift(k_ref, kbuf, mu_k)
    v = token_shift(v_ref, vbuf, mu_v)
    lora = token_shift(lora_ref, lbuf, mu_l)

    dw = _dot(jnp.tanh(lora).astype(BF16), w2_ref[...])
    w = -_softplus(-(w0_ref[...] + dw)) - 0.5
    logw = -jnp.exp(w)
    a = _sigmoid(a0_ref[...] + _dot(lora.astype(BF16), a2_ref[...]))
    g = _dot(_sigmoid(lora).astype(BF16), g2_ref[...])

    ri = lax.broadcasted_iota(jnp.int32, (P2, P2), 0)
    ci = lax.broadcasted_iota(jnp.int32, (P2, P2), 1)
    blk = jnp.where(ri >= L, 1, 0) == jnp.where(ci >= L, 1, 0)
    head_ones = jnp.where(blk, 1.0, 0.0).astype(BF16)

    def to_rows(x):
        return jnp.concatenate([x[:, j * LANES:(j + 1) * LANES] for j in range(pp)], axis=0)

    def from_rows(x):
        return jnp.concatenate([x[j * tc:(j + 1) * tc, :] for j in range(pp)], axis=1)

    def head_sum(x):
        rows = to_rows(x)
        hi = rows.astype(BF16)
        lo = (rows - hi.astype(F32)).astype(BF16)
        return from_rows(_dot(hi, head_ones) + _dot(lo, head_ones))

    kk = k * kk_ref[...]
    kk = kk / jnp.maximum(jnp.sqrt(head_sum(kk * kk)), 1e-12)
    kmod = k * (1.0 + (a - 1.0) * ka_ref[...])
    kka = kk * a
    bonus = head_sum(r * kmod * rk_ref[...]) * v

    rt = lax.broadcasted_iota(jnp.int32, (tc, tc), 0)
    ct = lax.broadcasted_iota(jnp.int32, (tc, tc), 1)
    shift = L.bit_length() - 1
    same_chunk = lax.shift_right_logical(rt, shift) == lax.shift_right_logical(ct, shift)
    cum = _dot(jnp.where(same_chunk & (ct <= rt), 1.0, 0.0).astype(F32), logw, precision=HIGHEST)
    p_inv = jnp.exp(-cum)
    r_d = r * jnp.exp(cum)
    kk_d = kk * jnp.exp(cum - logw)
    k_d = kmod * p_inv
    a_d = kka * p_inv
    cum_last = jnp.concatenate(
        [jnp.broadcast_to(cum[(c + 1) * L - 1:(c + 1) * L, :], (L, cum.shape[1])) for c in range(nc)], axis=0)
    p_tail = jnp.exp(cum_last - cum)
    k_t = kmod * p_tail
    a_t = kka * p_tail
    p_last = jnp.exp(cum_last)

    strict = (blk & (ci < ri))[None]
    incl = (blk & (ci <= ri))[None]
    eye = jnp.where(ri == ci, 1.0, 0.0).astype(F32)[None]
    lane_row = lax.broadcasted_iota(jnp.int32, (1, LANES), 1)
    m0 = jnp.where(lane_row < RWKV_HEAD, 1.0, 0.0).astype(F32)
    m1 = 1.0 - m0

    def batch(x):
        out = []
        for c in range(nc):
            for j in range(pp):
                xs = x[c * L:(c + 1) * L, j * LANES:(j + 1) * LANES]
                out.append(jnp.concatenate([xs * m0, xs * m1], axis=0))
        return jnp.stack(out)

    r_b, kk_b, k_b, a_b, v_b, kt_b, at_b = (batch(x) for x in (r_d, kk_d, k_d, a_d, v, k_t, a_t))

    gram = _bdot(jnp.concatenate([kk_b, r_b], axis=1), jnp.concatenate([k_b, a_b], axis=1), _BNT)
    a_kk_k = jnp.where(strict, gram[:, 0:P2, 0:P2], 0.0)
    n_mat = jnp.where(strict, gram[:, 0:P2, P2:], 0.0)
    a_r_k = jnp.where(incl, gram[:, P2:, 0:P2], 0.0)
    a_r_a = jnp.where(incl, gram[:, P2:, P2:], 0.0)

    inv = eye - n_mat
    pw = _bdot(n_mat, n_mat, _BNN)
    steps = 1
    while True:
        inv = inv + _bdot(inv, pw, _BNN)
        steps *= 2
        if 2 * steps >= L:
            break
        pw = _bdot(pw, pw, _BNN)
    akk_v = _bdot(a_kk_k, v_b, _BNN)
    ark_v = _bdot(a_r_k, v_b, _BNN)

    s = state[...]
    y_chunks = []
    for c in range(nc):
        sl = slice(c * pp, (c + 1) * pp)
        sa = _bdot(inv[sl], _bdot(kk_b[sl], s, _BNT) + akk_v[sl], _BNN)
        y_s = _bdot(r_b[sl], s, _BNT) + ark_v[sl] - _bdot(a_r_a[sl], sa, _BNN)
        y_c = y_s[:, 0:L, :] + y_s[:, L:, :]
        y_chunks.append(jnp.concatenate([y_c[j] for j in range(pp)], axis=1))
        decay = jnp.stack([p_last[c * L:c * L + 1, j * LANES:(j + 1) * LANES] for j in range(pp)])
        s = s * decay + _bdot(jnp.concatenate([v_b[sl], sa], axis=1),
                              jnp.concatenate([kt_b[sl], -at_b[sl]], axis=1), _BTN)
    state[...] = s

    y = jnp.concatenate(y_chunks, axis=0)
    inv_n = 1.0 / RWKV_HEAD
    mean = head_sum(y) * inv_n
    yc = y - mean
    var = head_sum(yc * yc) * inv_n
    yn = yc * lax.rsqrt(var + GN_EPS) * lnw_ref[...] + lnb_ref[...]
    o_ref[...] = ((yn + bonus) * g).astype(o_ref.dtype)


def _rwkv_branch(z, col0, z_lora, mu, w0, w2, a0, a2, g2, k_k, k_a, r_k, ln_w, ln_b, *, tc, pp):
    bsz, seq, _ = z.shape
    width = w0.shape[0]
    lora_w = z_lora.shape[2]
    bw = pp * LANES
    d_lora, a_lora, g_lora = w2.shape[0], a2.shape[0], g2.shape[0]
    assert width % bw == 0 and col0 % bw == 0 and lora_w % LANES == 0
    assert lora_w >= d_lora + a_lora + g_lora
    assert seq % tc == 0 and tc % RWKV_CHUNK == 0 and tc % SUBLANES == 0
    groups = width // bw
    cb = col0 // bw

    def pad_rows(wm, start):
        return jnp.zeros((lora_w, width), F32).at[start:start + wm.shape[0]].set(wm).astype(BF16)

    w2p = pad_rows(w2, 0)
    a2p = pad_rows(a2, d_lora)
    g2p = pad_rows(g2, d_lora + a_lora)
    mu_pad = jnp.zeros((1, 3 * width + lora_w), F32).at[0, :mu.shape[0]].set(mu)
    mu_r, mu_k, mu_v = (mu_pad[:, i * width:(i + 1) * width] for i in range(3))
    mu_l = mu_pad[:, 3 * width:]
    row = lambda x: x.reshape(1, width)

    act = lambda off: pl.BlockSpec((None, tc, bw), lambda b, p, t: (b, t, cb + off * groups + p))
    vec = pl.BlockSpec((1, bw), lambda b, p, t: (0, p))
    lmat = pl.BlockSpec((lora_w, bw), lambda b, p, t: (0, p))
    return pl.pallas_call(
        functools.partial(_rwkv_kernel, tc=tc, pp=pp),
        grid=(bsz, groups, seq // tc),
        in_specs=[
            act(0), act(1), act(2),
            pl.BlockSpec((None, tc, lora_w), lambda b, p, t: (b, t, 0)),
            vec, vec, vec,
            pl.BlockSpec((1, lora_w), lambda b, p, t: (0, 0)),
            vec, lmat, vec, lmat, lmat, vec, vec, vec, vec, vec,
        ],
        out_specs=pl.BlockSpec((None, tc, bw), lambda b, p, t: (b, t, p)),
        out_shape=jax.ShapeDtypeStruct((bsz, seq, width), BF16),
        scratch_shapes=[
            pltpu.VMEM((tc + SUBLANES, bw), F32), pltpu.VMEM((tc + SUBLANES, bw), F32),
            pltpu.VMEM((tc + SUBLANES, bw), F32), pltpu.VMEM((tc + SUBLANES, lora_w), F32),
            pltpu.VMEM((pp, LANES, LANES), F32),
        ],
        compiler_params=_cparams(("parallel", "parallel", "arbitrary")),
        name="rwkv7",
    )(z, z, z, z_lora, mu_r, mu_k, mu_v, mu_l, row(w0), w2p, row(a0), a2p, g2p, row(k_k), row(k_a),
      row(r_k), row(ln_w), row(ln_b))


def _mlstm_kernel(q_ref, k_ref, v_ref, zo_ref, gcol_ref, grow_ref, bcol_ref, brow_ref, nw_ref, o_ref,
                  c_st, n_st, m_st, *, heads, dqk, dv):
    L = q_ref.shape[0]

    @pl.when(pl.program_id(1) == 0)
    def _():
        c_st[...] = jnp.zeros(c_st.shape, F32)
        n_st[...] = jnp.zeros(n_st.shape, F32)
        m_st[...] = jnp.zeros(m_st.shape, F32)

    ri = lax.broadcasted_iota(jnp.int32, (L, L), 0)
    ci = lax.broadcasted_iota(jnp.int32, (L, L), 1)
    causal = ci <= ri
    tril_ones = jnp.where(causal, 1.0, 0.0).astype(F32)
    triu_ones = jnp.where(ci >= ri, 1.0, 0.0).astype(F32)

    gc = gcol_ref[...] + bcol_ref[...]
    gr = grow_ref[...] + brow_ref[...]
    b_c = _dot(tril_ones, -_softplus(-gc), precision=HIGHEST)
    b_r = _dot(-_softplus(-gr), triu_ones, precision=HIGHEST)

    scale = dqk ** -0.5
    for h in range(heads):
        q = (q_ref[:, h * dqk:(h + 1) * dqk].astype(F32) * scale).astype(BF16)
        k = k_ref[:, h * dqk:(h + 1) * dqk].astype(F32)
        v = v_ref[:, h * dv:(h + 1) * dv].astype(BF16)
        bc, lic = b_c[:, heads + h:heads + h + 1], gc[:, h:h + 1]
        br, lir = b_r[heads + h:heads + h + 1, :], gr[h:h + 1, :]
        m_prev = m_st[h][0:1, 0:1]
        c_mat = c_st[h]
        n_vec = n_st[h]

        log_d = jnp.where(causal, bc - br + lir, -jnp.inf)
        log_inter = bc + m_prev
        m_t = jnp.maximum(jnp.max(log_d, axis=-1, keepdims=True), log_inter)
        p = _dot(q, k.astype(BF16), _NT) * jnp.exp(log_d - m_t)
        inter = jnp.exp(log_inter - m_t)
        num = _dot(p.astype(BF16), v) + inter * _dot(q, c_mat.astype(BF16))
        den = (jnp.sum(p, axis=-1, keepdims=True)
               + inter * jnp.sum(q.astype(F32) * n_vec, axis=-1, keepdims=True))
        hs = num / jnp.maximum(jnp.abs(den), jnp.exp(-m_t))

        b_last = bc[L - 1:L, :]
        log_g = b_last - bc + lic
        m_new = jnp.maximum(b_last + m_prev, jnp.max(log_g, axis=0, keepdims=True))
        carry = jnp.exp(b_last + m_prev - m_new)
        wk = k * jnp.exp(log_g - m_new)
        c_st[h] = carry * c_mat + _dot(wk.astype(BF16), v, _TN)
        n_st[h] = carry * n_vec + jnp.sum(wk, axis=0, keepdims=True)
        m_st[h] = jnp.broadcast_to(m_new, m_st.shape[1:])

        hs = hs * lax.rsqrt(jnp.mean(hs * hs, axis=-1, keepdims=True) + NORM_EPS)
        sl = slice(h * dv, (h + 1) * dv)
        o_ref[:, sl] = (hs * nw_ref[:, sl] * _sigmoid(zo_ref[:, sl].astype(F32))).astype(o_ref.dtype)


def _mlstm_branch(z, gates, b_i, b_f, norm_w, *, heads, qk, vdim):
    bsz, seq, _ = z.shape
    L = ML_CHUNK_ROWS
    dqk, dv = qk // heads, vdim // heads
    assert seq % L == 0 and vdim % qk == 0 and qk % LANES == 0 and (2 * qk) % vdim == 0
    vblk = 2 * qk // vdim
    bias = jnp.concatenate([b_i, b_f]).astype(F32)
    bias_col = jnp.zeros((1, LANES), F32).at[0, :2 * heads].set(bias)
    gates_row = jnp.swapaxes(gates[:, :, :2 * heads], 1, 2)
    return pl.pallas_call(
        functools.partial(_mlstm_kernel, heads=heads, dqk=dqk, dv=dv),
        grid=(bsz, seq // L),
        in_specs=[
            pl.BlockSpec((None, L, qk), lambda b, c: (b, c, 0)),
            pl.BlockSpec((None, L, qk), lambda b, c: (b, c, 1)),
            pl.BlockSpec((None, L, vdim), lambda b, c: (b, c, vblk)),
            pl.BlockSpec((None, L, vdim), lambda b, c: (b, c, vblk + 1)),
            pl.BlockSpec((None, L, LANES), lambda b, c: (b, c, 0)),
            pl.BlockSpec((None, 2 * heads, L), lambda b, c: (b, 0, c)),
            pl.BlockSpec((1, LANES), lambda b, c: (0, 0)),
            pl.BlockSpec((2 * heads, 1), lambda b, c: (0, 0)),
            pl.BlockSpec((1, vdim), lambda b, c: (0, 0)),
        ],
        out_specs=pl.BlockSpec((None, L, vdim), lambda b, c: (b, c, 0)),
        out_shape=jax.ShapeDtypeStruct((bsz, seq, vdim), BF16),
        scratch_shapes=[
            pltpu.VMEM((heads, dqk, dv), F32),
            pltpu.VMEM((heads, 1, dqk), F32),
            pltpu.VMEM((heads, SUBLANES, LANES), F32),
        ],
        compiler_params=_cparams(("parallel", "arbitrary")),
        name="mlstm",
    )(z, z, z, z, gates, gates_row, bias_col, bias.reshape(-1, 1), norm_w.reshape(1, vdim))


def _col_tile(n, cap):
    for step in (MXU_WIDTH, LANES):
        fits = [c for c in range(step, cap + 1, step) if n % c == 0]
        if fits:
            return fits[-1]
    raise ValueError(f"no lane-aligned column tile divides {n}")


def _tail_weight(w, start):
    tail = w[:, :, start:]
    pad = (-tail.shape[2]) % LANES
    return jnp.pad(tail, ((0, 0), (0, 0), (0, pad)))


def kernel(x, norm_mix, norm_mlp, norm_final, mlp_up, mlp_down, hy_in, lru_conv_w, lru_conv_b, lru_wa, lru_ba, lru_wx, lru_bx, lru_lam, rwkv_mu, rwkv_w0, rwkv_w2, rwkv_a0, rwkv_a2, rwkv_g2, rwkv_kk, rwkv_ka, rwkv_rk, rwkv_ln_w, rwkv_ln_b, hy_out, ml_in, ml_bi, ml_bf, ml_norm, ml_out):
    bsz, seq, d = x.shape
    t = bsz * seq
    depth = norm_mix.shape[0]
    xf = x.reshape(t, d)

    lru_w = lru_lam.shape[1]
    rw_w = rwkv_w0.shape[1]
    hy_main = 2 * lru_w + 3 * rw_w
    heads = ml_bi.shape[1]
    vdim = ml_norm.shape[1]
    qk = (ml_in.shape[2] - 2 * vdim - 2 * heads) // 2
    ml_main = 2 * qk + 2 * vdim

    hy_lora, ml_gate = _tail_weight(hy_in, hy_main), _tail_weight(ml_in, ml_main)
    hy_in_main = hy_in[:, :, :hy_main].astype(BF16)
    ml_in_main = ml_in[:, :, :ml_main].astype(BF16)
    tn = lambda n: _col_tile(n, COL_TILE)
    tn_res = _col_tile(d, RES_COL_TILE)
    tk_res = min(K_TILE, lru_w, vdim)

    for layer in range(depth):
        hn = _norm_cast(xf, norm_mix[layer], BF16)
        if layer % 2 == 0:
            e = layer // 2
            z = _matmul([hn], hy_in_main, e, hy_main, tk=d, tn=tn(hy_main), out_dtype=BF16)
            z = z.reshape(bsz, seq, hy_main)
            z_lora = _matmul([hn], hy_lora, e, hy_lora.shape[2], tk=d, tn=tn(hy_lora.shape[2]))
            z_lora = z_lora.reshape(bsz, seq, -1)
            ya = _lru_branch(z, lru_conv_w[e], lru_conv_b[e], lru_wa[e], lru_ba[e], lru_wx[e], lru_bx[e],
                             lru_lam[e], tc=min(LRU_ROWS, seq))
            yb = _rwkv_branch(z, 2 * lru_w, z_lora, rwkv_mu[e], rwkv_w0[e], rwkv_w2[e], rwkv_a0[e], rwkv_a2[e],
                              rwkv_g2[e], rwkv_kk[e], rwkv_ka[e], rwkv_rk[e], rwkv_ln_w[e], rwkv_ln_b[e],
                              tc=min(RWKV_ROWS, seq), pp=min(RWKV_PAIRS, rw_w // LANES))
            lhs = [ya.reshape(t, lru_w), yb.reshape(t, rw_w)]
            assert lru_w == rw_w
            xf = _matmul(lhs, hy_out, e, d, tk=tk_res, tn=tn_res, tm=ROW_TILE // 2, res=xf)
        else:
            o = layer // 2
            z = _matmul([hn], ml_in_main, o, ml_main, tk=d, tn=tn(ml_main), out_dtype=BF16)
            z = z.reshape(bsz, seq, ml_main)
            gates = _matmul([hn], ml_gate, o, ml_gate.shape[2], tk=d, tn=tn(ml_gate.shape[2]))
            gates = gates.reshape(bsz, seq, -1)
            hs = _mlstm_branch(z, gates, ml_bi[o], ml_bf[o], ml_norm[o], heads=heads, qk=qk, vdim=vdim)
            xf = _matmul([hs.reshape(t, vdim)], ml_out, o, d, tk=tk_res, tn=tn_res, res=xf)
        hn = _norm_cast(xf, norm_mlp[layer], BF16)
        u = _matmul([hn], mlp_up, layer, mlp_up.shape[2], tk=d, tn=tn(mlp_up.shape[2]), out_dtype=BF16,
                    relu2=True)
        xf = _matmul([u], mlp_down, layer, d, tk=tk_res, tn=tn_res, res=xf)
    return _norm_cast(xf, norm_final, F32).reshape(bsz, seq, d)
```

```python
import functools

import jax
import jax.numpy as jnp
from jax import lax
from jax.experimental import pallas as pl
from jax.experimental.pallas import tpu as pltpu

F32 = jnp.float32
BF16 = jnp.bfloat16
HIGHEST = lax.Precision.HIGHEST

NORM_EPS = 1e-6
GN_EPS = 64e-5
LRU_C = 8.0
CONV_WIDTH = 4
RWKV_HEAD = 64
LANES = 128
SUBLANES = 8
MXU_WIDTH = 256
VMEM_LIMIT_BYTES = 60 * 1024 * 1024
RWKV_CHUNK = 64
RWKV_ROWS = 128
RWKV_PAIRS = 16
LRU_ROWS = 256
ML_CHUNK_ROWS = 256
NORM_ROWS = 256
ROW_TILE = 2048
COL_TILE = 512
K_TILE_OUT = 2048
DOWN_COL_TILE = 1024
K_TILE_DOWN = 1024


def _cparams(semantics):
    return pltpu.CompilerParams(dimension_semantics=semantics, vmem_limit_bytes=VMEM_LIMIT_BYTES)


def _softplus(x):
    return jnp.maximum(x, 0.0) + jnp.log1p(jnp.exp(-jnp.abs(x)))


def _sigmoid(x):
    return 1.0 / (1.0 + jnp.exp(-x))


_NN = (((1,), (0,)), ((), ()))
_NT = (((1,), (1,)), ((), ()))
_TN = (((0,), (0,)), ((), ()))
_BNN = (((2,), (1,)), ((0,), (0,)))
_BNT = (((2,), (2,)), ((0,), (0,)))
_BTN = (((1,), (1,)), ((0,), (0,)))


def _dot(a, b, dims=_NN, precision=None):
    return lax.dot_general(a, b, dims, precision=precision, preferred_element_type=F32)


def _bdot(a, b, dims=_NN):
    return lax.dot_general(a.astype(BF16), b.astype(BF16), dims, preferred_element_type=F32)


def _norm_cast_kernel(x_ref, g_ref, o_ref):
    x = x_ref[...]
    ms = jnp.mean(x * x, axis=-1, keepdims=True)
    o_ref[...] = (x * lax.rsqrt(ms + NORM_EPS) * g_ref[...]).astype(o_ref.dtype)


def _norm_cast(x, g, out_dtype):
    t, d = x.shape
    tm = min(NORM_ROWS, t)
    return pl.pallas_call(
        _norm_cast_kernel,
        grid=(t // tm,),
        in_specs=[pl.BlockSpec((tm, d), lambda i: (i, 0)), pl.BlockSpec((1, d), lambda i: (0, 0))],
        out_specs=pl.BlockSpec((tm, d), lambda i: (i, 0)),
        out_shape=jax.ShapeDtypeStruct((t, d), out_dtype),
        compiler_params=_cparams(("parallel",)),
        name="rmsnorm",
    )(x, g.reshape(1, d))


def _matmul_kernel(*refs, n_lhs, nk, has_res, relu2, w_t):
    a_refs, w_ref = refs[:n_lhs], refs[n_lhs]
    r_ref = refs[n_lhs + 1] if has_res else None
    o_ref = refs[n_lhs + 1 + has_res]
    acc_ref = refs[-1] if (not has_res and nk > 1) else None
    kk = pl.program_id(2)

    def finish(acc):
        if relu2:
            acc = jnp.maximum(acc, 0.0)
            acc = acc * acc
        return acc.astype(o_ref.dtype)

    def accumulate(a_ref):
        def part():
            return lax.dot_general(a_ref[...], w_ref[...].astype(BF16), _NT if w_t else _NN,
                                   preferred_element_type=F32)

        if has_res:
            @pl.when(kk == 0)
            def _():
                o_ref[...] = r_ref[...] + part()

            @pl.when(kk > 0)
            def _():
                o_ref[...] += part()
        elif nk == 1:
            o_ref[...] = finish(part())
        else:
            @pl.when(kk == 0)
            def _():
                acc_ref[...] = part()

            @pl.when((kk > 0) & (kk < nk - 1))
            def _():
                acc_ref[...] += part()

            @pl.when(kk == nk - 1)
            def _():
                o_ref[...] = finish(acc_ref[...] + part())

    if n_lhs == 1:
        accumulate(a_refs[0])
    else:
        per = nk // n_lhs
        for idx, a_ref in enumerate(a_refs):
            pl.when((kk >= idx * per) & (kk < (idx + 1) * per))(functools.partial(accumulate, a_ref))


def _matmul(lhs, w, layer, n, *, tk, tn, tm=ROW_TILE, out_dtype=F32, res=None, relu2=False, w_t=False):
    t = lhs[0].shape[0]
    k = w.shape[2 if w_t else 1]
    tm = min(tm, t)
    nk = k // tk
    assert t % tm == 0 and k % tk == 0 and n % tn == 0
    per = nk // len(lhs)
    assert all(a.shape[1] == per * tk for a in lhs) and per * len(lhs) == nk
    once = dict(pipeline_mode=pl.Buffered(1)) if per == 1 else {}

    def lhs_spec(idx):
        return pl.BlockSpec((tm, tk), lambda i, j, kk: (i, jnp.clip(kk - idx * per, 0, per - 1)), **once)

    a_specs = [lhs_spec(idx) for idx in range(len(lhs))]
    has_res = res is not None
    assert not (has_res and (relu2 or out_dtype != F32))
    tile = pl.BlockSpec((tm, tn), lambda i, j, kk: (i, j))
    scratch = [pltpu.VMEM((tm, tn), F32)] if (not has_res and nk > 1) else []
    if w_t:
        w_spec = pl.BlockSpec((None, tn, tk), lambda i, j, kk: (layer, j, kk))
    else:
        w_spec = pl.BlockSpec((None, tk, tn), lambda i, j, kk: (layer, kk, j))
    return pl.pallas_call(
        functools.partial(_matmul_kernel, n_lhs=len(lhs), nk=nk, has_res=has_res, relu2=relu2, w_t=w_t),
        grid=(t // tm, n // tn, nk),
        in_specs=a_specs + [w_spec] + ([tile] if has_res else []),
        out_specs=tile,
        out_shape=jax.ShapeDtypeStruct((t, n), out_dtype),
        scratch_shapes=scratch,
        compiler_params=_cparams(("parallel", "parallel", "arbitrary")),
        name="matmul",
    )(*lhs, w, *([res] if has_res else []))


def _lru_kernel(u_ref, gate_ref, cw_ref, cb_ref, wa_ref, ba_ref, wx_ref, bx_ref, lam_ref,
                o_ref, xbuf, hcar, *, tc):
    @pl.when(pl.program_id(2) == 0)
    def _():
        xbuf[0:SUBLANES, :] = jnp.zeros((SUBLANES, xbuf.shape[1]), F32)
        hcar[...] = jnp.zeros(hcar.shape, F32)

    u = u_ref[...].astype(F32)
    xbuf[SUBLANES:, :] = u
    cw = cw_ref[...]
    conv = cb_ref[...] + cw[CONV_WIDTH - 1:CONV_WIDTH] * u
    for j in range(CONV_WIDTH - 1):
        back = CONV_WIDTH - 1 - j
        conv = conv + cw[j:j + 1] * xbuf[SUBLANES - back:SUBLANES - back + tc, :]
    xbuf[0:SUBLANES, :] = u[tc - SUBLANES:, :]

    cbf = conv.astype(BF16)
    r = _sigmoid(jnp.dot(cbf, wa_ref[...], preferred_element_type=F32) + ba_ref[...])
    i = _sigmoid(jnp.dot(cbf, wx_ref[...], preferred_element_type=F32) + bx_ref[...])
    log_a = (-LRU_C) * r * _softplus(-lam_ref[...])
    a = jnp.exp(log_a)
    b = jnp.sqrt(-jnp.tanh(log_a) * (a * a + 1.0)) * (i * conv)

    row = lax.broadcasted_iota(jnp.int32, a.shape, 0)
    d = 1
    while d < tc:
        a_sh = pltpu.roll(a, d, 0)
        b_sh = pltpu.roll(b, d, 0)
        keep = row >= d
        b = jnp.where(keep, a * b_sh + b, b)
        a = jnp.where(keep, a * a_sh, a)
        d *= 2
    h = a * hcar[...] + b
    hcar[...] = h[tc - 1:tc, :]
    o_ref[...] = (h * jax.nn.gelu(gate_ref[...].astype(F32))).astype(o_ref.dtype)


def _lru_branch(z, conv_w, conv_b, w_a, b_a, w_x, b_x, lam, *, tc):
    bsz, seq, _ = z.shape
    heads, blk, _ = w_a.shape
    width = heads * blk
    assert blk % LANES == 0 and seq % tc == 0
    row = lambda v: v.reshape(1, width)
    vec_spec = pl.BlockSpec((1, blk), lambda b, h, t: (0, h))
    mat_spec = pl.BlockSpec((None, blk, blk), lambda b, h, t: (h, 0, 0))
    return pl.pallas_call(
        functools.partial(_lru_kernel, tc=tc),
        grid=(bsz, heads, seq // tc),
        in_specs=[
            pl.BlockSpec((None, tc, blk), lambda b, h, t: (b, t, h)),
            pl.BlockSpec((None, tc, blk), lambda b, h, t: (b, t, heads + h)),
            pl.BlockSpec((CONV_WIDTH, blk), lambda b, h, t: (0, h)),
            vec_spec, mat_spec, vec_spec, mat_spec, vec_spec, vec_spec,
        ],
        out_specs=pl.BlockSpec((None, tc, blk), lambda b, h, t: (b, t, h)),
        out_shape=jax.ShapeDtypeStruct((bsz, seq, width), BF16),
        scratch_shapes=[pltpu.VMEM((tc + SUBLANES, blk), F32), pltpu.VMEM((1, blk), F32)],
        compiler_params=_cparams(("parallel", "parallel", "arbitrary")),
        name="rglru",
    )(z, z, conv_w, row(conv_b), w_a.astype(BF16), row(b_a), w_x.astype(BF16), row(b_x), row(lam))


def _rwkv_kernel(r_ref, k_ref, v_ref, lora_ref, mu_r, mu_k, mu_v, mu_l, w0_ref, w2_ref, a0_ref, a2_ref,
                 g2_ref, kk_ref, ka_ref, rk_ref, lnw_ref, lnb_ref, o_ref,
                 rbuf, kbuf, vbuf, lbuf, state, *, tc, pp):
    L = RWKV_CHUNK
    P2 = 2 * L
    nc = tc // L
    assert P2 == LANES and RWKV_HEAD == L

    @pl.when(pl.program_id(2) == 0)
    def _():
        for buf in (rbuf, kbuf, vbuf, lbuf):
            buf[0:SUBLANES, :] = jnp.zeros((SUBLANES, buf.shape[1]), F32)
        state[...] = jnp.zeros(state.shape, F32)

    def token_shift(ref, buf, mu_ref):
        cur = ref[...].astype(F32)
        buf[SUBLANES:, :] = cur
        prev = buf[SUBLANES - 1:SUBLANES - 1 + tc, :]
        buf[0:SUBLANES, :] = cur[tc - SUBLANES:, :]
        return cur + (prev - cur) * mu_ref[...]

    r = token_shift(r_ref, rbuf, mu_r)
    k = token_shift(k_ref, kbuf, mu_k)
    v = token_shift(v_ref, vbuf, mu_v)
    lora = token_shift(lora_ref, lbuf, mu_l)

    dw = _dot(jnp.tanh(lora).astype(BF16), w2_ref[...])
    w = -_softplus(-(w0_ref[...] + dw)) - 0.5
    logw = -jnp.exp(w)
    a = _sigmoid(a0_ref[...] + _dot(lora.astype(BF16), a2_ref[...]))
    g = _dot(_sigmoid(lora).astype(BF16), g2_ref[...])

    ri = lax.broadcasted_iota(jnp.int32, (P2, P2), 0)
    ci = lax.broadcasted_iota(jnp.int32, (P2, P2), 1)
    blk = jnp.where(ri >= L, 1, 0) == jnp.where(ci >= L, 1, 0)
    head_ones = jnp.where(blk, 1.0, 0.0).astype(BF16)

    def to_rows(x):
        return jnp.concatenate([x[:, j * LANES:(j + 1) * LANES] for j in range(pp)], axis=0)

    def from_rows(x):
        return jnp.concatenate([x[j * tc:(j + 1) * tc, :] for j in range(pp)], axis=1)

    def head_sum(x):
        rows = to_rows(x)
        hi = rows.astype(BF16)
        lo = (rows - hi.astype(F32)).astype(BF16)
        return from_rows(_dot(hi, head_ones) + _dot(lo, head_ones))

    kk = k * kk_ref[...]
    kk = kk / jnp.maximum(jnp.sqrt(head_sum(kk * kk)), 1e-12)
    kmod = k * (1.0 + (a - 1.0) * ka_ref[...])
    kka = kk * a
    bonus = head_sum(r * kmod * rk_ref[...]) * v

    rt = lax.broadcasted_iota(jnp.int32, (tc, tc), 0)
    ct = lax.broadcasted_iota(jnp.int32, (tc, tc), 1)
    shift = L.bit_length() - 1
    same_chunk = lax.shift_right_logical(rt, shift) == lax.shift_right_logical(ct, shift)
    cum = _dot(jnp.where(same_chunk & (ct <= rt), 1.0, 0.0).astype(F32), logw, precision=HIGHEST)
    p_inv = jnp.exp(-cum)
    r_d = r * jnp.exp(cum)
    kk_d = kk * jnp.exp(cum - logw)
    k_d = kmod * p_inv
    a_d = kka * p_inv
    cum_last = jnp.concatenate(
        [jnp.broadcast_to(cum[(c + 1) * L - 1:(c + 1) * L, :], (L, cum.shape[1])) for c in range(nc)], axis=0)
    p_tail = jnp.exp(cum_last - cum)
    k_t = kmod * p_tail
    a_t = kka * p_tail
    p_last = jnp.exp(cum_last)

    strict = (blk & (ci < ri))[None]
    incl = (blk & (ci <= ri))[None]
    eye = jnp.where(ri == ci, 1.0, 0.0).astype(F32)[None]
    lane_row = lax.broadcasted_iota(jnp.int32, (1, LANES), 1)
    m0 = jnp.where(lane_row < RWKV_HEAD, 1.0, 0.0).astype(F32)
    m1 = 1.0 - m0

    def batch(x):
        out = []
        for c in range(nc):
            for j in range(pp):
                xs = x[c * L:(c + 1) * L, j * LANES:(j + 1) * LANES]
                out.append(jnp.concatenate([xs * m0, xs * m1], axis=0))
        return jnp.stack(out)

    r_b, kk_b, k_b, a_b, v_b, kt_b, at_b = (batch(x) for x in (r_d, kk_d, k_d, a_d, v, k_t, a_t))

    gram = _bdot(jnp.concatenate([kk_b, r_b], axis=1), jnp.concatenate([k_b, a_b], axis=1), _BNT)
    a_kk_k = jnp.where(strict, gram[:, 0:P2, 0:P2], 0.0)
    n_mat = jnp.where(strict, gram[:, 0:P2, P2:], 0.0)
    a_r_k = jnp.where(incl, gram[:, P2:, 0:P2], 0.0)
    a_r_a = jnp.where(incl, gram[:, P2:, P2:], 0.0)

    inv = eye - n_mat
    pw = _bdot(n_mat, n_mat, _BNN)
    steps = 2
    while 2 * steps < L:
        both = _bdot(jnp.concatenate([inv, pw], axis=1), pw, _BNN)
        inv = inv + both[:, 0:P2]
        pw = both[:, P2:]
        steps *= 2
    inv = inv + _bdot(inv, pw, _BNN)
    akk_v = _bdot(a_kk_k, v_b, _BNN)
    ark_v = _bdot(a_r_k, v_b, _BNN)

    s = state[...]
    y_chunks = []
    for c in range(nc):
        sl = slice(c * pp, (c + 1) * pp)
        sa = _bdot(inv[sl], _bdot(kk_b[sl], s, _BNT) + akk_v[sl], _BNN)
        y_s = _bdot(r_b[sl], s, _BNT) + ark_v[sl] - _bdot(a_r_a[sl], sa, _BNN)
        y_c = y_s[:, 0:L, :] + y_s[:, L:, :]
        y_chunks.append(jnp.concatenate([y_c[j] for j in range(pp)], axis=1))
        decay = jnp.stack([p_last[c * L:c * L + 1, j * LANES:(j + 1) * LANES] for j in range(pp)])
        s = s * decay + _bdot(jnp.concatenate([v_b[sl], sa], axis=1),
                              jnp.concatenate([kt_b[sl], -at_b[sl]], axis=1), _BTN)
    state[...] = s

    y = jnp.concatenate(y_chunks, axis=0)
    inv_n = 1.0 / RWKV_HEAD
    mean = head_sum(y) * inv_n
    yc = y - mean
    var = head_sum(yc * yc) * inv_n
    yn = yc * lax.rsqrt(var + GN_EPS) * lnw_ref[...] + lnb_ref[...]
    o_ref[...] = ((yn + bonus) * g).astype(o_ref.dtype)


def _rwkv_branch(z, col0, z_lora, mu, w0, w2, a0, a2, g2, k_k, k_a, r_k, ln_w, ln_b, *, tc, pp):
    bsz, seq, _ = z.shape
    width = w0.shape[0]
    lora_w = z_lora.shape[2]
    bw = pp * LANES
    d_lora, a_lora, g_lora = w2.shape[0], a2.shape[0], g2.shape[0]
    assert width % bw == 0 and col0 % bw == 0 and lora_w % LANES == 0
    assert lora_w >= d_lora + a_lora + g_lora
    assert seq % tc == 0 and tc % RWKV_CHUNK == 0 and tc % SUBLANES == 0
    groups = width // bw
    cb = col0 // bw

    def pad_rows(wm, start):
        return jnp.zeros((lora_w, width), F32).at[start:start + wm.shape[0]].set(wm).astype(BF16)

    w2p = pad_rows(w2, 0)
    a2p = pad_rows(a2, d_lora)
    g2p = pad_rows(g2, d_lora + a_lora)
    mu_pad = jnp.zeros((1, 3 * width + lora_w), F32).at[0, :mu.shape[0]].set(mu)
    mu_r, mu_k, mu_v = (mu_pad[:, i * width:(i + 1) * width] for i in range(3))
    mu_l = mu_pad[:, 3 * width:]
    row = lambda x: x.reshape(1, width)

    act = lambda off: pl.BlockSpec((None, tc, bw), lambda b, p, t: (b, t, cb + off * groups + p))
    vec = pl.BlockSpec((1, bw), lambda b, p, t: (0, p))
    lmat = pl.BlockSpec((lora_w, bw), lambda b, p, t: (0, p))
    return pl.pallas_call(
        functools.partial(_rwkv_kernel, tc=tc, pp=pp),
        grid=(bsz, groups, seq // tc),
        in_specs=[
            act(0), act(1), act(2),
            pl.BlockSpec((None, tc, lora_w), lambda b, p, t: (b, t, 0)),
            vec, vec, vec,
            pl.BlockSpec((1, lora_w), lambda b, p, t: (0, 0)),
            vec, lmat, vec, lmat, lmat, vec, vec, vec, vec, vec,
        ],
        out_specs=pl.BlockSpec((None, tc, bw), lambda b, p, t: (b, t, p)),
        out_shape=jax.ShapeDtypeStruct((bsz, seq, width), BF16),
        scratch_shapes=[
            pltpu.VMEM((tc + SUBLANES, bw), F32), pltpu.VMEM((tc + SUBLANES, bw), F32),
            pltpu.VMEM((tc + SUBLANES, bw), F32), pltpu.VMEM((tc + SUBLANES, lora_w), F32),
            pltpu.VMEM((pp, LANES, LANES), F32),
        ],
        compiler_params=_cparams(("parallel", "parallel", "arbitrary")),
        name="rwkv7",
    )(z, z, z, z_lora, mu_r, mu_k, mu_v, mu_l, row(w0), w2p, row(a0), a2p, g2p, row(k_k), row(k_a),
      row(r_k), row(ln_w), row(ln_b))


def _mlstm_kernel(q_ref, k_ref, v_ref, zo_ref, gcol_ref, grow_ref, bcol_ref, brow_ref, nw_ref, o_ref,
                  c_st, n_st, m_st, *, heads, dqk, dv):
    L = q_ref.shape[0]

    @pl.when(pl.program_id(1) == 0)
    def _():
        c_st[...] = jnp.zeros(c_st.shape, F32)
        n_st[...] = jnp.zeros(n_st.shape, F32)
        m_st[...] = jnp.zeros(m_st.shape, F32)

    ri = lax.broadcasted_iota(jnp.int32, (L, L), 0)
    ci = lax.broadcasted_iota(jnp.int32, (L, L), 1)
    causal = ci <= ri
    tril_ones = jnp.where(causal, 1.0, 0.0).astype(F32)
    triu_ones = jnp.where(ci >= ri, 1.0, 0.0).astype(F32)

    gc = gcol_ref[...] + bcol_ref[...]
    gr = grow_ref[...] + brow_ref[...]
    b_c = _dot(tril_ones, -_softplus(-gc), precision=HIGHEST)
    b_r = _dot(-_softplus(-gr), triu_ones, precision=HIGHEST)

    scale = dqk ** -0.5
    for h in range(heads):
        q = (q_ref[:, h * dqk:(h + 1) * dqk].astype(F32) * scale).astype(BF16)
        k = k_ref[:, h * dqk:(h + 1) * dqk].astype(F32)
        v = v_ref[:, h * dv:(h + 1) * dv].astype(BF16)
        bc, lic = b_c[:, heads + h:heads + h + 1], gc[:, h:h + 1]
        br, lir = b_r[heads + h:heads + h + 1, :], gr[h:h + 1, :]
        m_prev = m_st[h][0:1, 0:1]
        c_mat = c_st[h]
        n_vec = n_st[h]

        log_d = jnp.where(causal, bc - br + lir, -jnp.inf)
        log_inter = bc + m_prev
        m_t = jnp.maximum(jnp.max(log_d, axis=-1, keepdims=True), log_inter)
        p = _dot(q, k.astype(BF16), _NT) * jnp.exp(log_d - m_t)
        inter = jnp.exp(log_inter - m_t)
        num = _dot(p.astype(BF16), v) + inter * _dot(q, c_mat.astype(BF16))
        den = (jnp.sum(p, axis=-1, keepdims=True)
               + inter * jnp.sum(q.astype(F32) * n_vec, axis=-1, keepdims=True))
        hs = num / jnp.maximum(jnp.abs(den), jnp.exp(-m_t))

        b_last = bc[L - 1:L, :]
        log_g = b_last - bc + lic
        m_new = jnp.maximum(b_last + m_prev, jnp.max(log_g, axis=0, keepdims=True))
        carry = jnp.exp(b_last + m_prev - m_new)
        wk = k * jnp.exp(log_g - m_new)
        c_st[h] = carry * c_mat + _dot(wk.astype(BF16), v, _TN)
        n_st[h] = carry * n_vec + jnp.sum(wk, axis=0, keepdims=True)
        m_st[h] = jnp.broadcast_to(m_new, m_st.shape[1:])

        hs = hs * lax.rsqrt(jnp.mean(hs * hs, axis=-1, keepdims=True) + NORM_EPS)
        sl = slice(h * dv, (h + 1) * dv)
        o_ref[:, sl] = (hs * nw_ref[:, sl] * _sigmoid(zo_ref[:, sl].astype(F32))).astype(o_ref.dtype)


def _mlstm_branch(z, gates, b_i, b_f, norm_w, *, heads, qk, vdim):
    bsz, seq, _ = z.shape
    L = ML_CHUNK_ROWS
    dqk, dv = qk // heads, vdim // heads
    assert seq % L == 0 and vdim % qk == 0 and qk % LANES == 0 and (2 * qk) % vdim == 0
    vblk = 2 * qk // vdim
    bias = jnp.concatenate([b_i, b_f]).astype(F32)
    bias_col = jnp.zeros((1, LANES), F32).at[0, :2 * heads].set(bias)
    gates_row = jnp.swapaxes(gates[:, :, :2 * heads], 1, 2)
    return pl.pallas_call(
        functools.partial(_mlstm_kernel, heads=heads, dqk=dqk, dv=dv),
        grid=(bsz, seq // L),
        in_specs=[
            pl.BlockSpec((None, L, qk), lambda b, c: (b, c, 0)),
            pl.BlockSpec((None, L, qk), lambda b, c: (b, c, 1)),
            pl.BlockSpec((None, L, vdim), lambda b, c: (b, c, vblk)),
            pl.BlockSpec((None, L, vdim), lambda b, c: (b, c, vblk + 1)),
            pl.BlockSpec((None, L, LANES), lambda b, c: (b, c, 0)),
            pl.BlockSpec((None, 2 * heads, L), lambda b, c: (b, 0, c)),
            pl.BlockSpec((1, LANES), lambda b, c: (0, 0)),
            pl.BlockSpec((2 * heads, 1), lambda b, c: (0, 0)),
            pl.BlockSpec((1, vdim), lambda b, c: (0, 0)),
        ],
        out_specs=pl.BlockSpec((None, L, vdim), lambda b, c: (b, c, 0)),
        out_shape=jax.ShapeDtypeStruct((bsz, seq, vdim), BF16),
        scratch_shapes=[
            pltpu.VMEM((heads, dqk, dv), F32),
            pltpu.VMEM((heads, 1, dqk), F32),
            pltpu.VMEM((heads, SUBLANES, LANES), F32),
        ],
        compiler_params=_cparams(("parallel", "arbitrary")),
        name="mlstm",
    )(z, z, z, z, gates, gates_row, bias_col, bias.reshape(-1, 1), norm_w.reshape(1, vdim))


def _col_tile(n, cap):
    for step in (MXU_WIDTH, LANES):
        fits = [c for c in range(step, cap + 1, step) if n % c == 0]
        if fits:
            return fits[-1]
    raise ValueError(f"no lane-aligned column tile divides {n}")


def _tail_rows(wt, start):
    tail = wt[:, start:, :]
    pad = (-tail.shape[1]) % LANES
    return jnp.pad(tail, ((0, 0), (0, pad), (0, 0)))


def kernel(x, norm_mix, norm_mlp, norm_final, mlp_up, mlp_down, hy_in, lru_conv_w, lru_conv_b, lru_wa, lru_ba, lru_wx, lru_bx, lru_lam, rwkv_mu, rwkv_w0, rwkv_w2, rwkv_a0, rwkv_a2, rwkv_g2, rwkv_kk, rwkv_ka, rwkv_rk, rwkv_ln_w, rwkv_ln_b, hy_out, ml_in, ml_bi, ml_bf, ml_norm, ml_out):
    bsz, seq, d = x.shape
    t = bsz * seq
    depth = norm_mix.shape[0]
    xf = x.reshape(t, d)

    lru_w = lru_lam.shape[1]
    rw_w = rwkv_w0.shape[1]
    hy_main = 2 * lru_w + 3 * rw_w
    heads = ml_bi.shape[1]
    vdim = ml_norm.shape[1]
    qk = (ml_in.shape[2] - 2 * vdim - 2 * heads) // 2
    ml_main = 2 * qk + 2 * vdim

    hy_in_t, ml_in_t = jnp.swapaxes(hy_in, 1, 2), jnp.swapaxes(ml_in, 1, 2)
    hy_lora_t, ml_gate_t = _tail_rows(hy_in_t, hy_main), _tail_rows(ml_in_t, ml_main)
    tn = lambda n: _col_tile(n, COL_TILE)
    tk_out = min(K_TILE_OUT, lru_w, vdim)

    for layer in range(depth):
        hn = _norm_cast(xf, norm_mix[layer], BF16)
        if layer % 2 == 0:
            e = layer // 2
            z = _matmul([hn], hy_in_t, e, hy_main, tk=d, tn=tn(hy_main), out_dtype=BF16, w_t=True)
            z = z.reshape(bsz, seq, hy_main)
            n_lora = hy_lora_t.shape[1]
            z_lora = _matmul([hn], hy_lora_t, e, n_lora, tk=d, tn=tn(n_lora), w_t=True)
            z_lora = z_lora.reshape(bsz, seq, n_lora)
            ya = _lru_branch(z, lru_conv_w[e], lru_conv_b[e], lru_wa[e], lru_ba[e], lru_wx[e], lru_bx[e],
                             lru_lam[e], tc=min(LRU_ROWS, seq))
            yb = _rwkv_branch(z, 2 * lru_w, z_lora, rwkv_mu[e], rwkv_w0[e], rwkv_w2[e], rwkv_a0[e], rwkv_a2[e],
                              rwkv_g2[e], rwkv_kk[e], rwkv_ka[e], rwkv_rk[e], rwkv_ln_w[e], rwkv_ln_b[e],
                              tc=min(RWKV_ROWS, seq), pp=min(RWKV_PAIRS, rw_w // LANES))
            lhs = [ya.reshape(t, lru_w), yb.reshape(t, rw_w)]
            assert lru_w == rw_w
            xf = _matmul(lhs, hy_out, e, d, tk=tk_out, tn=tn(d), res=xf)
        else:
            o = layer // 2
            z = _matmul([hn], ml_in_t, o, ml_main, tk=d, tn=tn(ml_main), out_dtype=BF16, w_t=True)
            z = z.reshape(bsz, seq, ml_main)
            n_gate = ml_gate_t.shape[1]
            gates = _matmul([hn], ml_gate_t, o, n_gate, tk=d, tn=tn(n_gate), w_t=True)
            gates = gates.reshape(bsz, seq, n_gate)
            hs = _mlstm_branch(z, gates, ml_bi[o], ml_bf[o], ml_norm[o], heads=heads, qk=qk, vdim=vdim)
            xf = _matmul([hs.reshape(t, vdim)], ml_out, o, d, tk=tk_out, tn=tn(d), res=xf)
        hn = _norm_cast(xf, norm_mlp[layer], BF16)
        u = _matmul([hn], mlp_up, layer, mlp_up.shape[2], tk=d, tn=tn(mlp_up.shape[2]), out_dtype=BF16,
                    relu2=True)
        xf = _matmul([u], mlp_down, layer, d, tk=min(K_TILE_DOWN, mlp_down.shape[1]),
                     tn=_col_tile(d, DOWN_COL_TILE), res=xf)
    return _norm_cast(xf, norm_final, F32).reshape(bsz, seq, d)
```

```python
import functools

import jax
import jax.numpy as jnp
from jax import lax
from jax.experimental import pallas as pl
from jax.experimental.pallas import tpu as pltpu

F32 = jnp.float32
BF16 = jnp.bfloat16
HIGHEST = lax.Precision.HIGHEST

NORM_EPS = 1e-6
GN_EPS = 64e-5
LRU_C = 8.0
CONV_WIDTH = 4
RWKV_HEAD = 64
LANES = 128
SUBLANES = 8
MXU_WIDTH = 256
VMEM_LIMIT_BYTES = 60 * 1024 * 1024
RWKV_CHUNK = 64
RWKV_ROWS = 128
RWKV_PAIRS = 16
LRU_ROWS = 256
ML_CHUNK_ROWS = 256
NORM_ROWS = 512
ROW_TILE = 2048
COL_TILE = 512
K_TILE_OUT = 2048
DOWN_COL_TILE = 1024
K_TILE_DOWN = 1024


def _cparams(semantics):
    return pltpu.CompilerParams(dimension_semantics=semantics, vmem_limit_bytes=VMEM_LIMIT_BYTES)


def _softplus(x):
    return jnp.maximum(x, 0.0) + jnp.log1p(jnp.exp(-jnp.abs(x)))


def _sigmoid(x):
    return 1.0 / (1.0 + jnp.exp(-x))


_NN = (((1,), (0,)), ((), ()))
_NT = (((1,), (1,)), ((), ()))
_TN = (((0,), (0,)), ((), ()))
_BNN = (((2,), (1,)), ((0,), (0,)))
_BNT = (((2,), (2,)), ((0,), (0,)))
_BTN = (((1,), (1,)), ((0,), (0,)))


def _dot(a, b, dims=_NN, precision=None):
    return lax.dot_general(a, b, dims, precision=precision, preferred_element_type=F32)


def _bdot(a, b, dims=_NN):
    return lax.dot_general(a.astype(BF16), b.astype(BF16), dims, preferred_element_type=F32)


def _norm_cast_kernel(x_ref, g_ref, o_ref):
    x = x_ref[...]
    ms = jnp.mean(x * x, axis=-1, keepdims=True)
    o_ref[...] = (x * lax.rsqrt(ms + NORM_EPS) * g_ref[...]).astype(o_ref.dtype)


def _norm_cast(x, g, out_dtype):
    t, d = x.shape
    tm = min(NORM_ROWS, t)
    return pl.pallas_call(
        _norm_cast_kernel,
        grid=(t // tm,),
        in_specs=[pl.BlockSpec((tm, d), lambda i: (i, 0)), pl.BlockSpec((1, d), lambda i: (0, 0))],
        out_specs=pl.BlockSpec((tm, d), lambda i: (i, 0)),
        out_shape=jax.ShapeDtypeStruct((t, d), out_dtype),
        compiler_params=_cparams(("parallel",)),
        name="rmsnorm",
    )(x, g.reshape(1, d))


def _matmul_kernel(*refs, n_lhs, nk, has_res, relu2, w_t):
    a_refs, w_ref = refs[:n_lhs], refs[n_lhs]
    r_ref = refs[n_lhs + 1] if has_res else None
    o_ref = refs[n_lhs + 1 + has_res]
    acc_ref = refs[-1] if (not has_res and nk > 1) else None
    kk = pl.program_id(2)

    def finish(acc):
        if relu2:
            acc = jnp.maximum(acc, 0.0)
            acc = acc * acc
        return acc.astype(o_ref.dtype)

    def accumulate(a_ref):
        def part():
            return lax.dot_general(a_ref[...], w_ref[...].astype(BF16), _NT if w_t else _NN,
                                   preferred_element_type=F32)

        if has_res:
            @pl.when(kk == 0)
            def _():
                o_ref[...] = r_ref[...] + part()

            @pl.when(kk > 0)
            def _():
                o_ref[...] += part()
        elif nk == 1:
            o_ref[...] = finish(part())
        else:
            @pl.when(kk == 0)
            def _():
                acc_ref[...] = part()

            @pl.when((kk > 0) & (kk < nk - 1))
            def _():
                acc_ref[...] += part()

            @pl.when(kk == nk - 1)
            def _():
                o_ref[...] = finish(acc_ref[...] + part())

    if n_lhs == 1:
        accumulate(a_refs[0])
    else:
        per = nk // n_lhs
        for idx, a_ref in enumerate(a_refs):
            pl.when((kk >= idx * per) & (kk < (idx + 1) * per))(functools.partial(accumulate, a_ref))


def _matmul(lhs, w, layer, n, *, tk, tn, tm=ROW_TILE, out_dtype=F32, res=None, relu2=False, w_t=False):
    t = lhs[0].shape[0]
    k = w.shape[2 if w_t else 1]
    tm = min(tm, t)
    nk = k // tk
    assert t % tm == 0 and k % tk == 0 and n % tn == 0
    per = nk // len(lhs)
    assert all(a.shape[1] == per * tk for a in lhs) and per * len(lhs) == nk
    once = dict(pipeline_mode=pl.Buffered(1)) if per == 1 else {}

    def lhs_spec(idx):
        return pl.BlockSpec((tm, tk), lambda i, j, kk: (i, jnp.clip(kk - idx * per, 0, per - 1)), **once)

    a_specs = [lhs_spec(idx) for idx in range(len(lhs))]
    has_res = res is not None
    assert not (has_res and (relu2 or out_dtype != F32))
    tile = pl.BlockSpec((tm, tn), lambda i, j, kk: (i, j))
    scratch = [pltpu.VMEM((tm, tn), F32)] if (not has_res and nk > 1) else []
    if w_t:
        w_spec = pl.BlockSpec((None, tn, tk), lambda i, j, kk: (layer, j, kk))
    else:
        w_spec = pl.BlockSpec((None, tk, tn), lambda i, j, kk: (layer, kk, j))
    return pl.pallas_call(
        functools.partial(_matmul_kernel, n_lhs=len(lhs), nk=nk, has_res=has_res, relu2=relu2, w_t=w_t),
        grid=(t // tm, n // tn, nk),
        in_specs=a_specs + [w_spec] + ([tile] if has_res else []),
        out_specs=tile,
        out_shape=jax.ShapeDtypeStruct((t, n), out_dtype),
        scratch_shapes=scratch,
        compiler_params=_cparams(("parallel", "parallel", "arbitrary")),
        name="matmul",
    )(*lhs, w, *([res] if has_res else []))


def _lru_kernel(u_ref, gate_ref, cw_ref, cb_ref, wa_ref, ba_ref, wx_ref, bx_ref, lam_ref,
                o_ref, xbuf, hcar, *, tc):
    @pl.when(pl.program_id(2) == 0)
    def _():
        xbuf[0:SUBLANES, :] = jnp.zeros((SUBLANES, xbuf.shape[1]), F32)
        hcar[...] = jnp.zeros(hcar.shape, F32)

    u = u_ref[...].astype(F32)
    xbuf[SUBLANES:, :] = u
    cw = cw_ref[...]
    conv = cb_ref[...] + cw[CONV_WIDTH - 1:CONV_WIDTH] * u
    for j in range(CONV_WIDTH - 1):
        back = CONV_WIDTH - 1 - j
        conv = conv + cw[j:j + 1] * xbuf[SUBLANES - back:SUBLANES - back + tc, :]
    xbuf[0:SUBLANES, :] = u[tc - SUBLANES:, :]

    cbf = conv.astype(BF16)
    r = _sigmoid(jnp.dot(cbf, wa_ref[...], preferred_element_type=F32) + ba_ref[...])
    i = _sigmoid(jnp.dot(cbf, wx_ref[...], preferred_element_type=F32) + bx_ref[...])
    log_a = (-LRU_C) * r * _softplus(-lam_ref[...])
    a = jnp.exp(log_a)
    b = jnp.sqrt(-jnp.tanh(log_a) * (a * a + 1.0)) * (i * conv)

    n_tiles = tc // SUBLANES
    a = a.reshape(n_tiles, SUBLANES, a.shape[1])
    b = b.reshape(n_tiles, SUBLANES, b.shape[1])
    sub = lax.broadcasted_iota(jnp.int32, a.shape, 1)
    d = 1
    while d < SUBLANES:
        a_sh = pltpu.roll(a, d, 1)
        b_sh = pltpu.roll(b, d, 1)
        keep = sub >= d
        b = jnp.where(keep, a * b_sh + b, b)
        a = jnp.where(keep, a * a_sh, a)
        d *= 2
    carry = hcar[...]
    tiles = []
    for i in range(n_tiles):
        tiles.append(a[i] * carry + b[i])
        carry = tiles[-1][SUBLANES - 1:SUBLANES, :]
    h = jnp.concatenate(tiles, axis=0)
    hcar[...] = carry
    o_ref[...] = (h * jax.nn.gelu(gate_ref[...].astype(F32))).astype(o_ref.dtype)


def _lru_branch(z, conv_w, conv_b, w_a, b_a, w_x, b_x, lam, *, tc):
    bsz, seq, _ = z.shape
    heads, blk, _ = w_a.shape
    width = heads * blk
    assert blk % LANES == 0 and seq % tc == 0
    row = lambda v: v.reshape(1, width)
    vec_spec = pl.BlockSpec((1, blk), lambda b, h, t: (0, h))
    mat_spec = pl.BlockSpec((None, blk, blk), lambda b, h, t: (h, 0, 0))
    return pl.pallas_call(
        functools.partial(_lru_kernel, tc=tc),
        grid=(bsz, heads, seq // tc),
        in_specs=[
            pl.BlockSpec((None, tc, blk), lambda b, h, t: (b, t, h)),
            pl.BlockSpec((None, tc, blk), lambda b, h, t: (b, t, heads + h)),
            pl.BlockSpec((CONV_WIDTH, blk), lambda b, h, t: (0, h)),
            vec_spec, mat_spec, vec_spec, mat_spec, vec_spec, vec_spec,
        ],
        out_specs=pl.BlockSpec((None, tc, blk), lambda b, h, t: (b, t, h)),
        out_shape=jax.ShapeDtypeStruct((bsz, seq, width), BF16),
        scratch_shapes=[pltpu.VMEM((tc + SUBLANES, blk), F32), pltpu.VMEM((1, blk), F32)],
        compiler_params=_cparams(("parallel", "parallel", "arbitrary")),
        name="rglru",
    )(z, z, conv_w, row(conv_b), w_a.astype(BF16), row(b_a), w_x.astype(BF16), row(b_x), row(lam))


def _rwkv_kernel(r_ref, k_ref, v_ref, lora_ref, mu_r, mu_k, mu_v, mu_l, w0_ref, w2_ref, a0_ref, a2_ref,
                 g2_ref, kk_ref, ka_ref, rk_ref, lnw_ref, lnb_ref, o_ref,
                 rbuf, kbuf, vbuf, lbuf, state, *, tc, pp):
    L = RWKV_CHUNK
    P2 = 2 * L
    nc = tc // L
    assert P2 == LANES and RWKV_HEAD == L

    @pl.when(pl.program_id(2) == 0)
    def _():
        for buf in (rbuf, kbuf, vbuf, lbuf):
            buf[0:SUBLANES, :] = jnp.zeros((SUBLANES, buf.shape[1]), F32)
        state[...] = jnp.zeros(state.shape, F32)

    def token_shift(ref, buf, mu_ref):
        cur = ref[...].astype(F32)
        buf[SUBLANES:, :] = cur
        prev = buf[SUBLANES - 1:SUBLANES - 1 + tc, :]
        buf[0:SUBLANES, :] = cur[tc - SUBLANES:, :]
        return cur + (prev - cur) * mu_ref[...]

    r = token_shift(r_ref, rbuf, mu_r)
    k = token_shift(k_ref, kbuf, mu_k)
    v = token_shift(v_ref, vbuf, mu_v)
    lora = token_shift(lora_ref, lbuf, mu_l)

    dw = _dot(jnp.tanh(lora).astype(BF16), w2_ref[...])
    w = -_softplus(-(w0_ref[...] + dw)) - 0.5
    logw = -jnp.exp(w)
    a = _sigmoid(a0_ref[...] + _dot(lora.astype(BF16), a2_ref[...]))
    g = _dot(_sigmoid(lora).astype(BF16), g2_ref[...])

    ri = lax.broadcasted_iota(jnp.int32, (P2, P2), 0)
    ci = lax.broadcasted_iota(jnp.int32, (P2, P2), 1)
    blk = jnp.where(ri >= L, 1, 0) == jnp.where(ci >= L, 1, 0)
    head_ones = jnp.where(blk, 1.0, 0.0).astype(BF16)

    def to_rows(x):
        return jnp.concatenate([x[:, j * LANES:(j + 1) * LANES] for j in range(pp)], axis=0)

    def from_rows(x):
        return jnp.concatenate([x[j * tc:(j + 1) * tc, :] for j in range(pp)], axis=1)

    def head_sum(x):
        rows = to_rows(x)
        hi = rows.astype(BF16)
        lo = (rows - hi.astype(F32)).astype(BF16)
        return from_rows(_dot(hi, head_ones) + _dot(lo, head_ones))

    kk = k * kk_ref[...]
    kk = kk / jnp.maximum(jnp.sqrt(head_sum(kk * kk)), 1e-12)
    kmod = k * (1.0 + (a - 1.0) * ka_ref[...])
    kka = kk * a
    bonus = head_sum(r * kmod * rk_ref[...]) * v

    rt = lax.broadcasted_iota(jnp.int32, (tc, tc), 0)
    ct = lax.broadcasted_iota(jnp.int32, (tc, tc), 1)
    shift = L.bit_length() - 1
    same_chunk = lax.shift_right_logical(rt, shift) == lax.shift_right_logical(ct, shift)
    cum = _dot(jnp.where(same_chunk & (ct <= rt), 1.0, 0.0).astype(F32), logw, precision=HIGHEST)
    p_inv = jnp.exp(-cum)
    r_d = r * jnp.exp(cum)
    kk_d = kk * jnp.exp(cum - logw)
    k_d = kmod * p_inv
    a_d = kka * p_inv
    cum_last = jnp.concatenate(
        [jnp.broadcast_to(cum[(c + 1) * L - 1:(c + 1) * L, :], (L, cum.shape[1])) for c in range(nc)], axis=0)
    p_tail = jnp.exp(cum_last - cum)
    k_t = kmod * p_tail
    a_t = kka * p_tail
    p_last = jnp.exp(cum_last)

    strict = (blk & (ci < ri))[None]
    incl = (blk & (ci <= ri))[None]
    eye = jnp.where(ri == ci, 1.0, 0.0).astype(F32)[None]
    lane_row = lax.broadcasted_iota(jnp.int32, (1, LANES), 1)
    m0 = jnp.where(lane_row < RWKV_HEAD, 1.0, 0.0).astype(F32)
    m1 = 1.0 - m0

    def batch(x):
        out = []
        for c in range(nc):
            for j in range(pp):
                xs = x[c * L:(c + 1) * L, j * LANES:(j + 1) * LANES]
                out.append(jnp.concatenate([xs * m0, xs * m1], axis=0))
        return jnp.stack(out)

    r_b, kk_b, k_b, a_b, v_b, kt_b, at_b = (batch(x) for x in (r_d, kk_d, k_d, a_d, v, k_t, a_t))

    gram = _bdot(jnp.concatenate([kk_b, r_b], axis=1), jnp.concatenate([k_b, a_b], axis=1), _BNT)
    a_kk_k = jnp.where(strict, gram[:, 0:P2, 0:P2], 0.0)
    n_mat = jnp.where(strict, gram[:, 0:P2, P2:], 0.0)
    a_r_k = jnp.where(incl, gram[:, P2:, 0:P2], 0.0)
    a_r_a = jnp.where(incl, gram[:, P2:, P2:], 0.0)

    inv = eye - n_mat
    pw = _bdot(n_mat, n_mat, _BNN)
    steps = 2
    while 2 * steps < L:
        both = _bdot(jnp.concatenate([inv, pw], axis=1), pw, _BNN)
        inv = inv + both[:, 0:P2]
        pw = both[:, P2:]
        steps *= 2
    inv = inv + _bdot(inv, pw, _BNN)
    akk_v = _bdot(a_kk_k, v_b, _BNN)
    ark_v = _bdot(a_r_k, v_b, _BNN)

    s = state[...]
    y_chunks = []
    for c in range(nc):
        sl = slice(c * pp, (c + 1) * pp)
        sa = _bdot(inv[sl], _bdot(kk_b[sl], s, _BNT) + akk_v[sl], _BNN)
        y_s = _bdot(r_b[sl], s, _BNT) + ark_v[sl] - _bdot(a_r_a[sl], sa, _BNN)
        y_c = y_s[:, 0:L, :] + y_s[:, L:, :]
        y_chunks.append(jnp.concatenate([y_c[j] for j in range(pp)], axis=1))
        decay = jnp.stack([p_last[c * L:c * L + 1, j * LANES:(j + 1) * LANES] for j in range(pp)])
        s = s * decay + _bdot(jnp.concatenate([v_b[sl], sa], axis=1),
                              jnp.concatenate([kt_b[sl], -at_b[sl]], axis=1), _BTN)
    state[...] = s

    y = jnp.concatenate(y_chunks, axis=0)
    inv_n = 1.0 / RWKV_HEAD
    mean = head_sum(y) * inv_n
    yc = y - mean
    var = head_sum(yc * yc) * inv_n
    yn = yc * lax.rsqrt(var + GN_EPS) * lnw_ref[...] + lnb_ref[...]
    o_ref[...] = ((yn + bonus) * g).astype(o_ref.dtype)


def _rwkv_branch(z, col0, z_lora, mu, w0, w2, a0, a2, g2, k_k, k_a, r_k, ln_w, ln_b, *, tc, pp):
    bsz, seq, _ = z.shape
    width = w0.shape[0]
    lora_w = z_lora.shape[2]
    bw = pp * LANES
    d_lora, a_lora, g_lora = w2.shape[0], a2.shape[0], g2.shape[0]
    assert width % bw == 0 and col0 % bw == 0 and lora_w % LANES == 0
    assert lora_w >= d_lora + a_lora + g_lora
    assert seq % tc == 0 and tc % RWKV_CHUNK == 0 and tc % SUBLANES == 0
    groups = width // bw
    cb = col0 // bw

    def pad_rows(wm, start):
        return jnp.zeros((lora_w, width), F32).at[start:start + wm.shape[0]].set(wm).astype(BF16)

    w2p = pad_rows(w2, 0)
    a2p = pad_rows(a2, d_lora)
    g2p = pad_rows(g2, d_lora + a_lora)
    mu_pad = jnp.zeros((1, 3 * width + lora_w), F32).at[0, :mu.shape[0]].set(mu)
    mu_r, mu_k, mu_v = (mu_pad[:, i * width:(i + 1) * width] for i in range(3))
    mu_l = mu_pad[:, 3 * width:]
    row = lambda x: x.reshape(1, width)

    act = lambda off: pl.BlockSpec((None, tc, bw), lambda b, p, t: (b, t, cb + off * groups + p))
    vec = pl.BlockSpec((1, bw), lambda b, p, t: (0, p))
    lmat = pl.BlockSpec((lora_w, bw), lambda b, p, t: (0, p))
    return pl.pallas_call(
        functools.partial(_rwkv_kernel, tc=tc, pp=pp),
        grid=(bsz, groups, seq // tc),
        in_specs=[
            act(0), act(1), act(2),
            pl.BlockSpec((None, tc, lora_w), lambda b, p, t: (b, t, 0)),
            vec, vec, vec,
            pl.BlockSpec((1, lora_w), lambda b, p, t: (0, 0)),
            vec, lmat, vec, lmat, lmat, vec, vec, vec, vec, vec,
        ],
        out_specs=pl.BlockSpec((None, tc, bw), lambda b, p, t: (b, t, p)),
        out_shape=jax.ShapeDtypeStruct((bsz, seq, width), BF16),
        scratch_shapes=[
            pltpu.VMEM((tc + SUBLANES, bw), F32), pltpu.VMEM((tc + SUBLANES, bw), F32),
            pltpu.VMEM((tc + SUBLANES, bw), F32), pltpu.VMEM((tc + SUBLANES, lora_w), F32),
            pltpu.VMEM((pp, LANES, LANES), F32),
        ],
        compiler_params=_cparams(("parallel", "parallel", "arbitrary")),
        name="rwkv7",
    )(z, z, z, z_lora, mu_r, mu_k, mu_v, mu_l, row(w0), w2p, row(a0), a2p, g2p, row(k_k), row(k_a),
      row(r_k), row(ln_w), row(ln_b))


def _mlstm_kernel(q_ref, k_ref, v_ref, zo_ref, gcol_ref, grow_ref, bcol_ref, brow_ref, nw_ref, o_ref,
                  c_st, n_st, m_st, *, heads, dqk, dv):
    L = q_ref.shape[0]

    @pl.when(pl.program_id(1) == 0)
    def _():
        c_st[...] = jnp.zeros(c_st.shape, F32)
        n_st[...] = jnp.zeros(n_st.shape, F32)
        m_st[...] = jnp.zeros(m_st.shape, F32)

    ri = lax.broadcasted_iota(jnp.int32, (L, L), 0)
    ci = lax.broadcasted_iota(jnp.int32, (L, L), 1)
    causal = ci <= ri
    tril_ones = jnp.where(causal, 1.0, 0.0).astype(F32)
    triu_ones = jnp.where(ci >= ri, 1.0, 0.0).astype(F32)

    gc = gcol_ref[...] + bcol_ref[...]
    gr = grow_ref[...] + brow_ref[...]
    b_c = _dot(tril_ones, -_softplus(-gc), precision=HIGHEST)
    b_r = _dot(-_softplus(-gr), triu_ones, precision=HIGHEST)

    per_head = lambda f: jnp.stack([f(h) for h in range(heads)])
    q = per_head(lambda h: q_ref[:, h * dqk:(h + 1) * dqk])
    q = (q.astype(F32) * dqk ** -0.5).astype(BF16)
    k = per_head(lambda h: k_ref[:, h * dqk:(h + 1) * dqk])
    v = per_head(lambda h: v_ref[:, h * dv:(h + 1) * dv])
    bc = per_head(lambda h: b_c[:, heads + h:heads + h + 1])
    lic = per_head(lambda h: gc[:, h:h + 1])
    br = per_head(lambda h: b_r[heads + h:heads + h + 1, :])
    lir = per_head(lambda h: gr[h:h + 1, :])
    m_prev = m_st[...][:, 0:1, 0:1]
    c_mat = c_st[...]
    n_vec = n_st[...]

    log_d = jnp.where(causal[None], bc - br + lir, -jnp.inf)
    log_inter = bc + m_prev
    m_t = jnp.maximum(jnp.max(log_d, axis=-1, keepdims=True), log_inter)
    p = _bdot(q, k, _BNT) * jnp.exp(log_d - m_t)
    inter = jnp.exp(log_inter - m_t)
    num = _bdot(p, v, _BNN) + inter * _bdot(q, c_mat, _BNN)
    den = (jnp.sum(p, axis=-1, keepdims=True)
           + inter * jnp.sum(q.astype(F32) * n_vec, axis=-1, keepdims=True))
    hs = num / jnp.maximum(jnp.abs(den), jnp.exp(-m_t))

    b_last = bc[:, L - 1:L, :]
    log_g = b_last - bc + lic
    m_new = jnp.maximum(b_last + m_prev, jnp.max(log_g, axis=1, keepdims=True))
    carry = jnp.exp(b_last + m_prev - m_new)
    wk = k.astype(F32) * jnp.exp(log_g - m_new)
    c_st[...] = carry * c_mat + _bdot(wk, v, _BTN)
    n_st[...] = carry * n_vec + jnp.sum(wk, axis=1, keepdims=True)
    m_st[...] = jnp.broadcast_to(m_new, m_st.shape)

    hs = hs * lax.rsqrt(jnp.mean(hs * hs, axis=-1, keepdims=True) + NORM_EPS)
    for h in range(heads):
        sl = slice(h * dv, (h + 1) * dv)
        o_ref[:, sl] = (hs[h] * nw_ref[:, sl] * _sigmoid(zo_ref[:, sl].astype(F32))).astype(o_ref.dtype)


def _mlstm_branch(z, gates, b_i, b_f, norm_w, *, heads, qk, vdim):
    bsz, seq, _ = z.shape
    L = ML_CHUNK_ROWS
    dqk, dv = qk // heads, vdim // heads
    assert seq % L == 0 and vdim % qk == 0 and qk % LANES == 0 and (2 * qk) % vdim == 0
    vblk = 2 * qk // vdim
    bias = jnp.concatenate([b_i, b_f]).astype(F32)
    bias_col = jnp.zeros((1, LANES), F32).at[0, :2 * heads].set(bias)
    gates_row = jnp.swapaxes(gates[:, :, :2 * heads], 1, 2)
    return pl.pallas_call(
        functools.partial(_mlstm_kernel, heads=heads, dqk=dqk, dv=dv),
        grid=(bsz, seq // L),
        in_specs=[
            pl.BlockSpec((None, L, qk), lambda b, c: (b, c, 0)),
            pl.BlockSpec((None, L, qk), lambda b, c: (b, c, 1)),
            pl.BlockSpec((None, L, vdim), lambda b, c: (b, c, vblk)),
            pl.BlockSpec((None, L, vdim), lambda b, c: (b, c, vblk + 1)),
            pl.BlockSpec((None, L, LANES), lambda b, c: (b, c, 0)),
            pl.BlockSpec((None, 2 * heads, L), lambda b, c: (b, 0, c)),
            pl.BlockSpec((1, LANES), lambda b, c: (0, 0)),
            pl.BlockSpec((2 * heads, 1), lambda b, c: (0, 0)),
            pl.BlockSpec((1, vdim), lambda b, c: (0, 0)),
        ],
        out_specs=pl.BlockSpec((None, L, vdim), lambda b, c: (b, c, 0)),
        out_shape=jax.ShapeDtypeStruct((bsz, seq, vdim), BF16),
        scratch_shapes=[
            pltpu.VMEM((heads, dqk, dv), F32),
            pltpu.VMEM((heads, 1, dqk), F32),
            pltpu.VMEM((heads, SUBLANES, LANES), F32),
        ],
        compiler_params=_cparams(("parallel", "arbitrary")),
        name="mlstm",
    )(z, z, z, z, gates, gates_row, bias_col, bias.reshape(-1, 1), norm_w.reshape(1, vdim))


def _col_tile(n, cap):
    for step in (MXU_WIDTH, LANES):
        fits = [c for c in range(step, cap + 1, step) if n % c == 0]
        if fits:
            return fits[-1]
    raise ValueError(f"no lane-aligned column tile divides {n}")


def _tail_rows(wt, start):
    tail = wt[:, start:, :]
    pad = (-tail.shape[1]) % LANES
    return jnp.pad(tail, ((0, 0), (0, pad), (0, 0)))


def kernel(x, norm_mix, norm_mlp, norm_final, mlp_up, mlp_down, hy_in, lru_conv_w, lru_conv_b, lru_wa, lru_ba, lru_wx, lru_bx, lru_lam, rwkv_mu, rwkv_w0, rwkv_w2, rwkv_a0, rwkv_a2, rwkv_g2, rwkv_kk, rwkv_ka, rwkv_rk, rwkv_ln_w, rwkv_ln_b, hy_out, ml_in, ml_bi, ml_bf, ml_norm, ml_out):
    bsz, seq, d = x.shape
    t = bsz * seq
    depth = norm_mix.shape[0]
    xf = x.reshape(t, d)

    lru_w = lru_lam.shape[1]
    rw_w = rwkv_w0.shape[1]
    hy_main = 2 * lru_w + 3 * rw_w
    heads = ml_bi.shape[1]
    vdim = ml_norm.shape[1]
    qk = (ml_in.shape[2] - 2 * vdim - 2 * heads) // 2
    ml_main = 2 * qk + 2 * vdim

    hy_in_t, ml_in_t = jnp.swapaxes(hy_in, 1, 2), jnp.swapaxes(ml_in, 1, 2)
    hy_lora_t, ml_gate_t = _tail_rows(hy_in_t, hy_main), _tail_rows(ml_in_t, ml_main)
    tn = lambda n: _col_tile(n, COL_TILE)
    tk_out = min(K_TILE_OUT, lru_w, vdim)

    for layer in range(depth):
        hn = _norm_cast(xf, norm_mix[layer], BF16)
        if layer % 2 == 0:
            e = layer // 2
            z = _matmul([hn], hy_in_t, e, hy_main, tk=d, tn=tn(hy_main), out_dtype=BF16, w_t=True)
            z = z.reshape(bsz, seq, hy_main)
            n_lora = hy_lora_t.shape[1]
            z_lora = _matmul([hn], hy_lora_t, e, n_lora, tk=d, tn=tn(n_lora), w_t=True)
            z_lora = z_lora.reshape(bsz, seq, n_lora)
            ya = _lru_branch(z, lru_conv_w[e], lru_conv_b[e], lru_wa[e], lru_ba[e], lru_wx[e], lru_bx[e],
                             lru_lam[e], tc=min(LRU_ROWS, seq))
            yb = _rwkv_branch(z, 2 * lru_w, z_lora, rwkv_mu[e], rwkv_w0[e], rwkv_w2[e], rwkv_a0[e], rwkv_a2[e],
                              rwkv_g2[e], rwkv_kk[e], rwkv_ka[e], rwkv_rk[e], rwkv_ln_w[e], rwkv_ln_b[e],
                              tc=min(RWKV_ROWS, seq), pp=min(RWKV_PAIRS, rw_w // LANES))
            lhs = [ya.reshape(t, lru_w), yb.reshape(t, rw_w)]
            assert lru_w == rw_w
            xf = _matmul(lhs, hy_out, e, d, tk=tk_out, tn=tn(d), res=xf)
        else:
            o = layer // 2
            z = _matmul([hn], ml_in_t, o, ml_main, tk=d, tn=tn(ml_main), out_dtype=BF16, w_t=True)
            z = z.reshape(bsz, seq, ml_main)
            n_gate = ml_gate_t.shape[1]
            gates = _matmul([hn], ml_gate_t, o, n_gate, tk=d, tn=tn(n_gate), w_t=True)
            gates = gates.reshape(bsz, seq, n_gate)
            hs = _mlstm_branch(z, gates, ml_bi[o], ml_bf[o], ml_norm[o], heads=heads, qk=qk, vdim=vdim)
            xf = _matmul([hs.reshape(t, vdim)], ml_out, o, d, tk=tk_out, tn=tn(d), res=xf)
        hn = _norm_cast(xf, norm_mlp[layer], BF16)
        u = _matmul([hn], mlp_up, layer, mlp_up.shape[2], tk=d, tn=tn(mlp_up.shape[2]), out_dtype=BF16,
                    relu2=True)
        xf = _matmul([u], mlp_down, layer, d, tk=min(K_TILE_DOWN, mlp_down.shape[1]),
                     tn=_col_tile(d, DOWN_COL_TILE), res=xf)
    return _norm_cast(xf, norm_final, F32).reshape(bsz, seq, d)
```

```python
import functools

import jax
import jax.numpy as jnp
from jax import lax
from jax.experimental import pallas as pl
from jax.experimental.pallas import tpu as pltpu

F32 = jnp.float32
BF16 = jnp.bfloat16
HIGHEST = lax.Precision.HIGHEST

NORM_EPS = 1e-6
GN_EPS = 64e-5
LRU_C = 8.0
CONV_WIDTH = 4
RWKV_HEAD = 64
DECAY_SCALE = 0.6065306597126334
LANES = 128
SUBLANES = 8
MXU_WIDTH = 256
VMEM_LIMIT_BYTES = 60 * 1024 * 1024
RWKV_CHUNK = 64
RWKV_ROWS = 128
RWKV_PAIRS = 16
LRU_ROWS = 256
ML_CHUNK_ROWS = 256
NORM_ROWS = 512
ROW_TILE = 2048
COL_TILE = 512
K_TILE_OUT = 2048
DOWN_COL_TILE = 1024
K_TILE_DOWN = 1024


def _cparams(semantics):
    return pltpu.CompilerParams(dimension_semantics=semantics, vmem_limit_bytes=VMEM_LIMIT_BYTES)


def _softplus(x):
    return jnp.maximum(x, 0.0) + jnp.log1p(jnp.exp(-jnp.abs(x)))


def _sigmoid(x):
    return 1.0 / (1.0 + jnp.exp(-x))


_NN = (((1,), (0,)), ((), ()))
_NT = (((1,), (1,)), ((), ()))
_TN = (((0,), (0,)), ((), ()))
_BNN = (((2,), (1,)), ((0,), (0,)))
_BNT = (((2,), (2,)), ((0,), (0,)))
_BTN = (((1,), (1,)), ((0,), (0,)))


def _dot(a, b, dims=_NN, precision=None):
    return lax.dot_general(a, b, dims, precision=precision, preferred_element_type=F32)


def _bdot(a, b, dims=_NN):
    return lax.dot_general(a.astype(BF16), b.astype(BF16), dims, preferred_element_type=F32)


def _norm_cast_kernel(x_ref, g_ref, o_ref):
    x = x_ref[...]
    ms = jnp.mean(x * x, axis=-1, keepdims=True)
    o_ref[...] = (x * lax.rsqrt(ms + NORM_EPS) * g_ref[...]).astype(o_ref.dtype)


def _norm_cast(x, g, out_dtype):
    t, d = x.shape
    tm = min(NORM_ROWS, t)
    return pl.pallas_call(
        _norm_cast_kernel,
        grid=(t // tm,),
        in_specs=[pl.BlockSpec((tm, d), lambda i: (i, 0)), pl.BlockSpec((1, d), lambda i: (0, 0))],
        out_specs=pl.BlockSpec((tm, d), lambda i: (i, 0)),
        out_shape=jax.ShapeDtypeStruct((t, d), out_dtype),
        compiler_params=_cparams(("parallel",)),
        name="rmsnorm",
    )(x, g.reshape(1, d))


def _matmul_kernel(*refs, n_lhs, nk, has_res, relu2, w_t):
    a_refs, w_ref = refs[:n_lhs], refs[n_lhs]
    r_ref = refs[n_lhs + 1] if has_res else None
    o_ref = refs[n_lhs + 1 + has_res]
    acc_ref = refs[-1] if (not has_res and nk > 1) else None
    kk = pl.program_id(2)

    def finish(acc):
        if relu2:
            acc = jnp.maximum(acc, 0.0)
            acc = acc * acc
        return acc.astype(o_ref.dtype)

    def accumulate(a_ref):
        def part():
            return lax.dot_general(a_ref[...], w_ref[...].astype(BF16), _NT if w_t else _NN,
                                   preferred_element_type=F32)

        if has_res:
            @pl.when(kk == 0)
            def _():
                o_ref[...] = r_ref[...] + part()

            @pl.when(kk > 0)
            def _():
                o_ref[...] += part()
        elif nk == 1:
            o_ref[...] = finish(part())
        else:
            @pl.when(kk == 0)
            def _():
                acc_ref[...] = part()

            @pl.when((kk > 0) & (kk < nk - 1))
            def _():
                acc_ref[...] += part()

            @pl.when(kk == nk - 1)
            def _():
                o_ref[...] = finish(acc_ref[...] + part())

    if n_lhs == 1:
        accumulate(a_refs[0])
    else:
        per = nk // n_lhs
        for idx, a_ref in enumerate(a_refs):
            pl.when((kk >= idx * per) & (kk < (idx + 1) * per))(functools.partial(accumulate, a_ref))


def _matmul(lhs, w, layer, n, *, tk, tn, tm=ROW_TILE, out_dtype=F32, res=None, relu2=False, w_t=False):
    t = lhs[0].shape[0]
    k = w.shape[2 if w_t else 1]
    tm = min(tm, t)
    nk = k // tk
    assert t % tm == 0 and k % tk == 0 and n % tn == 0
    per = nk // len(lhs)
    assert all(a.shape[1] == per * tk for a in lhs) and per * len(lhs) == nk
    once = dict(pipeline_mode=pl.Buffered(1)) if per == 1 else {}

    def lhs_spec(idx):
        return pl.BlockSpec((tm, tk), lambda i, j, kk: (i, jnp.clip(kk - idx * per, 0, per - 1)), **once)

    a_specs = [lhs_spec(idx) for idx in range(len(lhs))]
    has_res = res is not None
    assert not (has_res and (relu2 or out_dtype != F32))
    tile = pl.BlockSpec((tm, tn), lambda i, j, kk: (i, j))
    scratch = [pltpu.VMEM((tm, tn), F32)] if (not has_res and nk > 1) else []
    if w_t:
        w_spec = pl.BlockSpec((None, tn, tk), lambda i, j, kk: (layer, j, kk))
    else:
        w_spec = pl.BlockSpec((None, tk, tn), lambda i, j, kk: (layer, kk, j))
    return pl.pallas_call(
        functools.partial(_matmul_kernel, n_lhs=len(lhs), nk=nk, has_res=has_res, relu2=relu2, w_t=w_t),
        grid=(t // tm, n // tn, nk),
        in_specs=a_specs + [w_spec] + ([tile] if has_res else []),
        out_specs=tile,
        out_shape=jax.ShapeDtypeStruct((t, n), out_dtype),
        scratch_shapes=scratch,
        compiler_params=_cparams(("parallel", "parallel", "arbitrary")),
        name="matmul",
    )(*lhs, w, *([res] if has_res else []))


def _lru_kernel(u_ref, gate_ref, cw_ref, cb_ref, wa_ref, ba_ref, wx_ref, bx_ref, lam_ref,
                o_ref, xbuf, hcar, *, tc):
    @pl.when(pl.program_id(2) == 0)
    def _():
        xbuf[0:SUBLANES, :] = jnp.zeros((SUBLANES, xbuf.shape[1]), F32)
        hcar[...] = jnp.zeros(hcar.shape, F32)

    u = u_ref[...].astype(F32)
    xbuf[SUBLANES:, :] = u
    cw = cw_ref[...]
    conv = cb_ref[...] + cw[CONV_WIDTH - 1:CONV_WIDTH] * u
    for j in range(CONV_WIDTH - 1):
        back = CONV_WIDTH - 1 - j
        conv = conv + cw[j:j + 1] * xbuf[SUBLANES - back:SUBLANES - back + tc, :]
    xbuf[0:SUBLANES, :] = u[tc - SUBLANES:, :]

    cbf = conv.astype(BF16)
    r = _sigmoid(jnp.dot(cbf, wa_ref[...], preferred_element_type=F32) + ba_ref[...])
    i = _sigmoid(jnp.dot(cbf, wx_ref[...], preferred_element_type=F32) + bx_ref[...])
    log_a = (-LRU_C) * r * _softplus(-lam_ref[...])
    a = jnp.exp(log_a)
    b = jnp.sqrt(-jnp.tanh(log_a) * (a * a + 1.0)) * (i * conv)

    n_tiles = tc // SUBLANES
    a = a.reshape(n_tiles, SUBLANES, a.shape[1])
    b = b.reshape(n_tiles, SUBLANES, b.shape[1])
    sub = lax.broadcasted_iota(jnp.int32, a.shape, 1)
    d = 1
    while d < SUBLANES:
        a_sh = pltpu.roll(a, d, 1)
        b_sh = pltpu.roll(b, d, 1)
        keep = sub >= d
        b = jnp.where(keep, a * b_sh + b, b)
        a = jnp.where(keep, a * a_sh, a)
        d *= 2
    carry = hcar[...]
    tiles = []
    for i in range(n_tiles):
        tiles.append(a[i] * carry + b[i])
        carry = tiles[-1][SUBLANES - 1:SUBLANES, :]
    h = jnp.concatenate(tiles, axis=0)
    hcar[...] = carry
    o_ref[...] = (h * jax.nn.gelu(gate_ref[...].astype(F32))).astype(o_ref.dtype)


def _lru_branch(z, conv_w, conv_b, w_a, b_a, w_x, b_x, lam, *, tc):
    bsz, seq, _ = z.shape
    heads, blk, _ = w_a.shape
    width = heads * blk
    assert blk % LANES == 0 and seq % tc == 0
    row = lambda v: v.reshape(1, width)
    vec_spec = pl.BlockSpec((1, blk), lambda b, h, t: (0, h))
    mat_spec = pl.BlockSpec((None, blk, blk), lambda b, h, t: (h, 0, 0))
    return pl.pallas_call(
        functools.partial(_lru_kernel, tc=tc),
        grid=(bsz, heads, seq // tc),
        in_specs=[
            pl.BlockSpec((None, tc, blk), lambda b, h, t: (b, t, h)),
            pl.BlockSpec((None, tc, blk), lambda b, h, t: (b, t, heads + h)),
            pl.BlockSpec((CONV_WIDTH, blk), lambda b, h, t: (0, h)),
            vec_spec, mat_spec, vec_spec, mat_spec, vec_spec, vec_spec,
        ],
        out_specs=pl.BlockSpec((None, tc, blk), lambda b, h, t: (b, t, h)),
        out_shape=jax.ShapeDtypeStruct((bsz, seq, width), BF16),
        scratch_shapes=[pltpu.VMEM((tc + SUBLANES, blk), F32), pltpu.VMEM((1, blk), F32)],
        compiler_params=_cparams(("parallel", "parallel", "arbitrary")),
        name="rglru",
    )(z, z, conv_w, row(conv_b), w_a.astype(BF16), row(b_a), w_x.astype(BF16), row(b_x), row(lam))


def _rwkv_kernel(r_ref, k_ref, v_ref, lora_ref, mu_r, mu_k, mu_v, mu_l, w0_ref, w2_ref, a0_ref, a2_ref,
                 g2_ref, kk_ref, ka_ref, rk_ref, lnw_ref, lnb_ref, o_ref,
                 rbuf, kbuf, vbuf, lbuf, state, *, tc, pp):
    L = RWKV_CHUNK
    P2 = 2 * L
    nc = tc // L
    assert P2 == LANES and RWKV_HEAD == L

    @pl.when(pl.program_id(2) == 0)
    def _():
        for buf in (rbuf, kbuf, vbuf, lbuf):
            buf[0:SUBLANES, :] = jnp.zeros((SUBLANES, buf.shape[1]), F32)
        state[...] = jnp.zeros(state.shape, F32)

    def token_shift(ref, buf, mu_ref):
        cur = ref[...].astype(F32)
        buf[SUBLANES:, :] = cur
        prev = buf[SUBLANES - 1:SUBLANES - 1 + tc, :]
        buf[0:SUBLANES, :] = cur[tc - SUBLANES:, :]
        return cur + (prev - cur) * mu_ref[...]

    r = token_shift(r_ref, rbuf, mu_r)
    k = token_shift(k_ref, kbuf, mu_k)
    v = token_shift(v_ref, vbuf, mu_v)
    lora = token_shift(lora_ref, lbuf, mu_l)

    dw = _dot(jnp.tanh(lora).astype(BF16), w2_ref[...])
    logw = -DECAY_SCALE * _sigmoid(w0_ref[...] + dw)
    a = _sigmoid(a0_ref[...] + _dot(lora.astype(BF16), a2_ref[...]))
    g = _dot(_sigmoid(lora).astype(BF16), g2_ref[...])

    ri = lax.broadcasted_iota(jnp.int32, (P2, P2), 0)
    ci = lax.broadcasted_iota(jnp.int32, (P2, P2), 1)
    blk = jnp.where(ri >= L, 1, 0) == jnp.where(ci >= L, 1, 0)
    head_ones = jnp.where(blk, 1.0, 0.0).astype(BF16)

    def to_rows(x):
        return jnp.concatenate([x[:, j * LANES:(j + 1) * LANES] for j in range(pp)], axis=0)

    def from_rows(x):
        return jnp.concatenate([x[j * tc:(j + 1) * tc, :] for j in range(pp)], axis=1)

    def head_sum(x):
        return from_rows(_dot(to_rows(x).astype(BF16), head_ones))

    kk = k * kk_ref[...]
    kk = kk * lax.rsqrt(jnp.maximum(head_sum(kk * kk), 1e-24))
    kmod = k * (1.0 + (a - 1.0) * ka_ref[...])
    kka = kk * a
    bonus = head_sum(r * kmod * rk_ref[...]) * v

    rt = lax.broadcasted_iota(jnp.int32, (tc, tc), 0)
    ct = lax.broadcasted_iota(jnp.int32, (tc, tc), 1)
    shift = L.bit_length() - 1
    same_chunk = lax.shift_right_logical(rt, shift) == lax.shift_right_logical(ct, shift)
    cum = _dot(jnp.where(same_chunk & (ct <= rt), 1.0, 0.0).astype(F32), logw, precision=HIGHEST)
    p_inv = jnp.exp(-cum)
    r_d = r * jnp.exp(cum)
    kk_d = kk * jnp.exp(cum - logw)
    k_d = kmod * p_inv
    a_d = kka * p_inv
    p_last = [jnp.exp(cum[(c + 1) * L - 1:(c + 1) * L, :]) for c in range(nc)]
    p_tail = p_inv * jnp.concatenate([jnp.broadcast_to(p, (L, cum.shape[1])) for p in p_last], axis=0)
    k_t = kmod * p_tail
    a_t = kka * p_tail

    strict = (blk & (ci < ri))[None]
    incl = (blk & (ci <= ri))[None]
    eye = jnp.where(ri == ci, 1.0, 0.0).astype(F32)[None]
    lane_row = lax.broadcasted_iota(jnp.int32, (1, LANES), 1)
    m0 = jnp.where(lane_row < RWKV_HEAD, 1.0, 0.0).astype(F32)
    m1 = 1.0 - m0

    def batch(x):
        out = []
        for c in range(nc):
            for j in range(pp):
                xs = x[c * L:(c + 1) * L, j * LANES:(j + 1) * LANES]
                out.append(jnp.concatenate([xs * m0, xs * m1], axis=0))
        return jnp.stack(out)

    r_b, kk_b, k_b, a_b, v_b, kt_b, at_b = (batch(x) for x in (r_d, kk_d, k_d, a_d, v, k_t, a_t))

    gram = _bdot(jnp.concatenate([kk_b, r_b], axis=1), jnp.concatenate([k_b, a_b], axis=1), _BNT)
    a_kk_k = jnp.where(strict, gram[:, 0:P2, 0:P2], 0.0)
    n_mat = jnp.where(strict, gram[:, 0:P2, P2:], 0.0)
    a_r_k = jnp.where(incl, gram[:, P2:, 0:P2], 0.0)
    a_r_a = jnp.where(incl, gram[:, P2:, P2:], 0.0)

    inv = eye - n_mat
    pw = _bdot(n_mat, n_mat, _BNN)
    steps = 2
    while 2 * steps < L:
        both = _bdot(jnp.concatenate([inv, pw], axis=1), pw, _BNN)
        inv = inv + both[:, 0:P2]
        pw = both[:, P2:]
        steps *= 2
    inv = inv + _bdot(inv, pw, _BNN)
    akk_v = _bdot(a_kk_k, v_b, _BNN)
    ark_v = _bdot(a_r_k, v_b, _BNN)

    s = state[...]
    y_chunks = []
    for c in range(nc):
        sl = slice(c * pp, (c + 1) * pp)
        sa = _bdot(inv[sl], _bdot(kk_b[sl], s, _BNT) + akk_v[sl], _BNN)
        y_s = _bdot(r_b[sl], s, _BNT) + ark_v[sl] - _bdot(a_r_a[sl], sa, _BNN)
        y_c = y_s[:, 0:L, :] + y_s[:, L:, :]
        y_chunks.append(jnp.concatenate([y_c[j] for j in range(pp)], axis=1))
        decay = jnp.stack([p_last[c][:, j * LANES:(j + 1) * LANES] for j in range(pp)])
        s = s * decay + _bdot(jnp.concatenate([v_b[sl], sa], axis=1),
                              jnp.concatenate([kt_b[sl], -at_b[sl]], axis=1), _BTN)
    state[...] = s

    y = jnp.concatenate(y_chunks, axis=0)
    inv_n = 1.0 / RWKV_HEAD
    mean = head_sum(y) * inv_n
    yc = y - mean
    var = head_sum(yc * yc) * inv_n
    yn = yc * lax.rsqrt(var + GN_EPS) * lnw_ref[...] + lnb_ref[...]
    o_ref[...] = ((yn + bonus) * g).astype(o_ref.dtype)


def _rwkv_branch(z, col0, z_lora, mu, w0, w2, a0, a2, g2, k_k, k_a, r_k, ln_w, ln_b, *, tc, pp):
    bsz, seq, _ = z.shape
    width = w0.shape[0]
    lora_w = z_lora.shape[2]
    bw = pp * LANES
    d_lora, a_lora, g_lora = w2.shape[0], a2.shape[0], g2.shape[0]
    assert width % bw == 0 and col0 % bw == 0 and lora_w % LANES == 0
    assert lora_w >= d_lora + a_lora + g_lora
    assert seq % tc == 0 and tc % RWKV_CHUNK == 0 and tc % SUBLANES == 0
    groups = width // bw
    cb = col0 // bw

    def pad_rows(wm, start):
        return jnp.zeros((lora_w, width), F32).at[start:start + wm.shape[0]].set(wm).astype(BF16)

    w2p = pad_rows(w2, 0)
    a2p = pad_rows(a2, d_lora)
    g2p = pad_rows(g2, d_lora + a_lora)
    mu_pad = jnp.zeros((1, 3 * width + lora_w), F32).at[0, :mu.shape[0]].set(mu)
    mu_r, mu_k, mu_v = (mu_pad[:, i * width:(i + 1) * width] for i in range(3))
    mu_l = mu_pad[:, 3 * width:]
    row = lambda x: x.reshape(1, width)

    act = lambda off: pl.BlockSpec((None, tc, bw), lambda b, p, t: (b, t, cb + off * groups + p))
    vec = pl.BlockSpec((1, bw), lambda b, p, t: (0, p))
    lmat = pl.BlockSpec((lora_w, bw), lambda b, p, t: (0, p))
    return pl.pallas_call(
        functools.partial(_rwkv_kernel, tc=tc, pp=pp),
        grid=(bsz, groups, seq // tc),
        in_specs=[
            act(0), act(1), act(2),
            pl.BlockSpec((None, tc, lora_w), lambda b, p, t: (b, t, 0)),
            vec, vec, vec,
            pl.BlockSpec((1, lora_w), lambda b, p, t: (0, 0)),
            vec, lmat, vec, lmat, lmat, vec, vec, vec, vec, vec,
        ],
        out_specs=pl.BlockSpec((None, tc, bw), lambda b, p, t: (b, t, p)),
        out_shape=jax.ShapeDtypeStruct((bsz, seq, width), BF16),
        scratch_shapes=[
            pltpu.VMEM((tc + SUBLANES, bw), F32), pltpu.VMEM((tc + SUBLANES, bw), F32),
            pltpu.VMEM((tc + SUBLANES, bw), F32), pltpu.VMEM((tc + SUBLANES, lora_w), F32),
            pltpu.VMEM((pp, LANES, LANES), F32),
        ],
        compiler_params=_cparams(("parallel", "parallel", "arbitrary")),
        name="rwkv7",
    )(z, z, z, z_lora, mu_r, mu_k, mu_v, mu_l, row(w0), w2p, row(a0), a2p, g2p, row(k_k), row(k_a),
      row(r_k), row(ln_w), row(ln_b))


def _mlstm_kernel(q_ref, k_ref, v_ref, zo_ref, gcol_ref, grow_ref, bcol_ref, brow_ref, nw_ref, o_ref,
                  c_st, n_st, m_st, *, heads, dqk, dv):
    L = q_ref.shape[0]

    @pl.when(pl.program_id(1) == 0)
    def _():
        c_st[...] = jnp.zeros(c_st.shape, F32)
        n_st[...] = jnp.zeros(n_st.shape, F32)
        m_st[...] = jnp.zeros(m_st.shape, F32)

    ri = lax.broadcasted_iota(jnp.int32, (L, L), 0)
    ci = lax.broadcasted_iota(jnp.int32, (L, L), 1)
    causal = ci <= ri
    tril_ones = jnp.where(causal, 1.0, 0.0).astype(F32)
    triu_ones = jnp.where(ci >= ri, 1.0, 0.0).astype(F32)

    gc = gcol_ref[...] + bcol_ref[...]
    gr = grow_ref[...] + brow_ref[...]
    b_c = _dot(tril_ones, -_softplus(-gc), precision=HIGHEST)
    b_r = _dot(-_softplus(-gr), triu_ones, precision=HIGHEST)

    per_head = lambda f: jnp.stack([f(h) for h in range(heads)])
    q = per_head(lambda h: q_ref[:, h * dqk:(h + 1) * dqk])
    q = (q.astype(F32) * dqk ** -0.5).astype(BF16)
    k = per_head(lambda h: k_ref[:, h * dqk:(h + 1) * dqk])
    v = per_head(lambda h: v_ref[:, h * dv:(h + 1) * dv])
    bc = per_head(lambda h: b_c[:, heads + h:heads + h + 1])
    lic = per_head(lambda h: gc[:, h:h + 1])
    br = per_head(lambda h: b_r[heads + h:heads + h + 1, :])
    lir = per_head(lambda h: gr[h:h + 1, :])
    m_prev = m_st[...][:, 0:1, 0:1]
    c_mat = c_st[...]
    n_vec = n_st[...]

    log_d = jnp.where(causal[None], bc - br + lir, -jnp.inf)
    log_inter = bc + m_prev
    m_t = jnp.maximum(jnp.max(log_d, axis=-1, keepdims=True), log_inter)
    p = _bdot(q, k, _BNT) * jnp.exp(log_d - m_t)
    inter = jnp.exp(log_inter - m_t)
    num = _bdot(p, v, _BNN) + inter * _bdot(q, c_mat, _BNN)
    den = (jnp.sum(p, axis=-1, keepdims=True)
           + inter * jnp.sum(q.astype(F32) * n_vec, axis=-1, keepdims=True))
    hs = num / jnp.maximum(jnp.abs(den), jnp.exp(-m_t))

    b_last = bc[:, L - 1:L, :]
    log_g = b_last - bc + lic
    m_new = jnp.maximum(b_last + m_prev, jnp.max(log_g, axis=1, keepdims=True))
    carry = jnp.exp(b_last + m_prev - m_new)
    wk = k.astype(F32) * jnp.exp(log_g - m_new)
    c_st[...] = carry * c_mat + _bdot(wk, v, _BTN)
    n_st[...] = carry * n_vec + jnp.sum(wk, axis=1, keepdims=True)
    m_st[...] = jnp.broadcast_to(m_new, m_st.shape)

    hs = hs * lax.rsqrt(jnp.mean(hs * hs, axis=-1, keepdims=True) + NORM_EPS)
    for h in range(heads):
        sl = slice(h * dv, (h + 1) * dv)
        o_ref[:, sl] = (hs[h] * nw_ref[:, sl] * _sigmoid(zo_ref[:, sl].astype(F32))).astype(o_ref.dtype)


def _mlstm_branch(z, gates, b_i, b_f, norm_w, *, heads, qk, vdim):
    bsz, seq, _ = z.shape
    L = ML_CHUNK_ROWS
    dqk, dv = qk // heads, vdim // heads
    assert seq % L == 0 and vdim % qk == 0 and qk % LANES == 0 and (2 * qk) % vdim == 0
    vblk = 2 * qk // vdim
    bias = jnp.concatenate([b_i, b_f]).astype(F32)
    bias_col = jnp.zeros((1, LANES), F32).at[0, :2 * heads].set(bias)
    gates_row = jnp.swapaxes(gates[:, :, :2 * heads], 1, 2)
    return pl.pallas_call(
        functools.partial(_mlstm_kernel, heads=heads, dqk=dqk, dv=dv),
        grid=(bsz, seq // L),
        in_specs=[
            pl.BlockSpec((None, L, qk), lambda b, c: (b, c, 0)),
            pl.BlockSpec((None, L, qk), lambda b, c: (b, c, 1)),
            pl.BlockSpec((None, L, vdim), lambda b, c: (b, c, vblk)),
            pl.BlockSpec((None, L, vdim), lambda b, c: (b, c, vblk + 1)),
            pl.BlockSpec((None, L, LANES), lambda b, c: (b, c, 0)),
            pl.BlockSpec((None, 2 * heads, L), lambda b, c: (b, 0, c)),
            pl.BlockSpec((1, LANES), lambda b, c: (0, 0)),
            pl.BlockSpec((2 * heads, 1), lambda b, c: (0, 0)),
            pl.BlockSpec((1, vdim), lambda b, c: (0, 0)),
        ],
        out_specs=pl.BlockSpec((None, L, vdim), lambda b, c: (b, c, 0)),
        out_shape=jax.ShapeDtypeStruct((bsz, seq, vdim), BF16),
        scratch_shapes=[
            pltpu.VMEM((heads, dqk, dv), F32),
            pltpu.VMEM((heads, 1, dqk), F32),
            pltpu.VMEM((heads, SUBLANES, LANES), F32),
        ],
        compiler_params=_cparams(("parallel", "arbitrary")),
        name="mlstm",
    )(z, z, z, z, gates, gates_row, bias_col, bias.reshape(-1, 1), norm_w.reshape(1, vdim))


def _col_tile(n, cap):
    for step in (MXU_WIDTH, LANES):
        fits = [c for c in range(step, cap + 1, step) if n % c == 0]
        if fits:
            return fits[-1]
    raise ValueError(f"no lane-aligned column tile divides {n}")


def _tail_rows(wt, start):
    tail = wt[:, start:, :]
    pad = (-tail.shape[1]) % LANES
    return jnp.pad(tail, ((0, 0), (0, pad), (0, 0)))


def kernel(x, norm_mix, norm_mlp, norm_final, mlp_up, mlp_down, hy_in, lru_conv_w, lru_conv_b, lru_wa, lru_ba, lru_wx, lru_bx, lru_lam, rwkv_mu, rwkv_w0, rwkv_w2, rwkv_a0, rwkv_a2, rwkv_g2, rwkv_kk, rwkv_ka, rwkv_rk, rwkv_ln_w, rwkv_ln_b, hy_out, ml_in, ml_bi, ml_bf, ml_norm, ml_out):
    bsz, seq, d = x.shape
    t = bsz * seq
    depth = norm_mix.shape[0]
    xf = x.reshape(t, d)

    lru_w = lru_lam.shape[1]
    rw_w = rwkv_w0.shape[1]
    hy_main = 2 * lru_w + 3 * rw_w
    heads = ml_bi.shape[1]
    vdim = ml_norm.shape[1]
    qk = (ml_in.shape[2] - 2 * vdim - 2 * heads) // 2
    ml_main = 2 * qk + 2 * vdim

    hy_in_t, ml_in_t = jnp.swapaxes(hy_in, 1, 2), jnp.swapaxes(ml_in, 1, 2)
    hy_lora_t, ml_gate_t = _tail_rows(hy_in_t, hy_main), _tail_rows(ml_in_t, ml_main)
    tn = lambda n: _col_tile(n, COL_TILE)
    tk_out = min(K_TILE_OUT, lru_w, vdim)

    for layer in range(depth):
        hn = _norm_cast(xf, norm_mix[layer], BF16)
        if layer % 2 == 0:
            e = layer // 2
            z = _matmul([hn], hy_in_t, e, hy_main, tk=d, tn=tn(hy_main), out_dtype=BF16, w_t=True)
            z = z.reshape(bsz, seq, hy_main)
            n_lora = hy_lora_t.shape[1]
            z_lora = _matmul([hn], hy_lora_t, e, n_lora, tk=d, tn=tn(n_lora), w_t=True)
            z_lora = z_lora.reshape(bsz, seq, n_lora)
            ya = _lru_branch(z, lru_conv_w[e], lru_conv_b[e], lru_wa[e], lru_ba[e], lru_wx[e], lru_bx[e],
                             lru_lam[e], tc=min(LRU_ROWS, seq))
            yb = _rwkv_branch(z, 2 * lru_w, z_lora, rwkv_mu[e], rwkv_w0[e], rwkv_w2[e], rwkv_a0[e], rwkv_a2[e],
                              rwkv_g2[e], rwkv_kk[e], rwkv_ka[e], rwkv_rk[e], rwkv_ln_w[e], rwkv_ln_b[e],
                              tc=min(RWKV_ROWS, seq), pp=min(RWKV_PAIRS, rw_w // LANES))
            lhs = [ya.reshape(t, lru_w), yb.reshape(t, rw_w)]
            assert lru_w == rw_w
            xf = _matmul(lhs, hy_out, e, d, tk=tk_out, tn=tn(d), res=xf)
        else:
            o = layer // 2
            z = _matmul([hn], ml_in_t, o, ml_main, tk=d, tn=tn(ml_main), out_dtype=BF16, w_t=True)
            z = z.reshape(bsz, seq, ml_main)
            n_gate = ml_gate_t.shape[1]
            gates = _matmul([hn], ml_gate_t, o, n_gate, tk=d, tn=tn(n_gate), w_t=True)
            gates = gates.reshape(bsz, seq, n_gate)
            hs = _mlstm_branch(z, gates, ml_bi[o], ml_bf[o], ml_norm[o], heads=heads, qk=qk, vdim=vdim)
            xf = _matmul([hs.reshape(t, vdim)], ml_out, o, d, tk=tk_out, tn=tn(d), res=xf)
        hn = _norm_cast(xf, norm_mlp[layer], BF16)
        u = _matmul([hn], mlp_up, layer, mlp_up.shape[2], tk=d, tn=tn(mlp_up.shape[2]), out_dtype=BF16,
                    relu2=True)
        xf = _matmul([u], mlp_down, layer, d, tk=min(K_TILE_DOWN, mlp_down.shape[1]),
                     tn=_col_tile(d, DOWN_COL_TILE), res=xf)
    return _norm_cast(xf, norm_final, F32).reshape(bsz, seq, d)
```

```python
import functools

import jax
import jax.numpy as jnp
from jax import lax
from jax.experimental import pallas as pl
from jax.experimental.pallas import tpu as pltpu

F32 = jnp.float32
BF16 = jnp.bfloat16
HIGHEST = lax.Precision.HIGHEST

NORM_EPS = 1e-6
GN_EPS = 64e-5
LRU_C = 8.0
CONV_WIDTH = 4
RWKV_HEAD = 64
DECAY_SCALE = 0.6065306597126334
LANES = 128
SUBLANES = 8
MXU_WIDTH = 256
VMEM_LIMIT_BYTES = 60 * 1024 * 1024
RWKV_CHUNK = 64
RWKV_ROWS = 128
RWKV_PAIRS = 16
LRU_ROWS = 256
ML_CHUNK_ROWS = 256
NORM_ROWS = 512
ROW_TILE = 2048
COL_TILE = 512
K_TILE_OUT = 2048
DOWN_COL_TILE = 1024
K_TILE_DOWN = 1024


def _cparams(semantics):
    return pltpu.CompilerParams(dimension_semantics=semantics, vmem_limit_bytes=VMEM_LIMIT_BYTES)


def _softplus(x):
    return jnp.maximum(x, 0.0) + jnp.log1p(jnp.exp(-jnp.abs(x)))


def _sigmoid(x):
    return 1.0 / (1.0 + jnp.exp(-x))


_NN = (((1,), (0,)), ((), ()))
_NT = (((1,), (1,)), ((), ()))
_TN = (((0,), (0,)), ((), ()))
_BNN = (((2,), (1,)), ((0,), (0,)))
_BNT = (((2,), (2,)), ((0,), (0,)))
_BTN = (((1,), (1,)), ((0,), (0,)))


def _dot(a, b, dims=_NN, precision=None):
    return lax.dot_general(a, b, dims, precision=precision, preferred_element_type=F32)


def _bdot(a, b, dims=_NN):
    return lax.dot_general(a.astype(BF16), b.astype(BF16), dims, preferred_element_type=F32)


def _norm_cast_kernel(x_ref, g_ref, o_ref):
    x = x_ref[...]
    ms = jnp.mean(x * x, axis=-1, keepdims=True)
    o_ref[...] = (x * lax.rsqrt(ms + NORM_EPS) * g_ref[...]).astype(o_ref.dtype)


def _norm_cast(x, g, out_dtype):
    t, d = x.shape
    tm = min(NORM_ROWS, t)
    return pl.pallas_call(
        _norm_cast_kernel,
        grid=(t // tm,),
        in_specs=[pl.BlockSpec((tm, d), lambda i: (i, 0)), pl.BlockSpec((1, d), lambda i: (0, 0))],
        out_specs=pl.BlockSpec((tm, d), lambda i: (i, 0)),
        out_shape=jax.ShapeDtypeStruct((t, d), out_dtype),
        compiler_params=_cparams(("parallel",)),
        name="rmsnorm",
    )(x, g.reshape(1, d))


def _matmul_kernel(*refs, n_lhs, nk, has_res, relu2, w_t):
    a_refs, w_ref = refs[:n_lhs], refs[n_lhs]
    r_ref = refs[n_lhs + 1] if has_res else None
    o_ref = refs[n_lhs + 1 + has_res]
    acc_ref = refs[-1] if (not has_res and nk > 1) else None
    kk = pl.program_id(2)

    def finish(acc):
        if relu2:
            acc = jnp.maximum(acc, 0.0)
            acc = acc * acc
        return acc.astype(o_ref.dtype)

    def accumulate(a_ref):
        def part():
            return lax.dot_general(a_ref[...], w_ref[...].astype(BF16), _NT if w_t else _NN,
                                   preferred_element_type=F32)

        if has_res:
            @pl.when(kk == 0)
            def _():
                o_ref[...] = r_ref[...] + part()

            @pl.when(kk > 0)
            def _():
                o_ref[...] += part()
        elif nk == 1:
            o_ref[...] = finish(part())
        else:
            @pl.when(kk == 0)
            def _():
                acc_ref[...] = part()

            @pl.when((kk > 0) & (kk < nk - 1))
            def _():
                acc_ref[...] += part()

            @pl.when(kk == nk - 1)
            def _():
                o_ref[...] = finish(acc_ref[...] + part())

    if n_lhs == 1:
        accumulate(a_refs[0])
    else:
        per = nk // n_lhs
        for idx, a_ref in enumerate(a_refs):
            pl.when((kk >= idx * per) & (kk < (idx + 1) * per))(functools.partial(accumulate, a_ref))


def _matmul(lhs, w, layer, n, *, tk, tn, tm=ROW_TILE, out_dtype=F32, res=None, relu2=False, w_t=False,
            lhs_double_buffer=False):
    t = lhs[0].shape[0]
    k = w.shape[2 if w_t else 1]
    tm = min(tm, t)
    nk = k // tk
    assert t % tm == 0 and k % tk == 0 and n % tn == 0
    per = nk // len(lhs)
    assert all(a.shape[1] == per * tk for a in lhs) and per * len(lhs) == nk
    once = dict(pipeline_mode=pl.Buffered(1)) if (per == 1 and not lhs_double_buffer) else {}

    def lhs_spec(idx):
        return pl.BlockSpec((tm, tk), lambda i, j, kk: (i, jnp.clip(kk - idx * per, 0, per - 1)), **once)

    a_specs = [lhs_spec(idx) for idx in range(len(lhs))]
    has_res = res is not None
    assert not (has_res and (relu2 or out_dtype != F32))
    tile = pl.BlockSpec((tm, tn), lambda i, j, kk: (i, j))
    scratch = [pltpu.VMEM((tm, tn), F32)] if (not has_res and nk > 1) else []
    if w_t:
        w_spec = pl.BlockSpec((None, tn, tk), lambda i, j, kk: (layer, j, kk))
    else:
        w_spec = pl.BlockSpec((None, tk, tn), lambda i, j, kk: (layer, kk, j))
    return pl.pallas_call(
        functools.partial(_matmul_kernel, n_lhs=len(lhs), nk=nk, has_res=has_res, relu2=relu2, w_t=w_t),
        grid=(t // tm, n // tn, nk),
        in_specs=a_specs + [w_spec] + ([tile] if has_res else []),
        out_specs=tile,
        out_shape=jax.ShapeDtypeStruct((t, n), out_dtype),
        scratch_shapes=scratch,
        compiler_params=_cparams(("parallel", "parallel", "arbitrary")),
        name="matmul",
    )(*lhs, w, *([res] if has_res else []))


def _lru_kernel(u_ref, gate_ref, cw_ref, cb_ref, wa_ref, ba_ref, wx_ref, bx_ref, lam_ref,
                o_ref, xbuf, hcar, *, tc):
    @pl.when(pl.program_id(2) == 0)
    def _():
        xbuf[0:SUBLANES, :] = jnp.zeros((SUBLANES, xbuf.shape[1]), F32)
        hcar[...] = jnp.zeros(hcar.shape, F32)

    u = u_ref[...].astype(F32)
    xbuf[SUBLANES:, :] = u
    cw = cw_ref[...]
    conv = cb_ref[...] + cw[CONV_WIDTH - 1:CONV_WIDTH] * u
    for j in range(CONV_WIDTH - 1):
        back = CONV_WIDTH - 1 - j
        conv = conv + cw[j:j + 1] * xbuf[SUBLANES - back:SUBLANES - back + tc, :]
    xbuf[0:SUBLANES, :] = u[tc - SUBLANES:, :]

    cbf = conv.astype(BF16)
    r = _sigmoid(jnp.dot(cbf, wa_ref[...], preferred_element_type=F32) + ba_ref[...])
    i = _sigmoid(jnp.dot(cbf, wx_ref[...], preferred_element_type=F32) + bx_ref[...])
    log_a = (-LRU_C) * r * _softplus(-lam_ref[...])
    a = jnp.exp(log_a)
    b = jnp.sqrt(-jnp.tanh(log_a) * (a * a + 1.0)) * (i * conv)

    n_tiles = tc // SUBLANES
    a = a.reshape(n_tiles, SUBLANES, a.shape[1])
    b = b.reshape(n_tiles, SUBLANES, b.shape[1])
    sub = lax.broadcasted_iota(jnp.int32, a.shape, 1)
    d = 1
    while d < SUBLANES:
        a_sh = pltpu.roll(a, d, 1)
        b_sh = pltpu.roll(b, d, 1)
        keep = sub >= d
        b = jnp.where(keep, a * b_sh + b, b)
        a = jnp.where(keep, a * a_sh, a)
        d *= 2
    carry = hcar[...]
    tiles = []
    for i in range(n_tiles):
        tiles.append(a[i] * carry + b[i])
        carry = tiles[-1][SUBLANES - 1:SUBLANES, :]
    h = jnp.concatenate(tiles, axis=0)
    hcar[...] = carry
    o_ref[...] = (h * jax.nn.gelu(gate_ref[...].astype(F32))).astype(o_ref.dtype)


def _lru_branch(z, conv_w, conv_b, w_a, b_a, w_x, b_x, lam, *, tc):
    bsz, seq, _ = z.shape
    heads, blk, _ = w_a.shape
    width = heads * blk
    assert blk % LANES == 0 and seq % tc == 0
    row = lambda v: v.reshape(1, width)
    vec_spec = pl.BlockSpec((1, blk), lambda b, h, t: (0, h))
    mat_spec = pl.BlockSpec((None, blk, blk), lambda b, h, t: (h, 0, 0))
    return pl.pallas_call(
        functools.partial(_lru_kernel, tc=tc),
        grid=(bsz, heads, seq // tc),
        in_specs=[
            pl.BlockSpec((None, tc, blk), lambda b, h, t: (b, t, h)),
            pl.BlockSpec((None, tc, blk), lambda b, h, t: (b, t, heads + h)),
            pl.BlockSpec((CONV_WIDTH, blk), lambda b, h, t: (0, h)),
            vec_spec, mat_spec, vec_spec, mat_spec, vec_spec, vec_spec,
        ],
        out_specs=pl.BlockSpec((None, tc, blk), lambda b, h, t: (b, t, h)),
        out_shape=jax.ShapeDtypeStruct((bsz, seq, width), BF16),
        scratch_shapes=[pltpu.VMEM((tc + SUBLANES, blk), F32), pltpu.VMEM((1, blk), F32)],
        compiler_params=_cparams(("parallel", "parallel", "arbitrary")),
        name="rglru",
    )(z, z, conv_w, row(conv_b), w_a.astype(BF16), row(b_a), w_x.astype(BF16), row(b_x), row(lam))


def _rwkv_kernel(r_ref, k_ref, v_ref, lora_ref, mu_r, mu_k, mu_v, mu_l, w0_ref, w2_ref, a0_ref, a2_ref,
                 g2_ref, kk_ref, ka_ref, rk_ref, lnw_ref, lnb_ref, o_ref,
                 rbuf, kbuf, vbuf, lbuf, state, *, tc, pp):
    L = RWKV_CHUNK
    P2 = 2 * L
    nc = tc // L
    assert P2 == LANES and RWKV_HEAD == L

    @pl.when(pl.program_id(2) == 0)
    def _():
        for buf in (rbuf, kbuf, vbuf, lbuf):
            buf[0:SUBLANES, :] = jnp.zeros((SUBLANES, buf.shape[1]), F32)
        state[...] = jnp.zeros(state.shape, F32)

    def token_shift(ref, buf, mu_ref):
        cur = ref[...].astype(F32)
        buf[SUBLANES:, :] = cur
        prev = buf[SUBLANES - 1:SUBLANES - 1 + tc, :]
        buf[0:SUBLANES, :] = cur[tc - SUBLANES:, :]
        return cur + (prev - cur) * mu_ref[...]

    r = token_shift(r_ref, rbuf, mu_r)
    k = token_shift(k_ref, kbuf, mu_k)
    v = token_shift(v_ref, vbuf, mu_v)
    lora = token_shift(lora_ref, lbuf, mu_l)

    dw = _dot(jnp.tanh(lora).astype(BF16), w2_ref[...])
    logw = -DECAY_SCALE * _sigmoid(w0_ref[...] + dw)
    a = _sigmoid(a0_ref[...] + _dot(lora.astype(BF16), a2_ref[...]))
    g = _dot(_sigmoid(lora).astype(BF16), g2_ref[...])

    ri = lax.broadcasted_iota(jnp.int32, (P2, P2), 0)
    ci = lax.broadcasted_iota(jnp.int32, (P2, P2), 1)
    blk = jnp.where(ri >= L, 1, 0) == jnp.where(ci >= L, 1, 0)
    head_ones = jnp.where(blk, 1.0, 0.0).astype(BF16)

    def to_rows(x):
        return jnp.concatenate([x[:, j * LANES:(j + 1) * LANES] for j in range(pp)], axis=0)

    def from_rows(x):
        return jnp.concatenate([x[j * tc:(j + 1) * tc, :] for j in range(pp)], axis=1)

    def head_sum(x):
        return from_rows(_dot(to_rows(x).astype(BF16), head_ones))

    kk = k * kk_ref[...]
    kk = kk * lax.rsqrt(jnp.maximum(head_sum(kk * kk), 1e-24))
    kmod = k * (1.0 + (a - 1.0) * ka_ref[...])
    kka = kk * a
    bonus = head_sum(r * kmod * rk_ref[...]) * v

    rt = lax.broadcasted_iota(jnp.int32, (tc, tc), 0)
    ct = lax.broadcasted_iota(jnp.int32, (tc, tc), 1)
    shift = L.bit_length() - 1
    same_chunk = lax.shift_right_logical(rt, shift) == lax.shift_right_logical(ct, shift)
    cum = _dot(jnp.where(same_chunk & (ct <= rt), 1.0, 0.0).astype(F32), logw, precision=HIGHEST)
    p_inv = jnp.exp(-cum)
    r_d = r * jnp.exp(cum)
    kk_d = kk * jnp.exp(cum - logw)
    k_d = kmod * p_inv
    a_d = kka * p_inv
    p_last = [jnp.exp(cum[(c + 1) * L - 1:(c + 1) * L, :]) for c in range(nc)]
    p_tail = p_inv * jnp.concatenate([jnp.broadcast_to(p, (L, cum.shape[1])) for p in p_last], axis=0)
    k_t = kmod * p_tail
    a_t = kka * p_tail

    strict = (blk & (ci < ri))[None]
    incl = (blk & (ci <= ri))[None]
    eye = jnp.where(ri == ci, 1.0, 0.0).astype(F32)[None]
    lane_row = lax.broadcasted_iota(jnp.int32, (1, LANES), 1)
    m0 = jnp.where(lane_row < RWKV_HEAD, 1.0, 0.0).astype(F32)
    m1 = 1.0 - m0

    def batch(x, zero_other_head=True):
        out = []
        for c in range(nc):
            for j in range(pp):
                xs = x[c * L:(c + 1) * L, j * LANES:(j + 1) * LANES]
                out.append(jnp.concatenate([xs * m0, xs * m1] if zero_other_head else [xs, xs], axis=0))
        return jnp.stack(out)

    r_b, kk_b, v_b, kt_b, at_b = (batch(x) for x in (r_d, kk_d, v, k_t, a_t))
    k_b, a_b = batch(k_d, False), batch(a_d, False)
    kkr_b = jnp.concatenate([kk_b, r_b], axis=1)

    gram = _bdot(kkr_b, jnp.concatenate([k_b, a_b], axis=1), _BNT)
    a_kk_k = jnp.where(strict, gram[:, 0:P2, 0:P2], 0.0)
    n_mat = jnp.where(strict, gram[:, 0:P2, P2:], 0.0)
    a_r_k = jnp.where(incl, gram[:, P2:, 0:P2], 0.0)
    a_r_a = jnp.where(incl, gram[:, P2:, P2:], 0.0)

    inv = eye - n_mat
    pw = _bdot(n_mat, n_mat, _BNN)
    steps = 2
    while 2 * steps < L:
        both = _bdot(jnp.concatenate([inv, pw], axis=1), pw, _BNN)
        inv = inv + both[:, 0:P2]
        pw = both[:, P2:]
        steps *= 2
    inv = inv + _bdot(inv, pw, _BNN)
    av = _bdot(jnp.concatenate([a_kk_k, a_r_k], axis=1), v_b, _BNN)

    s = state[...]
    y_chunks = []
    for c in range(nc):
        sl = slice(c * pp, (c + 1) * pp)
        from_state = _bdot(kkr_b[sl], s, _BNT) + av[sl]
        sa = _bdot(inv[sl], from_state[:, 0:P2], _BNN)
        y_s = from_state[:, P2:] - _bdot(a_r_a[sl], sa, _BNN)
        y_c = y_s[:, 0:L, :] + y_s[:, L:, :]
        y_chunks.append(jnp.concatenate([y_c[j] for j in range(pp)], axis=1))
        decay = jnp.stack([p_last[c][:, j * LANES:(j + 1) * LANES] for j in range(pp)])
        s = s * decay + _bdot(jnp.concatenate([v_b[sl], sa], axis=1),
                              jnp.concatenate([kt_b[sl], -at_b[sl]], axis=1), _BTN)
    state[...] = s

    y = jnp.concatenate(y_chunks, axis=0)
    inv_n = 1.0 / RWKV_HEAD
    mean = head_sum(y) * inv_n
    yc = y - mean
    var = head_sum(yc * yc) * inv_n
    yn = yc * lax.rsqrt(var + GN_EPS) * lnw_ref[...] + lnb_ref[...]
    o_ref[...] = ((yn + bonus) * g).astype(o_ref.dtype)


def _rwkv_branch(z, col0, z_lora, mu, w0, w2, a0, a2, g2, k_k, k_a, r_k, ln_w, ln_b, *, tc, pp):
    bsz, seq, _ = z.shape
    width = w0.shape[0]
    lora_w = z_lora.shape[2]
    bw = pp * LANES
    d_lora, a_lora, g_lora = w2.shape[0], a2.shape[0], g2.shape[0]
    assert width % bw == 0 and col0 % bw == 0 and lora_w % LANES == 0
    assert lora_w >= d_lora + a_lora + g_lora
    assert seq % tc == 0 and tc % RWKV_CHUNK == 0 and tc % SUBLANES == 0
    groups = width // bw
    cb = col0 // bw

    def pad_rows(wm, start):
        return jnp.zeros((lora_w, width), F32).at[start:start + wm.shape[0]].set(wm).astype(BF16)

    w2p = pad_rows(w2, 0)
    a2p = pad_rows(a2, d_lora)
    g2p = pad_rows(g2, d_lora + a_lora)
    mu_pad = jnp.zeros((1, 3 * width + lora_w), F32).at[0, :mu.shape[0]].set(mu)
    mu_r, mu_k, mu_v = (mu_pad[:, i * width:(i + 1) * width] for i in range(3))
    mu_l = mu_pad[:, 3 * width:]
    row = lambda x: x.reshape(1, width)

    act = lambda off: pl.BlockSpec((None, tc, bw), lambda b, p, t: (b, t, cb + off * groups + p))
    vec = pl.BlockSpec((1, bw), lambda b, p, t: (0, p))
    lmat = pl.BlockSpec((lora_w, bw), lambda b, p, t: (0, p))
    return pl.pallas_call(
        functools.partial(_rwkv_kernel, tc=tc, pp=pp),
        grid=(bsz, groups, seq // tc),
        in_specs=[
            act(0), act(1), act(2),
            pl.BlockSpec((None, tc, lora_w), lambda b, p, t: (b, t, 0)),
            vec, vec, vec,
            pl.BlockSpec((1, lora_w), lambda b, p, t: (0, 0)),
            vec, lmat, vec, lmat, lmat, vec, vec, vec, vec, vec,
        ],
        out_specs=pl.BlockSpec((None, tc, bw), lambda b, p, t: (b, t, p)),
        out_shape=jax.ShapeDtypeStruct((bsz, seq, width), BF16),
        scratch_shapes=[
            pltpu.VMEM((tc + SUBLANES, bw), F32), pltpu.VMEM((tc + SUBLANES, bw), F32),
            pltpu.VMEM((tc + SUBLANES, bw), F32), pltpu.VMEM((tc + SUBLANES, lora_w), F32),
            pltpu.VMEM((pp, LANES, LANES), F32),
        ],
        compiler_params=_cparams(("parallel", "parallel", "arbitrary")),
        name="rwkv7",
    )(z, z, z, z_lora, mu_r, mu_k, mu_v, mu_l, row(w0), w2p, row(a0), a2p, g2p, row(k_k), row(k_a),
      row(r_k), row(ln_w), row(ln_b))


def _mlstm_kernel(q_ref, k_ref, v_ref, zo_ref, gcol_ref, grow_ref, bcol_ref, brow_ref, nw_ref, o_ref,
                  c_st, n_st, m_st, *, heads, dqk, dv):
    L = q_ref.shape[0]

    @pl.when(pl.program_id(1) == 0)
    def _():
        c_st[...] = jnp.zeros(c_st.shape, F32)
        n_st[...] = jnp.zeros(n_st.shape, F32)
        m_st[...] = jnp.zeros(m_st.shape, F32)

    ri = lax.broadcasted_iota(jnp.int32, (L, L), 0)
    ci = lax.broadcasted_iota(jnp.int32, (L, L), 1)
    causal = ci <= ri
    tril_ones = jnp.where(causal, 1.0, 0.0).astype(F32)
    triu_ones = jnp.where(ci >= ri, 1.0, 0.0).astype(F32)

    gc = gcol_ref[...] + bcol_ref[...]
    gr = grow_ref[...] + brow_ref[...]
    b_c = _dot(tril_ones, -_softplus(-gc), precision=HIGHEST)
    b_r = _dot(-_softplus(-gr), triu_ones, precision=HIGHEST)

    per_head = lambda f: jnp.stack([f(h) for h in range(heads)])
    q = per_head(lambda h: q_ref[:, h * dqk:(h + 1) * dqk])
    q = (q.astype(F32) * dqk ** -0.5).astype(BF16)
    k = per_head(lambda h: k_ref[:, h * dqk:(h + 1) * dqk])
    v = per_head(lambda h: v_ref[:, h * dv:(h + 1) * dv])
    bc = per_head(lambda h: b_c[:, heads + h:heads + h + 1])
    lic = per_head(lambda h: gc[:, h:h + 1])
    br = per_head(lambda h: b_r[heads + h:heads + h + 1, :])
    lir = per_head(lambda h: gr[h:h + 1, :])
    m_prev = m_st[...][:, 0:1, 0:1]
    c_mat = c_st[...]
    n_vec = n_st[...]

    log_d = jnp.where(causal[None], bc - br + lir, -jnp.inf)
    log_inter = bc + m_prev
    m_t = jnp.maximum(jnp.max(log_d, axis=-1, keepdims=True), log_inter)
    p = _bdot(q, k, _BNT) * jnp.exp(log_d - m_t)
    inter = jnp.exp(log_inter - m_t)
    num = _bdot(p, v, _BNN) + inter * _bdot(q, c_mat, _BNN)
    den = (jnp.sum(p, axis=-1, keepdims=True)
           + inter * jnp.sum(q.astype(F32) * n_vec, axis=-1, keepdims=True))
    hs = num / jnp.maximum(jnp.abs(den), jnp.exp(-m_t))

    b_last = bc[:, L - 1:L, :]
    log_g = b_last - bc + lic
    m_new = jnp.maximum(b_last + m_prev, jnp.max(log_g, axis=1, keepdims=True))
    carry = jnp.exp(b_last + m_prev - m_new)
    wk = k.astype(F32) * jnp.exp(log_g - m_new)
    c_st[...] = carry * c_mat + _bdot(wk, v, _BTN)
    n_st[...] = carry * n_vec + jnp.sum(wk, axis=1, keepdims=True)
    m_st[...] = jnp.broadcast_to(m_new, m_st.shape)

    hs = hs * lax.rsqrt(jnp.mean(hs * hs, axis=-1, keepdims=True) + NORM_EPS)
    for h in range(heads):
        sl = slice(h * dv, (h + 1) * dv)
        o_ref[:, sl] = (hs[h] * nw_ref[:, sl] * _sigmoid(zo_ref[:, sl].astype(F32))).astype(o_ref.dtype)


def _mlstm_branch(z, gates, b_i, b_f, norm_w, *, heads, qk, vdim):
    bsz, seq, _ = z.shape
    L = ML_CHUNK_ROWS
    dqk, dv = qk // heads, vdim // heads
    assert seq % L == 0 and vdim % qk == 0 and qk % LANES == 0 and (2 * qk) % vdim == 0
    vblk = 2 * qk // vdim
    bias = jnp.concatenate([b_i, b_f]).astype(F32)
    bias_col = jnp.zeros((1, LANES), F32).at[0, :2 * heads].set(bias)
    gates_row = jnp.swapaxes(gates[:, :, :2 * heads], 1, 2)
    return pl.pallas_call(
        functools.partial(_mlstm_kernel, heads=heads, dqk=dqk, dv=dv),
        grid=(bsz, seq // L),
        in_specs=[
            pl.BlockSpec((None, L, qk), lambda b, c: (b, c, 0)),
            pl.BlockSpec((None, L, qk), lambda b, c: (b, c, 1)),
            pl.BlockSpec((None, L, vdim), lambda b, c: (b, c, vblk)),
            pl.BlockSpec((None, L, vdim), lambda b, c: (b, c, vblk + 1)),
            pl.BlockSpec((None, L, LANES), lambda b, c: (b, c, 0)),
            pl.BlockSpec((None, 2 * heads, L), lambda b, c: (b, 0, c)),
            pl.BlockSpec((1, LANES), lambda b, c: (0, 0)),
            pl.BlockSpec((2 * heads, 1), lambda b, c: (0, 0)),
            pl.BlockSpec((1, vdim), lambda b, c: (0, 0)),
        ],
        out_specs=pl.BlockSpec((None, L, vdim), lambda b, c: (b, c, 0)),
        out_shape=jax.ShapeDtypeStruct((bsz, seq, vdim), BF16),
        scratch_shapes=[
            pltpu.VMEM((heads, dqk, dv), F32),
            pltpu.VMEM((heads, 1, dqk), F32),
            pltpu.VMEM((heads, SUBLANES, LANES), F32),
        ],
        compiler_params=_cparams(("parallel", "arbitrary")),
        name="mlstm",
    )(z, z, z, z, gates, gates_row, bias_col, bias.reshape(-1, 1), norm_w.reshape(1, vdim))


def _col_tile(n, cap):
    for step in (MXU_WIDTH, LANES):
        fits = [c for c in range(step, cap + 1, step) if n % c == 0]
        if fits:
            return fits[-1]
    raise ValueError(f"no lane-aligned column tile divides {n}")


def _tail_rows(wt, start):
    tail = wt[:, start:, :]
    pad = (-tail.shape[1]) % LANES
    return jnp.pad(tail, ((0, 0), (0, pad), (0, 0)))


def kernel(x, norm_mix, norm_mlp, norm_final, mlp_up, mlp_down, hy_in, lru_conv_w, lru_conv_b, lru_wa, lru_ba, lru_wx, lru_bx, lru_lam, rwkv_mu, rwkv_w0, rwkv_w2, rwkv_a0, rwkv_a2, rwkv_g2, rwkv_kk, rwkv_ka, rwkv_rk, rwkv_ln_w, rwkv_ln_b, hy_out, ml_in, ml_bi, ml_bf, ml_norm, ml_out):
    bsz, seq, d = x.shape
    t = bsz * seq
    depth = norm_mix.shape[0]
    xf = x.reshape(t, d)

    lru_w = lru_lam.shape[1]
    rw_w = rwkv_w0.shape[1]
    hy_main = 2 * lru_w + 3 * rw_w
    heads = ml_bi.shape[1]
    vdim = ml_norm.shape[1]
    qk = (ml_in.shape[2] - 2 * vdim - 2 * heads) // 2
    ml_main = 2 * qk + 2 * vdim

    hy_in_t, ml_in_t = jnp.swapaxes(hy_in, 1, 2), jnp.swapaxes(ml_in, 1, 2)
    hy_lora_t, ml_gate_t = _tail_rows(hy_in_t, hy_main), _tail_rows(ml_in_t, ml_main)
    tn = lambda n: _col_tile(n, COL_TILE)
    tk_out = min(K_TILE_OUT, lru_w, vdim)

    for layer in range(depth):
        hn = _norm_cast(xf, norm_mix[layer], BF16)
        if layer % 2 == 0:
            e = layer // 2
            z = _matmul([hn], hy_in_t, e, hy_main, tk=d, tn=tn(hy_main), out_dtype=BF16, w_t=True)
            z = z.reshape(bsz, seq, hy_main)
            n_lora = hy_lora_t.shape[1]
            z_lora = _matmul([hn], hy_lora_t, e, n_lora, tk=d, tn=tn(n_lora), w_t=True,
                             tm=ROW_TILE // 2, lhs_double_buffer=True)
            z_lora = z_lora.reshape(bsz, seq, n_lora)
            ya = _lru_branch(z, lru_conv_w[e], lru_conv_b[e], lru_wa[e], lru_ba[e], lru_wx[e], lru_bx[e],
                             lru_lam[e], tc=min(LRU_ROWS, seq))
            yb = _rwkv_branch(z, 2 * lru_w, z_lora, rwkv_mu[e], rwkv_w0[e], rwkv_w2[e], rwkv_a0[e], rwkv_a2[e],
                              rwkv_g2[e], rwkv_kk[e], rwkv_ka[e], rwkv_rk[e], rwkv_ln_w[e], rwkv_ln_b[e],
                              tc=min(RWKV_ROWS, seq), pp=min(RWKV_PAIRS, rw_w // LANES))
            lhs = [ya.reshape(t, lru_w), yb.reshape(t, rw_w)]
            assert lru_w == rw_w
            xf = _matmul(lhs, hy_out, e, d, tk=tk_out, tn=tn(d), res=xf)
        else:
            o = layer // 2
            z = _matmul([hn], ml_in_t, o, ml_main, tk=d, tn=tn(ml_main), out_dtype=BF16, w_t=True)
            z = z.reshape(bsz, seq, ml_main)
            n_gate = ml_gate_t.shape[1]
            gates = _matmul([hn], ml_gate_t, o, n_gate, tk=d, tn=tn(n_gate), w_t=True,
                            tm=ROW_TILE // 2, lhs_double_buffer=True)
            gates = gates.reshape(bsz, seq, n_gate)
            hs = _mlstm_branch(z, gates, ml_bi[o], ml_bf[o], ml_norm[o], heads=heads, qk=qk, vdim=vdim)
            xf = _matmul([hs.reshape(t, vdim)], ml_out, o, d, tk=tk_out, tn=tn(d), res=xf)
        hn = _norm_cast(xf, norm_mlp[layer], BF16)
        u = _matmul([hn], mlp_up, layer, mlp_up.shape[2], tk=d, tn=tn(mlp_up.shape[2]), out_dtype=BF16,
                    relu2=True)
        xf = _matmul([u], mlp_down, layer, d, tk=min(K_TILE_DOWN, mlp_down.shape[1]),
                     tn=_col_tile(d, DOWN_COL_TILE), res=xf)
    return _norm_cast(xf, norm_final, F32).reshape(bsz, seq, d)
```

```python
import functools

import jax
import jax.numpy as jnp
from jax import lax
from jax.experimental import pallas as pl
from jax.experimental.pallas import tpu as pltpu

F32 = jnp.float32
BF16 = jnp.bfloat16
HIGHEST = lax.Precision.HIGHEST

NORM_EPS = 1e-6
GN_EPS = 64e-5
LRU_C = 8.0
CONV_WIDTH = 4
RWKV_HEAD = 64
DECAY_SCALE = 0.6065306597126334
LANES = 128
SUBLANES = 8
MXU_WIDTH = 256
VMEM_LIMIT_BYTES = 60 * 1024 * 1024
RWKV_CHUNK = 64
RWKV_ROWS = 128
RWKV_PAIRS = 16
LRU_ROWS = 1024
ML_CHUNK_ROWS = 256
NORM_ROWS = 512
ROW_TILE = 2048
COL_TILE = 512
K_TILE_OUT = 2048
DOWN_COL_TILE = 1024
K_TILE_DOWN = 1024


def _cparams(semantics):
    return pltpu.CompilerParams(dimension_semantics=semantics, vmem_limit_bytes=VMEM_LIMIT_BYTES)


def _softplus(x):
    return jnp.maximum(x, 0.0) + jnp.log1p(jnp.exp(-jnp.abs(x)))


def _sigmoid(x):
    return 1.0 / (1.0 + jnp.exp(-x))


_NN = (((1,), (0,)), ((), ()))
_NT = (((1,), (1,)), ((), ()))
_TN = (((0,), (0,)), ((), ()))
_BNN = (((2,), (1,)), ((0,), (0,)))
_BNT = (((2,), (2,)), ((0,), (0,)))
_BTN = (((1,), (1,)), ((0,), (0,)))


def _dot(a, b, dims=_NN, precision=None):
    return lax.dot_general(a, b, dims, precision=precision, preferred_element_type=F32)


def _bdot(a, b, dims=_NN):
    return lax.dot_general(a.astype(BF16), b.astype(BF16), dims, preferred_element_type=F32)


def _norm_cast_kernel(x_ref, g_ref, o_ref):
    x = x_ref[...]
    ms = jnp.mean(x * x, axis=-1, keepdims=True)
    o_ref[...] = (x * lax.rsqrt(ms + NORM_EPS) * g_ref[...]).astype(o_ref.dtype)


def _norm_cast(x, g, out_dtype):
    t, d = x.shape
    tm = min(NORM_ROWS, t)
    return pl.pallas_call(
        _norm_cast_kernel,
        grid=(t // tm,),
        in_specs=[pl.BlockSpec((tm, d), lambda i: (i, 0)), pl.BlockSpec((1, d), lambda i: (0, 0))],
        out_specs=pl.BlockSpec((tm, d), lambda i: (i, 0)),
        out_shape=jax.ShapeDtypeStruct((t, d), out_dtype),
        compiler_params=_cparams(("parallel",)),
        name="rmsnorm",
    )(x, g.reshape(1, d))


def _matmul_kernel(*refs, n_lhs, nk, has_res, relu2, w_t):
    a_refs, w_ref = refs[:n_lhs], refs[n_lhs]
    r_ref = refs[n_lhs + 1] if has_res else None
    o_ref = refs[n_lhs + 1 + has_res]
    acc_ref = refs[-1] if (not has_res and nk > 1) else None
    kk = pl.program_id(2)

    def finish(acc):
        if relu2:
            acc = jnp.maximum(acc, 0.0)
            acc = acc * acc
        return acc.astype(o_ref.dtype)

    def accumulate(a_ref):
        def part():
            return lax.dot_general(a_ref[...], w_ref[...].astype(BF16), _NT if w_t else _NN,
                                   preferred_element_type=F32)

        if has_res:
            @pl.when(kk == 0)
            def _():
                o_ref[...] = r_ref[...] + part()

            @pl.when(kk > 0)
            def _():
                o_ref[...] += part()
        elif nk == 1:
            o_ref[...] = finish(part())
        else:
            @pl.when(kk == 0)
            def _():
                acc_ref[...] = part()

            @pl.when((kk > 0) & (kk < nk - 1))
            def _():
                acc_ref[...] += part()

            @pl.when(kk == nk - 1)
            def _():
                o_ref[...] = finish(acc_ref[...] + part())

    if n_lhs == 1:
        accumulate(a_refs[0])
    else:
        per = nk // n_lhs
        for idx, a_ref in enumerate(a_refs):
            pl.when((kk >= idx * per) & (kk < (idx + 1) * per))(functools.partial(accumulate, a_ref))


def _matmul(lhs, w, layer, n, *, tk, tn, tm=ROW_TILE, out_dtype=F32, res=None, relu2=False, w_t=False,
            lhs_double_buffer=False):
    t = lhs[0].shape[0]
    k = w.shape[2 if w_t else 1]
    tm = min(tm, t)
    nk = k // tk
    assert t % tm == 0 and k % tk == 0 and n % tn == 0
    per = nk // len(lhs)
    assert all(a.shape[1] == per * tk for a in lhs) and per * len(lhs) == nk
    once = dict(pipeline_mode=pl.Buffered(1)) if (per == 1 and not lhs_double_buffer) else {}

    def lhs_spec(idx):
        return pl.BlockSpec((tm, tk), lambda i, j, kk: (i, jnp.clip(kk - idx * per, 0, per - 1)), **once)

    a_specs = [lhs_spec(idx) for idx in range(len(lhs))]
    has_res = res is not None
    assert not (has_res and (relu2 or out_dtype != F32))
    tile = pl.BlockSpec((tm, tn), lambda i, j, kk: (i, j))
    scratch = [pltpu.VMEM((tm, tn), F32)] if (not has_res and nk > 1) else []
    if w_t:
        w_spec = pl.BlockSpec((None, tn, tk), lambda i, j, kk: (layer, j, kk))
    else:
        w_spec = pl.BlockSpec((None, tk, tn), lambda i, j, kk: (layer, kk, j))
    return pl.pallas_call(
        functools.partial(_matmul_kernel, n_lhs=len(lhs), nk=nk, has_res=has_res, relu2=relu2, w_t=w_t),
        grid=(t // tm, n // tn, nk),
        in_specs=a_specs + [w_spec] + ([tile] if has_res else []),
        out_specs=tile,
        out_shape=jax.ShapeDtypeStruct((t, n), out_dtype),
        scratch_shapes=scratch,
        compiler_params=_cparams(("parallel", "parallel", "arbitrary")),
        name="matmul",
    )(*lhs, w, *([res] if has_res else []))


def _lru_kernel(u_ref, gate_ref, cw_ref, cb_ref, wa_ref, ba_ref, wx_ref, bx_ref, lam_ref,
                o_ref, xbuf, hcar, *, tc):
    @pl.when(pl.program_id(2) == 0)
    def _():
        xbuf[0:SUBLANES, :] = jnp.zeros((SUBLANES, xbuf.shape[1]), F32)
        hcar[...] = jnp.zeros(hcar.shape, F32)

    u = u_ref[...].astype(F32)
    xbuf[SUBLANES:, :] = u
    cw = cw_ref[...]
    conv = cb_ref[...] + cw[CONV_WIDTH - 1:CONV_WIDTH] * u
    for j in range(CONV_WIDTH - 1):
        back = CONV_WIDTH - 1 - j
        conv = conv + cw[j:j + 1] * xbuf[SUBLANES - back:SUBLANES - back + tc, :]
    xbuf[0:SUBLANES, :] = u[tc - SUBLANES:, :]

    cbf = conv.astype(BF16)
    r = _sigmoid(jnp.dot(cbf, wa_ref[...], preferred_element_type=F32) + ba_ref[...])
    i = _sigmoid(jnp.dot(cbf, wx_ref[...], preferred_element_type=F32) + bx_ref[...])
    log_a = (-LRU_C) * r * _softplus(-lam_ref[...])
    a = jnp.exp(log_a)
    b = jnp.sqrt(-jnp.tanh(log_a) * (a * a + 1.0)) * (i * conv)

    n_tiles = tc // SUBLANES
    a = a.reshape(n_tiles, SUBLANES, a.shape[1])
    b = b.reshape(n_tiles, SUBLANES, b.shape[1])
    sub = lax.broadcasted_iota(jnp.int32, a.shape, 1)
    d = 1
    while d < SUBLANES:
        a_sh = pltpu.roll(a, d, 1)
        b_sh = pltpu.roll(b, d, 1)
        keep = sub >= d
        b = jnp.where(keep, a * b_sh + b, b)
        a = jnp.where(keep, a * a_sh, a)
        d *= 2
    carry = hcar[...]
    tiles = []
    for i in range(n_tiles):
        tiles.append(a[i] * carry + b[i])
        carry = tiles[-1][SUBLANES - 1:SUBLANES, :]
    h = jnp.concatenate(tiles, axis=0)
    hcar[...] = carry
    o_ref[...] = (h * jax.nn.gelu(gate_ref[...].astype(F32))).astype(o_ref.dtype)


def _lru_branch(z, conv_w, conv_b, w_a, b_a, w_x, b_x, lam, *, tc):
    bsz, seq, _ = z.shape
    heads, blk, _ = w_a.shape
    width = heads * blk
    assert blk % LANES == 0 and seq % tc == 0
    row = lambda v: v.reshape(1, width)
    vec_spec = pl.BlockSpec((1, blk), lambda b, h, t: (0, h))
    mat_spec = pl.BlockSpec((None, blk, blk), lambda b, h, t: (h, 0, 0))
    return pl.pallas_call(
        functools.partial(_lru_kernel, tc=tc),
        grid=(bsz, heads, seq // tc),
        in_specs=[
            pl.BlockSpec((None, tc, blk), lambda b, h, t: (b, t, h)),
            pl.BlockSpec((None, tc, blk), lambda b, h, t: (b, t, heads + h)),
            pl.BlockSpec((CONV_WIDTH, blk), lambda b, h, t: (0, h)),
            vec_spec, mat_spec, vec_spec, mat_spec, vec_spec, vec_spec,
        ],
        out_specs=pl.BlockSpec((None, tc, blk), lambda b, h, t: (b, t, h)),
        out_shape=jax.ShapeDtypeStruct((bsz, seq, width), BF16),
        scratch_shapes=[pltpu.VMEM((tc + SUBLANES, blk), F32), pltpu.VMEM((1, blk), F32)],
        compiler_params=_cparams(("parallel", "parallel", "arbitrary")),
        name="rglru",
    )(z, z, conv_w, row(conv_b), w_a.astype(BF16), row(b_a), w_x.astype(BF16), row(b_x), row(lam))


def _rwkv_kernel(r_ref, k_ref, v_ref, lora_ref, mu_r, mu_k, mu_v, mu_l, w0_ref, w2_ref, a0_ref, a2_ref,
                 g2_ref, kk_ref, ka_ref, rk_ref, lnw_ref, lnb_ref, o_ref,
                 rbuf, kbuf, vbuf, lbuf, state, *, tc, pp, segs):
    L = RWKV_CHUNK
    P2 = 2 * L
    nc = tc // L
    assert P2 == LANES and RWKV_HEAD == L

    @pl.when(pl.program_id(2) == 0)
    def _():
        for buf in (rbuf, kbuf, vbuf, lbuf):
            buf[0:SUBLANES, :] = jnp.zeros((SUBLANES, buf.shape[1]), F32)
        state[...] = jnp.zeros(state.shape, F32)

    def token_shift(ref, buf, mu_ref):
        cur = ref[...].astype(F32)
        buf[SUBLANES:, :] = cur
        prev = buf[SUBLANES - 1:SUBLANES - 1 + tc, :]
        buf[0:SUBLANES, :] = cur[tc - SUBLANES:, :]
        return cur + (prev - cur) * mu_ref[...]

    r = token_shift(r_ref, rbuf, mu_r)
    k = token_shift(k_ref, kbuf, mu_k)
    v = token_shift(v_ref, vbuf, mu_v)
    lora = token_shift(lora_ref, lbuf, mu_l)

    lora_w, lora_a, lora_g = (lora[:, lo:hi] for lo, hi in segs)
    dw = _dot(jnp.tanh(lora_w).astype(BF16), w2_ref[...])
    logw = -DECAY_SCALE * _sigmoid(w0_ref[...] + dw)
    a = _sigmoid(a0_ref[...] + _dot(lora_a.astype(BF16), a2_ref[...]))
    g = _dot(_sigmoid(lora_g).astype(BF16), g2_ref[...])

    ri = lax.broadcasted_iota(jnp.int32, (P2, P2), 0)
    ci = lax.broadcasted_iota(jnp.int32, (P2, P2), 1)
    blk = jnp.where(ri >= L, 1, 0) == jnp.where(ci >= L, 1, 0)
    head_ones = jnp.where(blk, 1.0, 0.0).astype(BF16)

    def to_rows(x):
        return jnp.concatenate([x[:, j * LANES:(j + 1) * LANES] for j in range(pp)], axis=0)

    def from_rows(x):
        return jnp.concatenate([x[j * tc:(j + 1) * tc, :] for j in range(pp)], axis=1)

    def head_sum(x):
        return from_rows(_dot(to_rows(x).astype(BF16), head_ones))

    kk = k * kk_ref[...]
    kk = kk * lax.rsqrt(jnp.maximum(head_sum(kk * kk), 1e-24))
    kmod = k * (1.0 + (a - 1.0) * ka_ref[...])
    kka = kk * a
    bonus = head_sum(r * kmod * rk_ref[...]) * v

    rt = lax.broadcasted_iota(jnp.int32, (tc, tc), 0)
    ct = lax.broadcasted_iota(jnp.int32, (tc, tc), 1)
    shift = L.bit_length() - 1
    same_chunk = lax.shift_right_logical(rt, shift) == lax.shift_right_logical(ct, shift)
    tril_chunk = jnp.where(same_chunk & (ct <= rt), 1.0, 0.0).astype(BF16)
    logw_hi = logw.astype(BF16)
    logw_lo = (logw - logw_hi.astype(F32)).astype(BF16)
    cum = _dot(tril_chunk, logw_hi) + _dot(tril_chunk, logw_lo)
    p_inv = jnp.exp(-cum)
    r_d = r * jnp.exp(cum)
    kk_d = kk * jnp.exp(cum - logw)
    k_d = kmod * p_inv
    a_d = kka * p_inv
    p_last = [jnp.exp(cum[(c + 1) * L - 1:(c + 1) * L, :]) for c in range(nc)]
    p_tail = p_inv * jnp.concatenate([jnp.broadcast_to(p, (L, cum.shape[1])) for p in p_last], axis=0)
    k_t = kmod * p_tail
    a_t = kka * p_tail

    strict = (blk & (ci < ri))[None]
    incl = (blk & (ci <= ri))[None]
    eye = jnp.where(ri == ci, 1.0, 0.0).astype(F32)[None]
    lane_row = lax.broadcasted_iota(jnp.int32, (1, LANES), 1)
    m0 = jnp.where(lane_row < RWKV_HEAD, 1.0, 0.0).astype(F32)
    m1 = 1.0 - m0

    def batch(x, zero_other_head=True):
        out = []
        for c in range(nc):
            for j in range(pp):
                xs = x[c * L:(c + 1) * L, j * LANES:(j + 1) * LANES]
                out.append(jnp.concatenate([xs * m0, xs * m1] if zero_other_head else [xs, xs], axis=0))
        return jnp.stack(out)

    r_b, kk_b, v_b, kt_b, at_b = (batch(x) for x in (r_d, kk_d, v, k_t, a_t))
    k_b, a_b = batch(k_d, False), batch(a_d, False)
    kkr_b = jnp.concatenate([kk_b, r_b], axis=1)

    gram = _bdot(kkr_b, jnp.concatenate([k_b, a_b], axis=1), _BNT)
    a_kk_k = jnp.where(strict, gram[:, 0:P2, 0:P2], 0.0)
    n_mat = jnp.where(strict, gram[:, 0:P2, P2:], 0.0)
    a_r_k = jnp.where(incl, gram[:, P2:, 0:P2], 0.0)
    a_r_a = jnp.where(incl, gram[:, P2:, P2:], 0.0)

    inv = eye - n_mat
    pw = _bdot(n_mat, n_mat, _BNN)
    steps = 2
    while 2 * steps < L:
        both = _bdot(jnp.concatenate([inv, pw], axis=1), pw, _BNN)
        inv = inv + both[:, 0:P2]
        pw = both[:, P2:]
        steps *= 2
    inv = inv + _bdot(inv, pw, _BNN)
    av = _bdot(jnp.concatenate([a_kk_k, a_r_k], axis=1), v_b, _BNN)

    s = state[...]
    y_chunks = []
    for c in range(nc):
        sl = slice(c * pp, (c + 1) * pp)
        from_state = _bdot(kkr_b[sl], s, _BNT) + av[sl]
        sa = _bdot(inv[sl], from_state[:, 0:P2], _BNN)
        y_s = from_state[:, P2:] - _bdot(a_r_a[sl], sa, _BNN)
        y_c = y_s[:, 0:L, :] + y_s[:, L:, :]
        y_chunks.append(jnp.concatenate([y_c[j] for j in range(pp)], axis=1))
        decay = jnp.stack([p_last[c][:, j * LANES:(j + 1) * LANES] for j in range(pp)])
        s = s * decay + _bdot(jnp.concatenate([v_b[sl], sa], axis=1),
                              jnp.concatenate([kt_b[sl], -at_b[sl]], axis=1), _BTN)
    state[...] = s

    y = jnp.concatenate(y_chunks, axis=0)
    inv_n = 1.0 / RWKV_HEAD
    mean = head_sum(y) * inv_n
    yc = y - mean
    var = head_sum(yc * yc) * inv_n
    yn = yc * lax.rsqrt(var + GN_EPS) * lnw_ref[...] + lnb_ref[...]
    o_ref[...] = ((yn + bonus) * g).astype(o_ref.dtype)


def _rwkv_branch(z, col0, z_lora, mu, w0, w2, a0, a2, g2, k_k, k_a, r_k, ln_w, ln_b, *, tc, pp):
    bsz, seq, _ = z.shape
    width = w0.shape[0]
    bw = pp * LANES
    segs = _lane_segments((w2.shape[0], a2.shape[0], g2.shape[0]))
    lora_w = segs[-1][1]
    assert width % bw == 0 and col0 % bw == 0 and z_lora.shape[2] == lora_w
    assert seq % tc == 0 and tc % RWKV_CHUNK == 0 and tc % SUBLANES == 0
    groups = width // bw
    cb = col0 // bw

    def pad_rows(wm, seg):
        return jnp.pad(wm, ((0, seg[1] - seg[0] - wm.shape[0]), (0, 0))).astype(BF16)

    w2p, a2p, g2p = (pad_rows(wm, seg) for wm, seg in zip((w2, a2, g2), segs))
    mu_r, mu_k, mu_v = (mu[i * width:(i + 1) * width].reshape(1, width) for i in range(3))
    mu_l = _pad_segments(mu[3 * width:].reshape(1, -1), (w2.shape[0], a2.shape[0], g2.shape[0]), axis=1)
    row = lambda x: x.reshape(1, width)

    act = lambda off: pl.BlockSpec((None, tc, bw), lambda b, p, t: (b, t, cb + off * groups + p))
    vec = pl.BlockSpec((1, bw), lambda b, p, t: (0, p))
    lmat = lambda seg: pl.BlockSpec((seg[1] - seg[0], bw), lambda b, p, t: (0, p))
    return pl.pallas_call(
        functools.partial(_rwkv_kernel, tc=tc, pp=pp, segs=segs),
        grid=(bsz, groups, seq // tc),
        in_specs=[
            act(0), act(1), act(2),
            pl.BlockSpec((None, tc, lora_w), lambda b, p, t: (b, t, 0)),
            vec, vec, vec,
            pl.BlockSpec((1, lora_w), lambda b, p, t: (0, 0)),
            vec, lmat(segs[0]), vec, lmat(segs[1]), lmat(segs[2]), vec, vec, vec, vec, vec,
        ],
        out_specs=pl.BlockSpec((None, tc, bw), lambda b, p, t: (b, t, p)),
        out_shape=jax.ShapeDtypeStruct((bsz, seq, width), BF16),
        scratch_shapes=[
            pltpu.VMEM((tc + SUBLANES, bw), F32), pltpu.VMEM((tc + SUBLANES, bw), F32),
            pltpu.VMEM((tc + SUBLANES, bw), F32), pltpu.VMEM((tc + SUBLANES, lora_w), F32),
            pltpu.VMEM((pp, LANES, LANES), F32),
        ],
        compiler_params=_cparams(("parallel", "parallel", "arbitrary")),
        name="rwkv7",
    )(z, z, z, z_lora, mu_r, mu_k, mu_v, mu_l, row(w0), w2p, row(a0), a2p, g2p, row(k_k), row(k_a),
      row(r_k), row(ln_w), row(ln_b))


def _mlstm_kernel(q_ref, k_ref, v_ref, zo_ref, gcol_ref, grow_ref, bcol_ref, brow_ref, nw_ref, o_ref,
                  c_st, n_st, m_st, *, heads, dqk, dv):
    L = q_ref.shape[0]

    @pl.when(pl.program_id(1) == 0)
    def _():
        c_st[...] = jnp.zeros(c_st.shape, F32)
        n_st[...] = jnp.zeros(n_st.shape, F32)
        m_st[...] = jnp.zeros(m_st.shape, F32)

    ri = lax.broadcasted_iota(jnp.int32, (L, L), 0)
    ci = lax.broadcasted_iota(jnp.int32, (L, L), 1)
    causal = ci <= ri
    tril_ones = jnp.where(causal, 1.0, 0.0).astype(F32)
    triu_ones = jnp.where(ci >= ri, 1.0, 0.0).astype(F32)

    gc = gcol_ref[...] + bcol_ref[...]
    gr = grow_ref[...] + brow_ref[...]
    b_c = _dot(tril_ones, -_softplus(-gc), precision=HIGHEST)
    b_r = _dot(-_softplus(-gr), triu_ones, precision=HIGHEST)

    per_head = lambda f: jnp.stack([f(h) for h in range(heads)])
    q = per_head(lambda h: q_ref[:, h * dqk:(h + 1) * dqk])
    q = (q.astype(F32) * dqk ** -0.5).astype(BF16)
    k = per_head(lambda h: k_ref[:, h * dqk:(h + 1) * dqk])
    v = per_head(lambda h: v_ref[:, h * dv:(h + 1) * dv])
    bc = per_head(lambda h: b_c[:, heads + h:heads + h + 1])
    lic = per_head(lambda h: gc[:, h:h + 1])
    br = per_head(lambda h: b_r[heads + h:heads + h + 1, :])
    lir = per_head(lambda h: gr[h:h + 1, :])
    m_prev = m_st[...][:, 0:1, 0:1]
    c_mat = c_st[...]
    n_vec = n_st[...]

    log_d = jnp.where(causal[None], bc - br + lir, -jnp.inf)
    log_inter = bc + m_prev
    m_t = jnp.maximum(jnp.max(log_d, axis=-1, keepdims=True), log_inter)
    p = _bdot(q, k, _BNT) * jnp.exp(log_d - m_t)
    inter = jnp.exp(log_inter - m_t)
    num = _bdot(p, v, _BNN) + inter * _bdot(q, c_mat, _BNN)
    den = (jnp.sum(p, axis=-1, keepdims=True)
           + inter * jnp.sum(q.astype(F32) * n_vec, axis=-1, keepdims=True))
    hs = num / jnp.maximum(jnp.abs(den), jnp.exp(-m_t))

    b_last = bc[:, L - 1:L, :]
    log_g = b_last - bc + lic
    m_new = jnp.maximum(b_last + m_prev, jnp.max(log_g, axis=1, keepdims=True))
    carry = jnp.exp(b_last + m_prev - m_new)
    wk = k.astype(F32) * jnp.exp(log_g - m_new)
    c_st[...] = carry * c_mat + _bdot(wk, v, _BTN)
    n_st[...] = carry * n_vec + jnp.sum(wk, axis=1, keepdims=True)
    m_st[...] = jnp.broadcast_to(m_new, m_st.shape)

    hs = hs * lax.rsqrt(jnp.mean(hs * hs, axis=-1, keepdims=True) + NORM_EPS)
    for h in range(heads):
        sl = slice(h * dv, (h + 1) * dv)
        o_ref[:, sl] = (hs[h] * nw_ref[:, sl] * _sigmoid(zo_ref[:, sl].astype(F32))).astype(o_ref.dtype)


def _mlstm_branch(z, gates, b_i, b_f, norm_w, *, heads, qk, vdim):
    bsz, seq, _ = z.shape
    L = ML_CHUNK_ROWS
    dqk, dv = qk // heads, vdim // heads
    assert seq % L == 0 and vdim % qk == 0 and qk % LANES == 0 and (2 * qk) % vdim == 0
    vblk = 2 * qk // vdim
    bias = jnp.concatenate([b_i, b_f]).astype(F32)
    bias_col = jnp.zeros((1, LANES), F32).at[0, :2 * heads].set(bias)
    gates_row = jnp.swapaxes(gates[:, :, :2 * heads], 1, 2)
    return pl.pallas_call(
        functools.partial(_mlstm_kernel, heads=heads, dqk=dqk, dv=dv),
        grid=(bsz, seq // L),
        in_specs=[
            pl.BlockSpec((None, L, qk), lambda b, c: (b, c, 0)),
            pl.BlockSpec((None, L, qk), lambda b, c: (b, c, 1)),
            pl.BlockSpec((None, L, vdim), lambda b, c: (b, c, vblk)),
            pl.BlockSpec((None, L, vdim), lambda b, c: (b, c, vblk + 1)),
            pl.BlockSpec((None, L, LANES), lambda b, c: (b, c, 0)),
            pl.BlockSpec((None, 2 * heads, L), lambda b, c: (b, 0, c)),
            pl.BlockSpec((1, LANES), lambda b, c: (0, 0)),
            pl.BlockSpec((2 * heads, 1), lambda b, c: (0, 0)),
            pl.BlockSpec((1, vdim), lambda b, c: (0, 0)),
        ],
        out_specs=pl.BlockSpec((None, L, vdim), lambda b, c: (b, c, 0)),
        out_shape=jax.ShapeDtypeStruct((bsz, seq, vdim), BF16),
        scratch_shapes=[
            pltpu.VMEM((heads, dqk, dv), F32),
            pltpu.VMEM((heads, 1, dqk), F32),
            pltpu.VMEM((heads, SUBLANES, LANES), F32),
        ],
        compiler_params=_cparams(("parallel", "arbitrary")),
        name="mlstm",
    )(z, z, z, z, gates, gates_row, bias_col, bias.reshape(-1, 1), norm_w.reshape(1, vdim))


def _col_tile(n, cap):
    for step in (MXU_WIDTH, LANES):
        fits = [c for c in range(step, cap + 1, step) if n % c == 0]
        if fits:
            return fits[-1]
    raise ValueError(f"no lane-aligned column tile divides {n}")


def _lane_segments(widths):
    segs, lo = [], 0
    for w in widths:
        hi = lo + -(-w // LANES) * LANES
        segs.append((lo, hi))
        lo = hi
    return segs


def _pad_segments(x, widths, axis):
    assert x.shape[axis] == sum(widths)
    parts, start = [], 0
    for w, (lo, hi) in zip(widths, _lane_segments(widths)):
        part = lax.slice_in_dim(x, start, start + w, axis=axis)
        pad = [(0, 0)] * x.ndim
        pad[axis] = (0, hi - lo - w)
        parts.append(jnp.pad(part, pad))
        start += w
    return jnp.concatenate(parts, axis=axis)


def kernel(x, norm_mix, norm_mlp, norm_final, mlp_up, mlp_down, hy_in, lru_conv_w, lru_conv_b, lru_wa, lru_ba, lru_wx, lru_bx, lru_lam, rwkv_mu, rwkv_w0, rwkv_w2, rwkv_a0, rwkv_a2, rwkv_g2, rwkv_kk, rwkv_ka, rwkv_rk, rwkv_ln_w, rwkv_ln_b, hy_out, ml_in, ml_bi, ml_bf, ml_norm, ml_out):
    bsz, seq, d = x.shape
    t = bsz * seq
    depth = norm_mix.shape[0]
    xf = x.reshape(t, d)

    lru_w = lru_lam.shape[1]
    rw_w = rwkv_w0.shape[1]
    hy_main = 2 * lru_w + 3 * rw_w
    heads = ml_bi.shape[1]
    vdim = ml_norm.shape[1]
    qk = (ml_in.shape[2] - 2 * vdim - 2 * heads) // 2
    ml_main = 2 * qk + 2 * vdim

    hy_in_t, ml_in_t = jnp.swapaxes(hy_in, 1, 2), jnp.swapaxes(ml_in, 1, 2)
    lora_widths = (rwkv_w2.shape[1], rwkv_a2.shape[1], rwkv_g2.shape[1])
    hy_lora_t = _pad_segments(hy_in_t[:, hy_main:, :], lora_widths, axis=1)
    ml_gate_t = _pad_segments(ml_in_t[:, ml_main:, :], (2 * heads,), axis=1)
    tn = lambda n: _col_tile(n, COL_TILE)
    tk_out = min(K_TILE_OUT, lru_w, vdim)

    for layer in range(depth):
        hn = _norm_cast(xf, norm_mix[layer], BF16)
        if layer % 2 == 0:
            e = layer // 2
            z = _matmul([hn], hy_in_t, e, hy_main, tk=d, tn=tn(hy_main), out_dtype=BF16, w_t=True)
            z = z.reshape(bsz, seq, hy_main)
            n_lora = hy_lora_t.shape[1]
            z_lora = _matmul([hn], hy_lora_t, e, n_lora, tk=d, tn=tn(n_lora), w_t=True,
                             tm=ROW_TILE // 2, lhs_double_buffer=True)
            z_lora = z_lora.reshape(bsz, seq, n_lora)
            ya = _lru_branch(z, lru_conv_w[e], lru_conv_b[e], lru_wa[e], lru_ba[e], lru_wx[e], lru_bx[e],
                             lru_lam[e], tc=min(LRU_ROWS, seq))
            yb = _rwkv_branch(z, 2 * lru_w, z_lora, rwkv_mu[e], rwkv_w0[e], rwkv_w2[e], rwkv_a0[e], rwkv_a2[e],
                              rwkv_g2[e], rwkv_kk[e], rwkv_ka[e], rwkv_rk[e], rwkv_ln_w[e], rwkv_ln_b[e],
                              tc=min(RWKV_ROWS, seq), pp=min(RWKV_PAIRS, rw_w // LANES))
            lhs = [ya.reshape(t, lru_w), yb.reshape(t, rw_w)]
            assert lru_w == rw_w
            xf = _matmul(lhs, hy_out, e, d, tk=tk_out, tn=tn(d), res=xf)
        else:
            o = layer // 2
            z = _matmul([hn], ml_in_t, o, ml_main, tk=d, tn=tn(ml_main), out_dtype=BF16, w_t=True)
            z = z.reshape(bsz, seq, ml_main)
            n_gate = ml_gate_t.shape[1]
            gates = _matmul([hn], ml_gate_t, o, n_gate, tk=d, tn=tn(n_gate), w_t=True,
                            tm=ROW_TILE // 2, lhs_double_buffer=True)
            gates = gates.reshape(bsz, seq, n_gate)
            hs = _mlstm_branch(z, gates, ml_bi[o], ml_bf[o], ml_norm[o], heads=heads, qk=qk, vdim=vdim)
            xf = _matmul([hs.reshape(t, vdim)], ml_out, o, d, tk=tk_out, tn=tn(d), res=xf)
        hn = _norm_cast(xf, norm_mlp[layer], BF16)
        u = _matmul([hn], mlp_up, layer, mlp_up.shape[2], tk=d, tn=tn(mlp_up.shape[2]), out_dtype=BF16,
                    relu2=True)
        xf = _matmul([u], mlp_down, layer, d, tk=min(K_TILE_DOWN, mlp_down.shape[1]),
                     tn=_col_tile(d, DOWN_COL_TILE), res=xf)
    return _norm_cast(xf, norm_final, F32).reshape(bsz, seq, d)
```

```python
import functools

import jax
import jax.numpy as jnp
from jax import lax
from jax.experimental import pallas as pl
from jax.experimental.pallas import tpu as pltpu

F32 = jnp.float32
BF16 = jnp.bfloat16
HIGHEST = lax.Precision.HIGHEST

NORM_EPS = 1e-6
GN_EPS = 64e-5
LRU_C = 8.0
CONV_WIDTH = 4
RWKV_HEAD = 64
DECAY_SCALE = 0.6065306597126334
LANES = 128
SUBLANES = 8
MXU_WIDTH = 256
VMEM_LIMIT_BYTES = 60 * 1024 * 1024
RWKV_CHUNK = 64
RWKV_ROWS = 128
RWKV_PAIRS = 16
LRU_ROWS = 2048
ML_CHUNK_ROWS = 256
NORM_ROWS = 512
ROW_TILE = 2048
COL_TILE = 512
K_TILE_OUT = 2048
DOWN_COL_TILE = 1024
K_TILE_DOWN = 1024


def _cparams(semantics):
    return pltpu.CompilerParams(dimension_semantics=semantics, vmem_limit_bytes=VMEM_LIMIT_BYTES)


def _softplus(x):
    return jnp.maximum(x, 0.0) + jnp.log1p(jnp.exp(-jnp.abs(x)))


def _sigmoid(x):
    return 1.0 / (1.0 + jnp.exp(-x))


_NN = (((1,), (0,)), ((), ()))
_NT = (((1,), (1,)), ((), ()))
_TN = (((0,), (0,)), ((), ()))
_BNN = (((2,), (1,)), ((0,), (0,)))
_BNT = (((2,), (2,)), ((0,), (0,)))
_BTN = (((1,), (1,)), ((0,), (0,)))


def _dot(a, b, dims=_NN, precision=None):
    return lax.dot_general(a, b, dims, precision=precision, preferred_element_type=F32)


def _bdot(a, b, dims=_NN):
    return lax.dot_general(a.astype(BF16), b.astype(BF16), dims, preferred_element_type=F32)


def _norm_cast_kernel(x_ref, g_ref, o_ref):
    x = x_ref[...]
    ms = jnp.mean(x * x, axis=-1, keepdims=True)
    o_ref[...] = (x * lax.rsqrt(ms + NORM_EPS) * g_ref[...]).astype(o_ref.dtype)


def _norm_cast(x, g, out_dtype):
    t, d = x.shape
    tm = min(NORM_ROWS, t)
    return pl.pallas_call(
        _norm_cast_kernel,
        grid=(t // tm,),
        in_specs=[pl.BlockSpec((tm, d), lambda i: (i, 0)), pl.BlockSpec((1, d), lambda i: (0, 0))],
        out_specs=pl.BlockSpec((tm, d), lambda i: (i, 0)),
        out_shape=jax.ShapeDtypeStruct((t, d), out_dtype),
        compiler_params=_cparams(("parallel",)),
        name="rmsnorm",
    )(x, g.reshape(1, d))


def _matmul_kernel(*refs, n_lhs, nk, has_res, relu2, w_t):
    a_refs, w_ref = refs[:n_lhs], refs[n_lhs]
    r_ref = refs[n_lhs + 1] if has_res else None
    o_ref = refs[n_lhs + 1 + has_res]
    acc_ref = refs[-1] if (not has_res and nk > 1) else None
    kk = pl.program_id(2)

    def finish(acc):
        if relu2:
            acc = jnp.maximum(acc, 0.0)
            acc = acc * acc
        return acc.astype(o_ref.dtype)

    def accumulate(a_ref):
        def part():
            return lax.dot_general(a_ref[...], w_ref[...].astype(BF16), _NT if w_t else _NN,
                                   preferred_element_type=F32)

        if has_res:
            @pl.when(kk == 0)
            def _():
                o_ref[...] = r_ref[...] + part()

            @pl.when(kk > 0)
            def _():
                o_ref[...] += part()
        elif nk == 1:
            o_ref[...] = finish(part())
        else:
            @pl.when(kk == 0)
            def _():
                acc_ref[...] = part()

            @pl.when((kk > 0) & (kk < nk - 1))
            def _():
                acc_ref[...] += part()

            @pl.when(kk == nk - 1)
            def _():
                o_ref[...] = finish(acc_ref[...] + part())

    if n_lhs == 1:
        accumulate(a_refs[0])
    else:
        per = nk // n_lhs
        for idx, a_ref in enumerate(a_refs):
            pl.when((kk >= idx * per) & (kk < (idx + 1) * per))(functools.partial(accumulate, a_ref))


def _matmul(lhs, w, layer, n, *, tk, tn, tm=ROW_TILE, out_dtype=F32, res=None, relu2=False, w_t=False,
            lhs_double_buffer=False):
    t = lhs[0].shape[0]
    k = w.shape[2 if w_t else 1]
    tm = min(tm, t)
    nk = k // tk
    assert t % tm == 0 and k % tk == 0 and n % tn == 0
    per = nk // len(lhs)
    assert all(a.shape[1] == per * tk for a in lhs) and per * len(lhs) == nk
    once = dict(pipeline_mode=pl.Buffered(1)) if (per == 1 and not lhs_double_buffer) else {}

    def lhs_spec(idx):
        return pl.BlockSpec((tm, tk), lambda i, j, kk: (i, jnp.clip(kk - idx * per, 0, per - 1)), **once)

    a_specs = [lhs_spec(idx) for idx in range(len(lhs))]
    has_res = res is not None
    assert not (has_res and (relu2 or out_dtype != F32))
    tile = pl.BlockSpec((tm, tn), lambda i, j, kk: (i, j))
    scratch = [pltpu.VMEM((tm, tn), F32)] if (not has_res and nk > 1) else []
    if w_t:
        w_spec = pl.BlockSpec((None, tn, tk), lambda i, j, kk: (layer, j, kk))
    else:
        w_spec = pl.BlockSpec((None, tk, tn), lambda i, j, kk: (layer, kk, j))
    return pl.pallas_call(
        functools.partial(_matmul_kernel, n_lhs=len(lhs), nk=nk, has_res=has_res, relu2=relu2, w_t=w_t),
        grid=(t // tm, n // tn, nk),
        in_specs=a_specs + [w_spec] + ([tile] if has_res else []),
        out_specs=tile,
        out_shape=jax.ShapeDtypeStruct((t, n), out_dtype),
        scratch_shapes=scratch,
        compiler_params=_cparams(("parallel", "parallel", "arbitrary")),
        name="matmul",
    )(*lhs, w, *([res] if has_res else []))


def _lru_kernel(u_ref, gate_ref, cw_ref, cb_ref, wa_ref, ba_ref, wx_ref, bx_ref, lam_ref,
                o_ref, xbuf, hcar, *, tc):
    @pl.when(pl.program_id(2) == 0)
    def _():
        xbuf[0:SUBLANES, :] = jnp.zeros((SUBLANES, xbuf.shape[1]), F32)
        hcar[...] = jnp.zeros(hcar.shape, F32)

    u = u_ref[...].astype(F32)
    xbuf[SUBLANES:, :] = u
    cw = cw_ref[...]
    conv = cb_ref[...] + cw[CONV_WIDTH - 1:CONV_WIDTH] * u
    for j in range(CONV_WIDTH - 1):
        back = CONV_WIDTH - 1 - j
        conv = conv + cw[j:j + 1] * xbuf[SUBLANES - back:SUBLANES - back + tc, :]
    xbuf[0:SUBLANES, :] = u[tc - SUBLANES:, :]

    cbf = conv.astype(BF16)
    r = _sigmoid(jnp.dot(cbf, wa_ref[...], preferred_element_type=F32) + ba_ref[...])
    i = _sigmoid(jnp.dot(cbf, wx_ref[...], preferred_element_type=F32) + bx_ref[...])
    log_a = (-LRU_C) * r * _softplus(-lam_ref[...])
    a = jnp.exp(log_a)
    b = jnp.sqrt(-jnp.tanh(log_a) * (a * a + 1.0)) * (i * conv)

    n_tiles = tc // SUBLANES
    a = a.reshape(n_tiles, SUBLANES, a.shape[1])
    b = b.reshape(n_tiles, SUBLANES, b.shape[1])
    sub = lax.broadcasted_iota(jnp.int32, a.shape, 1)
    d = 1
    while d < SUBLANES:
        a_sh = pltpu.roll(a, d, 1)
        b_sh = pltpu.roll(b, d, 1)
        keep = sub >= d
        b = jnp.where(keep, a * b_sh + b, b)
        a = jnp.where(keep, a * a_sh, a)
        d *= 2
    carry = hcar[...]
    tiles = []
    for i in range(n_tiles):
        tiles.append(a[i] * carry + b[i])
        carry = tiles[-1][SUBLANES - 1:SUBLANES, :]
    h = jnp.concatenate(tiles, axis=0)
    hcar[...] = carry
    o_ref[...] = (h * jax.nn.gelu(gate_ref[...].astype(F32))).astype(o_ref.dtype)


def _lru_branch(z, conv_w, conv_b, w_a, b_a, w_x, b_x, lam, *, tc):
    bsz, seq, _ = z.shape
    heads, blk, _ = w_a.shape
    width = heads * blk
    assert blk % LANES == 0 and seq % tc == 0
    row = lambda v: v.reshape(1, width)
    vec_spec = pl.BlockSpec((1, blk), lambda b, h, t: (0, h))
    mat_spec = pl.BlockSpec((None, blk, blk), lambda b, h, t: (h, 0, 0))
    return pl.pallas_call(
        functools.partial(_lru_kernel, tc=tc),
        grid=(bsz, heads, seq // tc),
        in_specs=[
            pl.BlockSpec((None, tc, blk), lambda b, h, t: (b, t, h)),
            pl.BlockSpec((None, tc, blk), lambda b, h, t: (b, t, heads + h)),
            pl.BlockSpec((CONV_WIDTH, blk), lambda b, h, t: (0, h)),
            vec_spec, mat_spec, vec_spec, mat_spec, vec_spec, vec_spec,
        ],
        out_specs=pl.BlockSpec((None, tc, blk), lambda b, h, t: (b, t, h)),
        out_shape=jax.ShapeDtypeStruct((bsz, seq, width), BF16),
        scratch_shapes=[pltpu.VMEM((tc + SUBLANES, blk), F32), pltpu.VMEM((1, blk), F32)],
        compiler_params=_cparams(("parallel", "parallel", "arbitrary")),
        name="rglru",
    )(z, z, conv_w, row(conv_b), w_a.astype(BF16), row(b_a), w_x.astype(BF16), row(b_x), row(lam))


def _rwkv_kernel(r_ref, k_ref, v_ref, lora_ref, mu_r, mu_k, mu_v, mu_l, w0_ref, w2_ref, a0_ref, a2_ref,
                 g2_ref, kk_ref, ka_ref, rk_ref, lnw_ref, lnb_ref, o_ref,
                 rbuf, kbuf, vbuf, lbuf, state, *, tc, pp, segs):
    L = RWKV_CHUNK
    P2 = 2 * L
    nc = tc // L
    assert P2 == LANES and RWKV_HEAD == L

    @pl.when(pl.program_id(2) == 0)
    def _():
        for buf in (rbuf, kbuf, vbuf, lbuf):
            buf[0:SUBLANES, :] = jnp.zeros((SUBLANES, buf.shape[1]), F32)
        state[...] = jnp.zeros(state.shape, F32)

    def token_shift(ref, buf, mu_ref):
        cur = ref[...].astype(F32)
        buf[SUBLANES:, :] = cur
        prev = buf[SUBLANES - 1:SUBLANES - 1 + tc, :]
        buf[0:SUBLANES, :] = cur[tc - SUBLANES:, :]
        return cur + (prev - cur) * mu_ref[...]

    r = token_shift(r_ref, rbuf, mu_r)
    k = token_shift(k_ref, kbuf, mu_k)
    v = token_shift(v_ref, vbuf, mu_v)
    lora = token_shift(lora_ref, lbuf, mu_l)

    lora_w, lora_a, lora_g = (lora[:, lo:hi] for lo, hi in segs)
    dw = _dot(jnp.tanh(lora_w).astype(BF16), w2_ref[...])
    logw = -DECAY_SCALE * _sigmoid(w0_ref[...] + dw)
    a = _sigmoid(a0_ref[...] + _dot(lora_a.astype(BF16), a2_ref[...]))
    g = _dot(_sigmoid(lora_g).astype(BF16), g2_ref[...])

    ri = lax.broadcasted_iota(jnp.int32, (P2, P2), 0)
    ci = lax.broadcasted_iota(jnp.int32, (P2, P2), 1)
    blk = jnp.where(ri >= L, 1, 0) == jnp.where(ci >= L, 1, 0)
    head_ones = jnp.where(blk, 1.0, 0.0).astype(BF16)

    def to_rows(x):
        return jnp.concatenate([x[:, j * LANES:(j + 1) * LANES] for j in range(pp)], axis=0)

    def from_rows(x):
        return jnp.concatenate([x[j * tc:(j + 1) * tc, :] for j in range(pp)], axis=1)

    def head_sum(x):
        return from_rows(_dot(to_rows(x).astype(BF16), head_ones))

    kk = k * kk_ref[...]
    kk = kk * lax.rsqrt(jnp.maximum(head_sum(kk * kk), 1e-24))
    kmod = k * (1.0 + (a - 1.0) * ka_ref[...])
    kka = kk * a
    bonus = head_sum(r * kmod * rk_ref[...]) * v

    rt = lax.broadcasted_iota(jnp.int32, (tc, tc), 0)
    ct = lax.broadcasted_iota(jnp.int32, (tc, tc), 1)
    shift = L.bit_length() - 1
    same_chunk = lax.shift_right_logical(rt, shift) == lax.shift_right_logical(ct, shift)
    tril_chunk = jnp.where(same_chunk & (ct <= rt), 1.0, 0.0).astype(BF16)
    logw_hi = logw.astype(BF16)
    logw_lo = (logw - logw_hi.astype(F32)).astype(BF16)
    cum = _dot(tril_chunk, logw_hi) + _dot(tril_chunk, logw_lo)
    p_inv = jnp.exp(-cum)
    r_d = r * jnp.exp(cum)
    kk_d = kk * jnp.exp(cum - logw)
    k_d = kmod * p_inv
    a_d = kka * p_inv
    p_last = [jnp.exp(cum[(c + 1) * L - 1:(c + 1) * L, :]) for c in range(nc)]
    p_tail = p_inv * jnp.concatenate([jnp.broadcast_to(p, (L, cum.shape[1])) for p in p_last], axis=0)
    k_t = kmod * p_tail
    a_t = kka * p_tail

    strict = (blk & (ci < ri))[None]
    incl = (blk & (ci <= ri))[None]
    eye = jnp.where(ri == ci, 1.0, 0.0).astype(F32)[None]
    lane_row = lax.broadcasted_iota(jnp.int32, (1, LANES), 1)
    m0 = jnp.where(lane_row < RWKV_HEAD, 1.0, 0.0).astype(F32)
    m1 = 1.0 - m0

    def batch(x, zero_other_head=True):
        out = []
        for c in range(nc):
            for j in range(pp):
                xs = x[c * L:(c + 1) * L, j * LANES:(j + 1) * LANES]
                out.append(jnp.concatenate([xs * m0, xs * m1] if zero_other_head else [xs, xs], axis=0))
        return jnp.stack(out)

    r_b, kk_b, v_b, kt_b, at_b = (batch(x) for x in (r_d, kk_d, v, k_t, a_t))
    k_b, a_b = batch(k_d, False), batch(a_d, False)
    kkr_b = jnp.concatenate([kk_b, r_b], axis=1)

    gram = _bdot(kkr_b, jnp.concatenate([k_b, a_b], axis=1), _BNT)
    a_kk_k = jnp.where(strict, gram[:, 0:P2, 0:P2], 0.0)
    n_mat = jnp.where(strict, gram[:, 0:P2, P2:], 0.0)
    a_r_k = jnp.where(incl, gram[:, P2:, 0:P2], 0.0)
    a_r_a = jnp.where(incl, gram[:, P2:, P2:], 0.0)

    inv = eye - n_mat
    pw = _bdot(n_mat, n_mat, _BNN)
    steps = 2
    while 2 * steps < L:
        both = _bdot(jnp.concatenate([inv, pw], axis=1), pw, _BNN)
        inv = inv + both[:, 0:P2]
        pw = both[:, P2:]
        steps *= 2
    inv = inv + _bdot(inv, pw, _BNN)
    av = _bdot(jnp.concatenate([a_kk_k, a_r_k], axis=1), v_b, _BNN)

    s = state[...]
    y_chunks = []
    for c in range(nc):
        sl = slice(c * pp, (c + 1) * pp)
        from_state = _bdot(kkr_b[sl], s, _BNT) + av[sl]
        sa = _bdot(inv[sl], from_state[:, 0:P2], _BNN)
        y_s = from_state[:, P2:] - _bdot(a_r_a[sl], sa, _BNN)
        y_c = y_s[:, 0:L, :] + y_s[:, L:, :]
        y_chunks.append(jnp.concatenate([y_c[j] for j in range(pp)], axis=1))
        decay = jnp.stack([p_last[c][:, j * LANES:(j + 1) * LANES] for j in range(pp)])
        s = s * decay + _bdot(jnp.concatenate([v_b[sl], sa], axis=1),
                              jnp.concatenate([kt_b[sl], -at_b[sl]], axis=1), _BTN)
    state[...] = s

    y = jnp.concatenate(y_chunks, axis=0)
    inv_n = 1.0 / RWKV_HEAD
    mean = head_sum(y) * inv_n
    yc = y - mean
    var = head_sum(yc * yc) * inv_n
    yn = yc * lax.rsqrt(var + GN_EPS) * lnw_ref[...] + lnb_ref[...]
    o_ref[...] = ((yn + bonus) * g).astype(o_ref.dtype)


def _rwkv_branch(z, col0, z_lora, mu, w0, w2, a0, a2, g2, k_k, k_a, r_k, ln_w, ln_b, *, tc, pp):
    bsz, seq, _ = z.shape
    width = w0.shape[0]
    bw = pp * LANES
    segs = _lane_segments((w2.shape[0], a2.shape[0], g2.shape[0]))
    lora_w = segs[-1][1]
    assert width % bw == 0 and col0 % bw == 0 and z_lora.shape[2] == lora_w
    assert seq % tc == 0 and tc % RWKV_CHUNK == 0 and tc % SUBLANES == 0
    groups = width // bw
    cb = col0 // bw

    def pad_rows(wm, seg):
        return jnp.pad(wm, ((0, seg[1] - seg[0] - wm.shape[0]), (0, 0))).astype(BF16)

    w2p, a2p, g2p = (pad_rows(wm, seg) for wm, seg in zip((w2, a2, g2), segs))
    mu_r, mu_k, mu_v = (mu[i * width:(i + 1) * width].reshape(1, width) for i in range(3))
    mu_l = _pad_segments(mu[3 * width:].reshape(1, -1), (w2.shape[0], a2.shape[0], g2.shape[0]), axis=1)
    row = lambda x: x.reshape(1, width)

    act = lambda off: pl.BlockSpec((None, tc, bw), lambda b, p, t: (b, t, cb + off * groups + p))
    vec = pl.BlockSpec((1, bw), lambda b, p, t: (0, p))
    lmat = lambda seg: pl.BlockSpec((seg[1] - seg[0], bw), lambda b, p, t: (0, p))
    return pl.pallas_call(
        functools.partial(_rwkv_kernel, tc=tc, pp=pp, segs=segs),
        grid=(bsz, groups, seq // tc),
        in_specs=[
            act(0), act(1), act(2),
            pl.BlockSpec((None, tc, lora_w), lambda b, p, t: (b, t, 0)),
            vec, vec, vec,
            pl.BlockSpec((1, lora_w), lambda b, p, t: (0, 0)),
            vec, lmat(segs[0]), vec, lmat(segs[1]), lmat(segs[2]), vec, vec, vec, vec, vec,
        ],
        out_specs=pl.BlockSpec((None, tc, bw), lambda b, p, t: (b, t, p)),
        out_shape=jax.ShapeDtypeStruct((bsz, seq, width), BF16),
        scratch_shapes=[
            pltpu.VMEM((tc + SUBLANES, bw), F32), pltpu.VMEM((tc + SUBLANES, bw), F32),
            pltpu.VMEM((tc + SUBLANES, bw), F32), pltpu.VMEM((tc + SUBLANES, lora_w), F32),
            pltpu.VMEM((pp, LANES, LANES), F32),
        ],
        compiler_params=_cparams(("parallel", "parallel", "arbitrary")),
        name="rwkv7",
    )(z, z, z, z_lora, mu_r, mu_k, mu_v, mu_l, row(w0), w2p, row(a0), a2p, g2p, row(k_k), row(k_a),
      row(r_k), row(ln_w), row(ln_b))


def _mlstm_kernel(q_ref, k_ref, v_ref, zo_ref, gcol_ref, grow_ref, bcol_ref, brow_ref, nw_ref, o_ref,
                  c_st, n_st, m_st, *, heads, dqk, dv):
    L = q_ref.shape[0]

    @pl.when(pl.program_id(1) == 0)
    def _():
        c_st[...] = jnp.zeros(c_st.shape, F32)
        n_st[...] = jnp.zeros(n_st.shape, F32)
        m_st[...] = jnp.zeros(m_st.shape, F32)

    ri = lax.broadcasted_iota(jnp.int32, (L, L), 0)
    ci = lax.broadcasted_iota(jnp.int32, (L, L), 1)
    causal = ci <= ri
    tril_ones = jnp.where(causal, 1.0, 0.0).astype(F32)
    triu_ones = jnp.where(ci >= ri, 1.0, 0.0).astype(F32)

    gc = gcol_ref[...] + bcol_ref[...]
    gr = grow_ref[...] + brow_ref[...]
    b_c = _dot(tril_ones, -_softplus(-gc), precision=HIGHEST)
    b_r = _dot(-_softplus(-gr), triu_ones, precision=HIGHEST)

    per_head = lambda f: jnp.stack([f(h) for h in range(heads)])
    q = per_head(lambda h: q_ref[:, h * dqk:(h + 1) * dqk])
    q = (q.astype(F32) * dqk ** -0.5).astype(BF16)
    k = per_head(lambda h: k_ref[:, h * dqk:(h + 1) * dqk])
    v = per_head(lambda h: v_ref[:, h * dv:(h + 1) * dv])
    bc = per_head(lambda h: b_c[:, heads + h:heads + h + 1])
    lic = per_head(lambda h: gc[:, h:h + 1])
    br = per_head(lambda h: b_r[heads + h:heads + h + 1, :])
    lir = per_head(lambda h: gr[h:h + 1, :])
    m_prev = m_st[...][:, 0:1, 0:1]
    c_mat = c_st[...]
    n_vec = n_st[...]

    log_d = jnp.where(causal[None], bc - br + lir, -jnp.inf)
    log_inter = bc + m_prev
    m_t = jnp.maximum(jnp.max(log_d, axis=-1, keepdims=True), log_inter)
    p = _bdot(q, k, _BNT) * jnp.exp(log_d - m_t)
    inter = jnp.exp(log_inter - m_t)
    num = _bdot(p, v, _BNN) + inter * _bdot(q, c_mat, _BNN)
    den = (jnp.sum(p, axis=-1, keepdims=True)
           + inter * jnp.sum(q.astype(F32) * n_vec, axis=-1, keepdims=True))
    inv_den = 1.0 / jnp.maximum(jnp.abs(den), jnp.exp(-m_t))

    b_last = bc[:, L - 1:L, :]
    log_g = b_last - bc + lic
    m_new = jnp.maximum(b_last + m_prev, jnp.max(log_g, axis=1, keepdims=True))
    carry = jnp.exp(b_last + m_prev - m_new)
    wk = k.astype(F32) * jnp.exp(log_g - m_new)
    c_st[...] = carry * c_mat + _bdot(wk, v, _BTN)
    n_st[...] = carry * n_vec + jnp.sum(wk, axis=1, keepdims=True)
    m_st[...] = jnp.broadcast_to(m_new, m_st.shape)

    msq = jnp.mean(num * num, axis=-1, keepdims=True)
    hs = num * (inv_den * lax.rsqrt(inv_den * inv_den * msq + NORM_EPS))
    for h in range(heads):
        sl = slice(h * dv, (h + 1) * dv)
        o_ref[:, sl] = (hs[h] * nw_ref[:, sl] * _sigmoid(zo_ref[:, sl].astype(F32))).astype(o_ref.dtype)


def _mlstm_branch(z, gates, b_i, b_f, norm_w, *, heads, qk, vdim):
    bsz, seq, _ = z.shape
    L = ML_CHUNK_ROWS
    dqk, dv = qk // heads, vdim // heads
    assert seq % L == 0 and vdim % qk == 0 and qk % LANES == 0 and (2 * qk) % vdim == 0
    vblk = 2 * qk // vdim
    bias = jnp.concatenate([b_i, b_f]).astype(F32)
    bias_col = jnp.zeros((1, LANES), F32).at[0, :2 * heads].set(bias)
    gates_row = jnp.swapaxes(gates[:, :, :2 * heads], 1, 2)
    return pl.pallas_call(
        functools.partial(_mlstm_kernel, heads=heads, dqk=dqk, dv=dv),
        grid=(bsz, seq // L),
        in_specs=[
            pl.BlockSpec((None, L, qk), lambda b, c: (b, c, 0)),
            pl.BlockSpec((None, L, qk), lambda b, c: (b, c, 1)),
            pl.BlockSpec((None, L, vdim), lambda b, c: (b, c, vblk)),
            pl.BlockSpec((None, L, vdim), lambda b, c: (b, c, vblk + 1)),
            pl.BlockSpec((None, L, LANES), lambda b, c: (b, c, 0)),
            pl.BlockSpec((None, 2 * heads, L), lambda b, c: (b, 0, c)),
            pl.BlockSpec((1, LANES), lambda b, c: (0, 0)),
            pl.BlockSpec((2 * heads, 1), lambda b, c: (0, 0)),
            pl.BlockSpec((1, vdim), lambda b, c: (0, 0)),
        ],
        out_specs=pl.BlockSpec((None, L, vdim), lambda b, c: (b, c, 0)),
        out_shape=jax.ShapeDtypeStruct((bsz, seq, vdim), BF16),
        scratch_shapes=[
            pltpu.VMEM((heads, dqk, dv), F32),
            pltpu.VMEM((heads, 1, dqk), F32),
            pltpu.VMEM((heads, SUBLANES, LANES), F32),
        ],
        compiler_params=_cparams(("parallel", "arbitrary")),
        name="mlstm",
    )(z, z, z, z, gates, gates_row, bias_col, bias.reshape(-1, 1), norm_w.reshape(1, vdim))


def _col_tile(n, cap):
    for step in (MXU_WIDTH, LANES):
        fits = [c for c in range(step, cap + 1, step) if n % c == 0]
        if fits:
            return fits[-1]
    raise ValueError(f"no lane-aligned column tile divides {n}")


def _lane_segments(widths):
    segs, lo = [], 0
    for w in widths:
        hi = lo + -(-w // LANES) * LANES
        segs.append((lo, hi))
        lo = hi
    return segs


def _pad_segments(x, widths, axis):
    assert x.shape[axis] == sum(widths)
    parts, start = [], 0
    for w, (lo, hi) in zip(widths, _lane_segments(widths)):
        part = lax.slice_in_dim(x, start, start + w, axis=axis)
        pad = [(0, 0)] * x.ndim
        pad[axis] = (0, hi - lo - w)
        parts.append(jnp.pad(part, pad))
        start += w
    return jnp.concatenate(parts, axis=axis)


def kernel(x, norm_mix, norm_mlp, norm_final, mlp_up, mlp_down, hy_in, lru_conv_w, lru_conv_b, lru_wa, lru_ba, lru_wx, lru_bx, lru_lam, rwkv_mu, rwkv_w0, rwkv_w2, rwkv_a0, rwkv_a2, rwkv_g2, rwkv_kk, rwkv_ka, rwkv_rk, rwkv_ln_w, rwkv_ln_b, hy_out, ml_in, ml_bi, ml_bf, ml_norm, ml_out):
    bsz, seq, d = x.shape
    t = bsz * seq
    depth = norm_mix.shape[0]
    xf = x.reshape(t, d)

    lru_w = lru_lam.shape[1]
    rw_w = rwkv_w0.shape[1]
    hy_main = 2 * lru_w + 3 * rw_w
    heads = ml_bi.shape[1]
    vdim = ml_norm.shape[1]
    qk = (ml_in.shape[2] - 2 * vdim - 2 * heads) // 2
    ml_main = 2 * qk + 2 * vdim

    hy_in_t, ml_in_t = jnp.swapaxes(hy_in, 1, 2), jnp.swapaxes(ml_in, 1, 2)
    lora_widths = (rwkv_w2.shape[1], rwkv_a2.shape[1], rwkv_g2.shape[1])
    hy_lora_t = _pad_segments(hy_in_t[:, hy_main:, :], lora_widths, axis=1)
    ml_gate_t = _pad_segments(ml_in_t[:, ml_main:, :], (2 * heads,), axis=1)
    tn = lambda n: _col_tile(n, COL_TILE)
    tk_out = min(K_TILE_OUT, lru_w, vdim)

    for layer in range(depth):
        hn = _norm_cast(xf, norm_mix[layer], BF16)
        if layer % 2 == 0:
            e = layer // 2
            z = _matmul([hn], hy_in_t, e, hy_main, tk=d, tn=tn(hy_main), out_dtype=BF16, w_t=True)
            z = z.reshape(bsz, seq, hy_main)
            n_lora = hy_lora_t.shape[1]
            z_lora = _matmul([hn], hy_lora_t, e, n_lora, tk=d, tn=tn(n_lora), w_t=True,
                             tm=ROW_TILE // 2, lhs_double_buffer=True)
            z_lora = z_lora.reshape(bsz, seq, n_lora)
            ya = _lru_branch(z, lru_conv_w[e], lru_conv_b[e], lru_wa[e], lru_ba[e], lru_wx[e], lru_bx[e],
                             lru_lam[e], tc=min(LRU_ROWS, seq))
            yb = _rwkv_branch(z, 2 * lru_w, z_lora, rwkv_mu[e], rwkv_w0[e], rwkv_w2[e], rwkv_a0[e], rwkv_a2[e],
                              rwkv_g2[e], rwkv_kk[e], rwkv_ka[e], rwkv_rk[e], rwkv_ln_w[e], rwkv_ln_b[e],
                              tc=min(RWKV_ROWS, seq), pp=min(RWKV_PAIRS, rw_w // LANES))
            lhs = [ya.reshape(t, lru_w), yb.reshape(t, rw_w)]
            assert lru_w == rw_w
            xf = _matmul(lhs, hy_out, e, d, tk=tk_out, tn=tn(d), res=xf)
        else:
            o = layer // 2
            z = _matmul([hn], ml_in_t, o, ml_main, tk=d, tn=tn(ml_main), out_dtype=BF16, w_t=True)
            z = z.reshape(bsz, seq, ml_main)
            n_gate = ml_gate_t.shape[1]
            gates = _matmul([hn], ml_gate_t, o, n_gate, tk=d, tn=tn(n_gate), w_t=True,
                            tm=ROW_TILE // 2, lhs_double_buffer=True)
            gates = gates.reshape(bsz, seq, n_gate)
            hs = _mlstm_branch(z, gates, ml_bi[o], ml_bf[o], ml_norm[o], heads=heads, qk=qk, vdim=vdim)
            xf = _matmul([hs.reshape(t, vdim)], ml_out, o, d, tk=vdim, tn=tn(d), res=xf)
        hn = _norm_cast(xf, norm_mlp[layer], BF16)
        u = _matmul([hn], mlp_up, layer, mlp_up.shape[2], tk=d, tn=tn(mlp_up.shape[2]), out_dtype=BF16,
                    relu2=True)
        xf = _matmul([u], mlp_down, layer, d, tk=min(K_TILE_DOWN, mlp_down.shape[1]),
                     tn=_col_tile(d, DOWN_COL_TILE), res=xf)
    return _norm_cast(xf, norm_final, F32).reshape(bsz, seq, d)
```

```python
import functools

import jax
import jax.numpy as jnp
from jax import lax
from jax.experimental import pallas as pl
from jax.experimental.pallas import tpu as pltpu

F32 = jnp.float32
BF16 = jnp.bfloat16
HIGHEST = lax.Precision.HIGHEST

NORM_EPS = 1e-6
GN_EPS = 64e-5
LRU_C = 8.0
CONV_WIDTH = 4
RWKV_HEAD = 64
DECAY_SCALE = 0.6065306597126334
LANES = 128
SUBLANES = 8
MXU_WIDTH = 256
VMEM_LIMIT_BYTES = 60 * 1024 * 1024
RWKV_CHUNK = 64
RWKV_ROWS = 128
RWKV_PAIRS = 16
LRU_ROWS = 2048
ML_CHUNK_ROWS = 256
NORM_ROWS = 512
ROW_TILE = 2048
COL_TILE = 512
K_TILE_OUT = 2048
DOWN_COL_TILE = 1024
K_TILE_DOWN = 1024


def _cparams(semantics):
    return pltpu.CompilerParams(dimension_semantics=semantics, vmem_limit_bytes=VMEM_LIMIT_BYTES)


def _softplus(x):
    return jnp.maximum(x, 0.0) + jnp.log1p(jnp.exp(-jnp.abs(x)))


def _sigmoid(x):
    return 1.0 / (1.0 + jnp.exp(-x))


_NN = (((1,), (0,)), ((), ()))
_NT = (((1,), (1,)), ((), ()))
_TN = (((0,), (0,)), ((), ()))
_BNN = (((2,), (1,)), ((0,), (0,)))
_BNT = (((2,), (2,)), ((0,), (0,)))
_BTN = (((1,), (1,)), ((0,), (0,)))


def _dot(a, b, dims=_NN, precision=None):
    return lax.dot_general(a, b, dims, precision=precision, preferred_element_type=F32)


def _bdot(a, b, dims=_NN):
    return lax.dot_general(a.astype(BF16), b.astype(BF16), dims, preferred_element_type=F32)


def _norm_cast_kernel(x_ref, g_ref, o_ref):
    x = x_ref[...]
    ms = jnp.mean(x * x, axis=-1, keepdims=True)
    o_ref[...] = (x * lax.rsqrt(ms + NORM_EPS) * g_ref[...]).astype(o_ref.dtype)


def _norm_cast(x, g, out_dtype):
    t, d = x.shape
    tm = min(NORM_ROWS, t)
    return pl.pallas_call(
        _norm_cast_kernel,
        grid=(t // tm,),
        in_specs=[pl.BlockSpec((tm, d), lambda i: (i, 0)), pl.BlockSpec((1, d), lambda i: (0, 0))],
        out_specs=pl.BlockSpec((tm, d), lambda i: (i, 0)),
        out_shape=jax.ShapeDtypeStruct((t, d), out_dtype),
        compiler_params=_cparams(("parallel",)),
        name="rmsnorm",
    )(x, g.reshape(1, d))


def _matmul_kernel(*refs, n_lhs, nk, has_res, relu2, w_t):
    a_refs, w_ref = refs[:n_lhs], refs[n_lhs]
    r_ref = refs[n_lhs + 1] if has_res else None
    o_ref = refs[n_lhs + 1 + has_res]
    acc_ref = refs[-1] if (not has_res and nk > 1) else None
    kk = pl.program_id(2)

    def finish(acc):
        if relu2:
            acc = jnp.maximum(acc, 0.0)
            acc = acc * acc
        return acc.astype(o_ref.dtype)

    def accumulate(a_ref):
        def part():
            return lax.dot_general(a_ref[...], w_ref[...].astype(BF16), _NT if w_t else _NN,
                                   preferred_element_type=F32)

        if has_res:
            @pl.when(kk == 0)
            def _():
                o_ref[...] = r_ref[...] + part()

            @pl.when(kk > 0)
            def _():
                o_ref[...] += part()
        elif nk == 1:
            o_ref[...] = finish(part())
        else:
            @pl.when(kk == 0)
            def _():
                acc_ref[...] = part()

            @pl.when((kk > 0) & (kk < nk - 1))
            def _():
                acc_ref[...] += part()

            @pl.when(kk == nk - 1)
            def _():
                o_ref[...] = finish(acc_ref[...] + part())

    if n_lhs == 1:
        accumulate(a_refs[0])
    else:
        per = nk // n_lhs
        for idx, a_ref in enumerate(a_refs):
            pl.when((kk >= idx * per) & (kk < (idx + 1) * per))(functools.partial(accumulate, a_ref))


def _matmul(lhs, w, layer, n, *, tk, tn, tm=ROW_TILE, out_dtype=F32, res=None, relu2=False, w_t=False,
            lhs_double_buffer=False):
    t = lhs[0].shape[0]
    k = w.shape[2 if w_t else 1]
    tm = min(tm, t)
    nk = k // tk
    assert t % tm == 0 and k % tk == 0 and n % tn == 0
    per = nk // len(lhs)
    assert all(a.shape[1] == per * tk for a in lhs) and per * len(lhs) == nk
    once = dict(pipeline_mode=pl.Buffered(1)) if (per == 1 and not lhs_double_buffer) else {}

    def lhs_spec(idx):
        return pl.BlockSpec((tm, tk), lambda i, j, kk: (i, jnp.clip(kk - idx * per, 0, per - 1)), **once)

    a_specs = [lhs_spec(idx) for idx in range(len(lhs))]
    has_res = res is not None
    assert not (has_res and (relu2 or out_dtype != F32))
    tile = pl.BlockSpec((tm, tn), lambda i, j, kk: (i, j))
    scratch = [pltpu.VMEM((tm, tn), F32)] if (not has_res and nk > 1) else []
    if w_t:
        w_spec = pl.BlockSpec((None, tn, tk), lambda i, j, kk: (layer, j, kk))
    else:
        w_spec = pl.BlockSpec((None, tk, tn), lambda i, j, kk: (layer, kk, j))
    return pl.pallas_call(
        functools.partial(_matmul_kernel, n_lhs=len(lhs), nk=nk, has_res=has_res, relu2=relu2, w_t=w_t),
        grid=(t // tm, n // tn, nk),
        in_specs=a_specs + [w_spec] + ([tile] if has_res else []),
        out_specs=tile,
        out_shape=jax.ShapeDtypeStruct((t, n), out_dtype),
        scratch_shapes=scratch,
        compiler_params=_cparams(("parallel", "parallel", "arbitrary")),
        name="matmul",
    )(*lhs, w, *([res] if has_res else []))


def _lru_kernel(u_ref, gate_ref, cw_ref, cb_ref, wa_ref, ba_ref, wx_ref, bx_ref, lam_ref,
                o_ref, xbuf, hcar, *, tc):
    @pl.when(pl.program_id(2) == 0)
    def _():
        xbuf[0:SUBLANES, :] = jnp.zeros((SUBLANES, xbuf.shape[1]), F32)
        hcar[...] = jnp.zeros(hcar.shape, F32)

    u = u_ref[...].astype(F32)
    xbuf[SUBLANES:, :] = u
    cw = cw_ref[...]
    conv = cb_ref[...] + cw[CONV_WIDTH - 1:CONV_WIDTH] * u
    for j in range(CONV_WIDTH - 1):
        back = CONV_WIDTH - 1 - j
        conv = conv + cw[j:j + 1] * xbuf[SUBLANES - back:SUBLANES - back + tc, :]
    xbuf[0:SUBLANES, :] = u[tc - SUBLANES:, :]

    cbf = conv.astype(BF16)
    r = _sigmoid(jnp.dot(cbf, wa_ref[...], preferred_element_type=F32) + ba_ref[...])
    i = _sigmoid(jnp.dot(cbf, wx_ref[...], preferred_element_type=F32) + bx_ref[...])
    log_a = (-LRU_C) * r * _softplus(-lam_ref[...])
    a = jnp.exp(log_a)
    b = jnp.sqrt(-jnp.tanh(log_a) * (a * a + 1.0)) * (i * conv)

    n_tiles = tc // SUBLANES
    a = a.reshape(n_tiles, SUBLANES, a.shape[1])
    b = b.reshape(n_tiles, SUBLANES, b.shape[1])
    sub = lax.broadcasted_iota(jnp.int32, a.shape, 1)
    d = 1
    while d < SUBLANES:
        a_sh = pltpu.roll(a, d, 1)
        b_sh = pltpu.roll(b, d, 1)
        keep = sub >= d
        b = jnp.where(keep, a * b_sh + b, b)
        a = jnp.where(keep, a * a_sh, a)
        d *= 2
    carry = hcar[...]
    tiles = []
    for i in range(n_tiles):
        tiles.append(a[i] * carry + b[i])
        carry = tiles[-1][SUBLANES - 1:SUBLANES, :]
    h = jnp.concatenate(tiles, axis=0)
    hcar[...] = carry
    o_ref[...] = (h * jax.nn.gelu(gate_ref[...].astype(F32))).astype(o_ref.dtype)


def _lru_branch(z, conv_w, conv_b, w_a, b_a, w_x, b_x, lam, *, tc):
    bsz, seq, _ = z.shape
    heads, blk, _ = w_a.shape
    width = heads * blk
    assert blk % LANES == 0 and seq % tc == 0
    row = lambda v: v.reshape(1, width)
    vec_spec = pl.BlockSpec((1, blk), lambda b, h, t: (0, h))
    mat_spec = pl.BlockSpec((None, blk, blk), lambda b, h, t: (h, 0, 0))
    return pl.pallas_call(
        functools.partial(_lru_kernel, tc=tc),
        grid=(bsz, heads, seq // tc),
        in_specs=[
            pl.BlockSpec((None, tc, blk), lambda b, h, t: (b, t, h)),
            pl.BlockSpec((None, tc, blk), lambda b, h, t: (b, t, heads + h)),
            pl.BlockSpec((CONV_WIDTH, blk), lambda b, h, t: (0, h)),
            vec_spec, mat_spec, vec_spec, mat_spec, vec_spec, vec_spec,
        ],
        out_specs=pl.BlockSpec((None, tc, blk), lambda b, h, t: (b, t, h)),
        out_shape=jax.ShapeDtypeStruct((bsz, seq, width), BF16),
        scratch_shapes=[pltpu.VMEM((tc + SUBLANES, blk), F32), pltpu.VMEM((1, blk), F32)],
        compiler_params=_cparams(("parallel", "parallel", "arbitrary")),
        name="rglru",
    )(z, z, conv_w, row(conv_b), w_a.astype(BF16), row(b_a), w_x.astype(BF16), row(b_x), row(lam))


def _rwkv_kernel(r_ref, k_ref, v_ref, lora_ref, mu_r, mu_k, mu_v, mu_l, w0_ref, w2_ref, a0_ref, a2_ref,
                 g2_ref, kk_ref, ka_ref, rk_ref, lnw_ref, lnb_ref, o_ref,
                 rbuf, kbuf, vbuf, lbuf, state, *, tc, pp, segs):
    L = RWKV_CHUNK
    P2 = 2 * L
    nc = tc // L
    assert P2 == LANES and RWKV_HEAD == L

    @pl.when(pl.program_id(2) == 0)
    def _():
        for buf in (rbuf, kbuf, vbuf, lbuf):
            buf[...] = jnp.zeros(buf.shape, F32)
        state[...] = jnp.zeros(state.shape, F32)

    def token_shift(ref, buf, mu_ref):
        cur = ref[...].astype(F32)
        n_tiles = tc // SUBLANES
        rolled = pltpu.roll(cur.reshape(n_tiles, SUBLANES, cur.shape[1]), 1, 1)
        before = jnp.concatenate([buf[...][None], rolled[:n_tiles - 1]], axis=0)
        sub = lax.broadcasted_iota(jnp.int32, rolled.shape, 1)
        prev = jnp.where(sub == 0, before, rolled).reshape(cur.shape)
        buf[...] = rolled[n_tiles - 1]
        return cur + (prev - cur) * mu_ref[...]

    r = token_shift(r_ref, rbuf, mu_r)
    k = token_shift(k_ref, kbuf, mu_k)
    v = token_shift(v_ref, vbuf, mu_v)
    lora = token_shift(lora_ref, lbuf, mu_l)

    lora_w, lora_a, lora_g = (lora[:, lo:hi] for lo, hi in segs)
    dw = _dot(jnp.tanh(lora_w).astype(BF16), w2_ref[...])
    logw = -DECAY_SCALE * _sigmoid(w0_ref[...] + dw)
    a = _sigmoid(a0_ref[...] + _dot(lora_a.astype(BF16), a2_ref[...]))
    g = _dot(_sigmoid(lora_g).astype(BF16), g2_ref[...])

    ri = lax.broadcasted_iota(jnp.int32, (P2, P2), 0)
    ci = lax.broadcasted_iota(jnp.int32, (P2, P2), 1)
    blk = jnp.where(ri >= L, 1, 0) == jnp.where(ci >= L, 1, 0)
    head_ones = jnp.where(blk, 1.0, 0.0).astype(BF16)

    def to_rows(x):
        return jnp.concatenate([x[:, j * LANES:(j + 1) * LANES] for j in range(pp)], axis=0)

    def from_rows(x):
        return jnp.concatenate([x[j * tc:(j + 1) * tc, :] for j in range(pp)], axis=1)

    def head_sum(x):
        return from_rows(_dot(to_rows(x).astype(BF16), head_ones))

    kk = k * kk_ref[...]
    kk = kk * lax.rsqrt(jnp.maximum(head_sum(kk * kk), 1e-24))
    kmod = k * (1.0 + (a - 1.0) * ka_ref[...])
    kka = kk * a
    bonus = head_sum(r * kmod * rk_ref[...]) * v

    rt = lax.broadcasted_iota(jnp.int32, (tc, tc), 0)
    ct = lax.broadcasted_iota(jnp.int32, (tc, tc), 1)
    shift = L.bit_length() - 1
    same_chunk = lax.shift_right_logical(rt, shift) == lax.shift_right_logical(ct, shift)
    tril_chunk = jnp.where(same_chunk & (ct <= rt), 1.0, 0.0).astype(BF16)
    logw_hi = logw.astype(BF16)
    logw_lo = (logw - logw_hi.astype(F32)).astype(BF16)
    cum = _dot(tril_chunk, logw_hi) + _dot(tril_chunk, logw_lo)
    p_inv = jnp.exp(-cum)
    r_d = r * jnp.exp(cum)
    kk_d = kk * jnp.exp(cum - logw)
    k_d = kmod * p_inv
    a_d = kka * p_inv
    p_last = [jnp.exp(cum[(c + 1) * L - 1:(c + 1) * L, :]) for c in range(nc)]
    p_tail = p_inv * jnp.concatenate([jnp.broadcast_to(p, (L, cum.shape[1])) for p in p_last], axis=0)
    k_t = kmod * p_tail
    a_t = kka * p_tail

    strict = (blk & (ci < ri))[None]
    incl = (blk & (ci <= ri))[None]
    eye = jnp.where(ri == ci, 1.0, 0.0).astype(F32)[None]
    lane_row = lax.broadcasted_iota(jnp.int32, (1, LANES), 1)
    m0 = jnp.where(lane_row < RWKV_HEAD, 1.0, 0.0).astype(F32)
    m1 = 1.0 - m0

    def batch(x, zero_other_head=True):
        out = []
        for c in range(nc):
            for j in range(pp):
                xs = x[c * L:(c + 1) * L, j * LANES:(j + 1) * LANES]
                out.append(jnp.concatenate([xs * m0, xs * m1] if zero_other_head else [xs, xs], axis=0))
        return jnp.stack(out)

    r_b, kk_b, v_b, kt_b, at_b = (batch(x) for x in (r_d, kk_d, v, k_t, a_t))
    k_b, a_b = batch(k_d, False), batch(a_d, False)
    kkr_b = jnp.concatenate([kk_b, r_b], axis=1)

    gram = _bdot(kkr_b, jnp.concatenate([k_b, a_b], axis=1), _BNT)
    a_kk_k = jnp.where(strict, gram[:, 0:P2, 0:P2], 0.0)
    n_mat = jnp.where(strict, gram[:, 0:P2, P2:], 0.0)
    a_r_k = jnp.where(incl, gram[:, P2:, 0:P2], 0.0)
    a_r_a = jnp.where(incl, gram[:, P2:, P2:], 0.0)

    inv = eye - n_mat
    pw = _bdot(n_mat, n_mat, _BNN)
    steps = 2
    while 2 * steps < L:
        both = _bdot(jnp.concatenate([inv, pw], axis=1), pw, _BNN)
        inv = inv + both[:, 0:P2]
        pw = both[:, P2:]
        steps *= 2
    inv = inv + _bdot(inv, pw, _BNN)
    av = _bdot(jnp.concatenate([a_kk_k, a_r_k], axis=1), v_b, _BNN)

    s = state[...]
    y_chunks = []
    for c in range(nc):
        sl = slice(c * pp, (c + 1) * pp)
        from_state = _bdot(kkr_b[sl], s, _BNT) + av[sl]
        sa = _bdot(inv[sl], from_state[:, 0:P2], _BNN)
        y_s = from_state[:, P2:] - _bdot(a_r_a[sl], sa, _BNN)
        y_c = y_s[:, 0:L, :] + y_s[:, L:, :]
        y_chunks.append(jnp.concatenate([y_c[j] for j in range(pp)], axis=1))
        decay = jnp.stack([p_last[c][:, j * LANES:(j + 1) * LANES] for j in range(pp)])
        s = s * decay + _bdot(jnp.concatenate([v_b[sl], sa], axis=1),
                              jnp.concatenate([kt_b[sl], -at_b[sl]], axis=1), _BTN)
    state[...] = s

    y = jnp.concatenate(y_chunks, axis=0)
    inv_n = 1.0 / RWKV_HEAD
    mean = head_sum(y) * inv_n
    yc = y - mean
    var = head_sum(yc * yc) * inv_n
    yn = yc * lax.rsqrt(var + GN_EPS) * lnw_ref[...] + lnb_ref[...]
    o_ref[...] = ((yn + bonus) * g).astype(o_ref.dtype)


def _rwkv_branch(z, col0, z_lora, mu, w0, w2, a0, a2, g2, k_k, k_a, r_k, ln_w, ln_b, *, tc, pp):
    bsz, seq, _ = z.shape
    width = w0.shape[0]
    bw = pp * LANES
    segs = _lane_segments((w2.shape[0], a2.shape[0], g2.shape[0]))
    lora_w = segs[-1][1]
    assert width % bw == 0 and col0 % bw == 0 and z_lora.shape[2] == lora_w
    assert seq % tc == 0 and tc % RWKV_CHUNK == 0 and tc % SUBLANES == 0
    groups = width // bw
    cb = col0 // bw

    def pad_rows(wm, seg):
        return jnp.pad(wm, ((0, seg[1] - seg[0] - wm.shape[0]), (0, 0))).astype(BF16)

    w2p, a2p, g2p = (pad_rows(wm, seg) for wm, seg in zip((w2, a2, g2), segs))
    mu_r, mu_k, mu_v = (mu[i * width:(i + 1) * width].reshape(1, width) for i in range(3))
    mu_l = _pad_segments(mu[3 * width:].reshape(1, -1), (w2.shape[0], a2.shape[0], g2.shape[0]), axis=1)
    row = lambda x: x.reshape(1, width)

    act = lambda off: pl.BlockSpec((None, tc, bw), lambda b, p, t: (b, t, cb + off * groups + p))
    vec = pl.BlockSpec((1, bw), lambda b, p, t: (0, p))
    lmat = lambda seg: pl.BlockSpec((seg[1] - seg[0], bw), lambda b, p, t: (0, p))
    return pl.pallas_call(
        functools.partial(_rwkv_kernel, tc=tc, pp=pp, segs=segs),
        grid=(bsz, groups, seq // tc),
        in_specs=[
            act(0), act(1), act(2),
            pl.BlockSpec((None, tc, lora_w), lambda b, p, t: (b, t, 0)),
            vec, vec, vec,
            pl.BlockSpec((1, lora_w), lambda b, p, t: (0, 0)),
            vec, lmat(segs[0]), vec, lmat(segs[1]), lmat(segs[2]), vec, vec, vec, vec, vec,
        ],
        out_specs=pl.BlockSpec((None, tc, bw), lambda b, p, t: (b, t, p)),
        out_shape=jax.ShapeDtypeStruct((bsz, seq, width), BF16),
        scratch_shapes=[
            pltpu.VMEM((SUBLANES, bw), F32), pltpu.VMEM((SUBLANES, bw), F32),
            pltpu.VMEM((SUBLANES, bw), F32), pltpu.VMEM((SUBLANES, lora_w), F32),
            pltpu.VMEM((pp, LANES, LANES), F32),
        ],
        compiler_params=_cparams(("parallel", "parallel", "arbitrary")),
        name="rwkv7",
    )(z, z, z, z_lora, mu_r, mu_k, mu_v, mu_l, row(w0), w2p, row(a0), a2p, g2p, row(k_k), row(k_a),
      row(r_k), row(ln_w), row(ln_b))


def _mlstm_kernel(q_ref, k_ref, v_ref, zo_ref, gcol_ref, grow_ref, bcol_ref, brow_ref, nw_ref, o_ref,
                  c_st, n_st, m_st, *, heads, dqk, dv):
    L = q_ref.shape[0]

    @pl.when(pl.program_id(1) == 0)
    def _():
        c_st[...] = jnp.zeros(c_st.shape, F32)
        n_st[...] = jnp.zeros(n_st.shape, F32)
        m_st[...] = jnp.zeros(m_st.shape, F32)

    ri = lax.broadcasted_iota(jnp.int32, (L, L), 0)
    ci = lax.broadcasted_iota(jnp.int32, (L, L), 1)
    causal = ci <= ri
    tril_ones = jnp.where(causal, 1.0, 0.0).astype(F32)
    triu_ones = jnp.where(ci >= ri, 1.0, 0.0).astype(F32)

    gc = gcol_ref[...] + bcol_ref[...]
    gr = grow_ref[...] + brow_ref[...]
    b_c = _dot(tril_ones, -_softplus(-gc), precision=HIGHEST)
    b_r = _dot(-_softplus(-gr), triu_ones, precision=HIGHEST)

    per_head = lambda f: jnp.stack([f(h) for h in range(heads)])
    q = per_head(lambda h: q_ref[:, h * dqk:(h + 1) * dqk])
    q = (q.astype(F32) * dqk ** -0.5).astype(BF16)
    k = per_head(lambda h: k_ref[:, h * dqk:(h + 1) * dqk])
    v = per_head(lambda h: v_ref[:, h * dv:(h + 1) * dv])
    bc = per_head(lambda h: b_c[:, heads + h:heads + h + 1])
    lic = per_head(lambda h: gc[:, h:h + 1])
    br = per_head(lambda h: b_r[heads + h:heads + h + 1, :])
    lir = per_head(lambda h: gr[h:h + 1, :])
    m_prev = m_st[...][:, 0:1, 0:1]
    c_mat = c_st[...]
    n_vec = n_st[...]

    log_d = jnp.where(causal[None], bc - br + lir, -jnp.inf)
    log_inter = bc + m_prev
    m_t = jnp.maximum(jnp.max(log_d, axis=-1, keepdims=True), log_inter)
    p = _bdot(q, k, _BNT) * jnp.exp(log_d - m_t)
    inter = jnp.exp(log_inter - m_t)
    num = _bdot(p, v, _BNN) + inter * _bdot(q, c_mat, _BNN)
    den = (jnp.sum(p, axis=-1, keepdims=True)
           + inter * jnp.sum(q.astype(F32) * n_vec, axis=-1, keepdims=True))
    inv_den = 1.0 / jnp.maximum(jnp.abs(den), jnp.exp(-m_t))

    b_last = bc[:, L - 1:L, :]
    log_g = b_last - bc + lic
    m_new = jnp.maximum(b_last + m_prev, jnp.max(log_g, axis=1, keepdims=True))
    carry = jnp.exp(b_last + m_prev - m_new)
    wk = k.astype(F32) * jnp.exp(log_g - m_new)
    c_st[...] = carry * c_mat + _bdot(wk, v, _BTN)
    n_st[...] = carry * n_vec + jnp.sum(wk, axis=1, keepdims=True)
    m_st[...] = jnp.broadcast_to(m_new, m_st.shape)

    msq = jnp.mean(num * num, axis=-1, keepdims=True)
    hs = num * (inv_den * lax.rsqrt(inv_den * inv_den * msq + NORM_EPS))
    for h in range(heads):
        sl = slice(h * dv, (h + 1) * dv)
        o_ref[:, sl] = (hs[h] * nw_ref[:, sl] * _sigmoid(zo_ref[:, sl].astype(F32))).astype(o_ref.dtype)


def _mlstm_branch(z, gates, b_i, b_f, norm_w, *, heads, qk, vdim):
    bsz, seq, _ = z.shape
    L = ML_CHUNK_ROWS
    dqk, dv = qk // heads, vdim // heads
    assert seq % L == 0 and vdim % qk == 0 and qk % LANES == 0 and (2 * qk) % vdim == 0
    vblk = 2 * qk // vdim
    bias = jnp.concatenate([b_i, b_f]).astype(F32)
    bias_col = jnp.zeros((1, LANES), F32).at[0, :2 * heads].set(bias)
    gates_row = jnp.swapaxes(gates[:, :, :2 * heads], 1, 2)
    return pl.pallas_call(
        functools.partial(_mlstm_kernel, heads=heads, dqk=dqk, dv=dv),
        grid=(bsz, seq // L),
        in_specs=[
            pl.BlockSpec((None, L, qk), lambda b, c: (b, c, 0)),
            pl.BlockSpec((None, L, qk), lambda b, c: (b, c, 1)),
            pl.BlockSpec((None, L, vdim), lambda b, c: (b, c, vblk)),
            pl.BlockSpec((None, L, vdim), lambda b, c: (b, c, vblk + 1)),
            pl.BlockSpec((None, L, LANES), lambda b, c: (b, c, 0)),
            pl.BlockSpec((None, 2 * heads, L), lambda b, c: (b, 0, c)),
            pl.BlockSpec((1, LANES), lambda b, c: (0, 0)),
            pl.BlockSpec((2 * heads, 1), lambda b, c: (0, 0)),
            pl.BlockSpec((1, vdim), lambda b, c: (0, 0)),
        ],
        out_specs=pl.BlockSpec((None, L, vdim), lambda b, c: (b, c, 0)),
        out_shape=jax.ShapeDtypeStruct((bsz, seq, vdim), BF16),
        scratch_shapes=[
            pltpu.VMEM((heads, dqk, dv), F32),
            pltpu.VMEM((heads, 1, dqk), F32),
            pltpu.VMEM((heads, SUBLANES, LANES), F32),
        ],
        compiler_params=_cparams(("parallel", "arbitrary")),
        name="mlstm",
    )(z, z, z, z, gates, gates_row, bias_col, bias.reshape(-1, 1), norm_w.reshape(1, vdim))


def _col_tile(n, cap):
    for step in (MXU_WIDTH, LANES):
        fits = [c for c in range(step, cap + 1, step) if n % c == 0]
        if fits:
            return fits[-1]
    raise ValueError(f"no lane-aligned column tile divides {n}")


def _lane_segments(widths):
    segs, lo = [], 0
    for w in widths:
        hi = lo + -(-w // LANES) * LANES
        segs.append((lo, hi))
        lo = hi
    return segs


def _pad_segments(x, widths, axis):
    assert x.shape[axis] == sum(widths)
    parts, start = [], 0
    for w, (lo, hi) in zip(widths, _lane_segments(widths)):
        part = lax.slice_in_dim(x, start, start + w, axis=axis)
        pad = [(0, 0)] * x.ndim
        pad[axis] = (0, hi - lo - w)
        parts.append(jnp.pad(part, pad))
        start += w
    return jnp.concatenate(parts, axis=axis)


def kernel(x, norm_mix, norm_mlp, norm_final, mlp_up, mlp_down, hy_in, lru_conv_w, lru_conv_b, lru_wa, lru_ba, lru_wx, lru_bx, lru_lam, rwkv_mu, rwkv_w0, rwkv_w2, rwkv_a0, rwkv_a2, rwkv_g2, rwkv_kk, rwkv_ka, rwkv_rk, rwkv_ln_w, rwkv_ln_b, hy_out, ml_in, ml_bi, ml_bf, ml_norm, ml_out):
    bsz, seq, d = x.shape
    t = bsz * seq
    depth = norm_mix.shape[0]
    xf = x.reshape(t, d)

    lru_w = lru_lam.shape[1]
    rw_w = rwkv_w0.shape[1]
    hy_main = 2 * lru_w + 3 * rw_w
    heads = ml_bi.shape[1]
    vdim = ml_norm.shape[1]
    qk = (ml_in.shape[2] - 2 * vdim - 2 * heads) // 2
    ml_main = 2 * qk + 2 * vdim

    hy_in_t, ml_in_t = jnp.swapaxes(hy_in, 1, 2), jnp.swapaxes(ml_in, 1, 2)
    lora_widths = (rwkv_w2.shape[1], rwkv_a2.shape[1], rwkv_g2.shape[1])
    hy_lora_t = _pad_segments(hy_in_t[:, hy_main:, :], lora_widths, axis=1)
    ml_gate_t = _pad_segments(ml_in_t[:, ml_main:, :], (2 * heads,), axis=1)
    tn = lambda n: _col_tile(n, COL_TILE)
    tk_out = min(K_TILE_OUT, lru_w, vdim)

    for layer in range(depth):
        hn = _norm_cast(xf, norm_mix[layer], BF16)
        if layer % 2 == 0:
            e = layer // 2
            z = _matmul([hn], hy_in_t, e, hy_main, tk=d, tn=tn(hy_main), out_dtype=BF16, w_t=True)
            z = z.reshape(bsz, seq, hy_main)
            n_lora = hy_lora_t.shape[1]
            z_lora = _matmul([hn], hy_lora_t, e, n_lora, tk=d, tn=tn(n_lora), w_t=True,
                             tm=ROW_TILE // 2, lhs_double_buffer=True)
            z_lora = z_lora.reshape(bsz, seq, n_lora)
            ya = _lru_branch(z, lru_conv_w[e], lru_conv_b[e], lru_wa[e], lru_ba[e], lru_wx[e], lru_bx[e],
                             lru_lam[e], tc=min(LRU_ROWS, seq))
            yb = _rwkv_branch(z, 2 * lru_w, z_lora, rwkv_mu[e], rwkv_w0[e], rwkv_w2[e], rwkv_a0[e], rwkv_a2[e],
                              rwkv_g2[e], rwkv_kk[e], rwkv_ka[e], rwkv_rk[e], rwkv_ln_w[e], rwkv_ln_b[e],
                              tc=min(RWKV_ROWS, seq), pp=min(RWKV_PAIRS, rw_w // LANES))
            lhs = [ya.reshape(t, lru_w), yb.reshape(t, rw_w)]
            assert lru_w == rw_w
            xf = _matmul(lhs, hy_out, e, d, tk=tk_out, tn=tn(d), res=xf)
        else:
            o = layer // 2
            z = _matmul([hn], ml_in_t, o, ml_main, tk=d, tn=tn(ml_main), out_dtype=BF16, w_t=True)
            z = z.reshape(bsz, seq, ml_main)
            n_gate = ml_gate_t.shape[1]
            gates = _matmul([hn], ml_gate_t, o, n_gate, tk=d, tn=tn(n_gate), w_t=True,
                            tm=ROW_TILE // 2, lhs_double_buffer=True)
            gates = gates.reshape(bsz, seq, n_gate)
            hs = _mlstm_branch(z, gates, ml_bi[o], ml_bf[o], ml_norm[o], heads=heads, qk=qk, vdim=vdim)
            xf = _matmul([hs.reshape(t, vdim)], ml_out, o, d, tk=vdim, tn=tn(d), res=xf)
        hn = _norm_cast(xf, norm_mlp[layer], BF16)
        u = _matmul([hn], mlp_up, layer, mlp_up.shape[2], tk=d, tn=tn(mlp_up.shape[2]), out_dtype=BF16,
                    relu2=True)
        xf = _matmul([u], mlp_down, layer, d, tk=min(K_TILE_DOWN, mlp_down.shape[1]),
                     tn=_col_tile(d, DOWN_COL_TILE), res=xf)
    return _norm_cast(xf, norm_final, F32).reshape(bsz, seq, d)
```

```python
import functools

import jax
import jax.numpy as jnp
from jax import lax
from jax.experimental import pallas as pl
from jax.experimental.pallas import tpu as pltpu

F32 = jnp.float32
BF16 = jnp.bfloat16
HIGHEST = lax.Precision.HIGHEST

NORM_EPS = 1e-6
GN_EPS = 64e-5
LRU_C = 8.0
CONV_WIDTH = 4
RWKV_HEAD = 64
DECAY_SCALE = 0.6065306597126334
LANES = 128
SUBLANES = 8
MXU_WIDTH = 256
VMEM_LIMIT_BYTES = 60 * 1024 * 1024
RWKV_CHUNK = 64
RWKV_ROWS = 128
RWKV_PAIRS = 16
LRU_ROWS = 2048
ML_CHUNK_ROWS = 256
NORM_ROWS = 512
ROW_TILE = 2048
COL_TILE = 512
K_TILE_OUT = 2048
DOWN_COL_TILE = 1024
K_TILE_DOWN = 1024


def _cparams(semantics):
    return pltpu.CompilerParams(dimension_semantics=semantics, vmem_limit_bytes=VMEM_LIMIT_BYTES)


def _softplus(x):
    return jnp.maximum(x, 0.0) + jnp.log1p(jnp.exp(-jnp.abs(x)))


def _sigmoid(x):
    return 1.0 / (1.0 + jnp.exp(-x))


_NN = (((1,), (0,)), ((), ()))
_NT = (((1,), (1,)), ((), ()))
_TN = (((0,), (0,)), ((), ()))
_BNN = (((2,), (1,)), ((0,), (0,)))
_BNT = (((2,), (2,)), ((0,), (0,)))
_BTN = (((1,), (1,)), ((0,), (0,)))


def _dot(a, b, dims=_NN, precision=None):
    return lax.dot_general(a, b, dims, precision=precision, preferred_element_type=F32)


def _bdot(a, b, dims=_NN):
    return lax.dot_general(a.astype(BF16), b.astype(BF16), dims, preferred_element_type=F32)


def _norm_cast_kernel(x_ref, g_ref, o_ref):
    x = x_ref[...]
    ms = jnp.mean(x * x, axis=-1, keepdims=True)
    o_ref[...] = (x * lax.rsqrt(ms + NORM_EPS) * g_ref[...]).astype(o_ref.dtype)


def _norm_cast(x, g, out_dtype):
    t, d = x.shape
    tm = min(NORM_ROWS, t)
    return pl.pallas_call(
        _norm_cast_kernel,
        grid=(t // tm,),
        in_specs=[pl.BlockSpec((tm, d), lambda i: (i, 0)), pl.BlockSpec((1, d), lambda i: (0, 0))],
        out_specs=pl.BlockSpec((tm, d), lambda i: (i, 0)),
        out_shape=jax.ShapeDtypeStruct((t, d), out_dtype),
        compiler_params=_cparams(("parallel",)),
        name="rmsnorm",
    )(x, g.reshape(1, d))


def _matmul_kernel(*refs, n_lhs, nk, has_res, relu2, w_t):
    a_refs, w_ref = refs[:n_lhs], refs[n_lhs]
    r_ref = refs[n_lhs + 1] if has_res else None
    o_ref = refs[n_lhs + 1 + has_res]
    acc_ref = refs[-1] if (not has_res and nk > 1) else None
    kk = pl.program_id(2)

    def finish(acc):
        if relu2:
            acc = jnp.maximum(acc, 0.0)
            acc = acc * acc
        return acc.astype(o_ref.dtype)

    def accumulate(a_ref):
        def part():
            return lax.dot_general(a_ref[...], w_ref[...].astype(BF16), _NT if w_t else _NN,
                                   preferred_element_type=F32)

        if has_res:
            @pl.when(kk == 0)
            def _():
                o_ref[...] = r_ref[...] + part()

            @pl.when(kk > 0)
            def _():
                o_ref[...] += part()
        elif nk == 1:
            o_ref[...] = finish(part())
        else:
            @pl.when(kk == 0)
            def _():
                acc_ref[...] = part()

            @pl.when((kk > 0) & (kk < nk - 1))
            def _():
                acc_ref[...] += part()

            @pl.when(kk == nk - 1)
            def _():
                o_ref[...] = finish(acc_ref[...] + part())

    if n_lhs == 1:
        accumulate(a_refs[0])
    else:
        per = nk // n_lhs
        for idx, a_ref in enumerate(a_refs):
            pl.when((kk >= idx * per) & (kk < (idx + 1) * per))(functools.partial(accumulate, a_ref))


def _matmul(lhs, w, layer, n, *, tk, tn, tm=ROW_TILE, out_dtype=F32, res=None, relu2=False, w_t=False,
            lhs_double_buffer=False):
    t = lhs[0].shape[0]
    k = w.shape[2 if w_t else 1]
    tm = min(tm, t)
    nk = k // tk
    assert t % tm == 0 and k % tk == 0 and n % tn == 0
    per = nk // len(lhs)
    assert all(a.shape[1] == per * tk for a in lhs) and per * len(lhs) == nk
    once = dict(pipeline_mode=pl.Buffered(1)) if (per == 1 and not lhs_double_buffer) else {}

    def lhs_spec(idx):
        return pl.BlockSpec((tm, tk), lambda i, j, kk: (i, jnp.clip(kk - idx * per, 0, per - 1)), **once)

    a_specs = [lhs_spec(idx) for idx in range(len(lhs))]
    has_res = res is not None
    assert not (has_res and (relu2 or out_dtype != F32))
    tile = pl.BlockSpec((tm, tn), lambda i, j, kk: (i, j))
    scratch = [pltpu.VMEM((tm, tn), F32)] if (not has_res and nk > 1) else []
    if w_t:
        w_spec = pl.BlockSpec((None, tn, tk), lambda i, j, kk: (layer, j, kk))
    else:
        w_spec = pl.BlockSpec((None, tk, tn), lambda i, j, kk: (layer, kk, j))
    return pl.pallas_call(
        functools.partial(_matmul_kernel, n_lhs=len(lhs), nk=nk, has_res=has_res, relu2=relu2, w_t=w_t),
        grid=(t // tm, n // tn, nk),
        in_specs=a_specs + [w_spec] + ([tile] if has_res else []),
        out_specs=tile,
        out_shape=jax.ShapeDtypeStruct((t, n), out_dtype),
        scratch_shapes=scratch,
        compiler_params=_cparams(("parallel", "parallel", "arbitrary")),
        name="matmul",
    )(*lhs, w, *([res] if has_res else []))


def _lru_kernel(u_ref, gate_ref, cw_ref, cb_ref, wa_ref, ba_ref, wx_ref, bx_ref, lam_ref,
                o_ref, xbuf, hcar, *, tc):
    @pl.when(pl.program_id(2) == 0)
    def _():
        xbuf[0:SUBLANES, :] = jnp.zeros((SUBLANES, xbuf.shape[1]), F32)
        hcar[...] = jnp.zeros(hcar.shape, F32)

    u = u_ref[...].astype(F32)
    xbuf[SUBLANES:, :] = u
    cw = cw_ref[...]
    conv = cb_ref[...] + cw[CONV_WIDTH - 1:CONV_WIDTH] * u
    for j in range(CONV_WIDTH - 1):
        back = CONV_WIDTH - 1 - j
        conv = conv + cw[j:j + 1] * xbuf[SUBLANES - back:SUBLANES - back + tc, :]
    xbuf[0:SUBLANES, :] = u[tc - SUBLANES:, :]

    cbf = conv.astype(BF16)
    r = _sigmoid(jnp.dot(cbf, wa_ref[...], preferred_element_type=F32) + ba_ref[...])
    i = _sigmoid(jnp.dot(cbf, wx_ref[...], preferred_element_type=F32) + bx_ref[...])
    log_a = (-LRU_C) * r * _softplus(-lam_ref[...])
    a = jnp.exp(log_a)
    b = jnp.sqrt(-jnp.tanh(log_a) * (a * a + 1.0)) * (i * conv)

    n_tiles = tc // SUBLANES
    a = a.reshape(n_tiles, SUBLANES, a.shape[1])
    b = b.reshape(n_tiles, SUBLANES, b.shape[1])
    sub = lax.broadcasted_iota(jnp.int32, a.shape, 1)
    d = 1
    while d < SUBLANES:
        a_sh = pltpu.roll(a, d, 1)
        b_sh = pltpu.roll(b, d, 1)
        keep = sub >= d
        b = jnp.where(keep, a * b_sh + b, b)
        a = jnp.where(keep, a * a_sh, a)
        d *= 2
    carry = hcar[...]
    tiles = []
    for i in range(n_tiles):
        tiles.append(a[i] * carry + b[i])
        carry = tiles[-1][SUBLANES - 1:SUBLANES, :]
    h = jnp.concatenate(tiles, axis=0)
    hcar[...] = carry
    o_ref[...] = (h * jax.nn.gelu(gate_ref[...].astype(F32))).astype(o_ref.dtype)


def _lru_branch(z, conv_w, conv_b, w_a, b_a, w_x, b_x, lam, *, tc):
    bsz, seq, _ = z.shape
    heads, blk, _ = w_a.shape
    width = heads * blk
    assert blk % LANES == 0 and seq % tc == 0
    row = lambda v: v.reshape(1, width)
    vec_spec = pl.BlockSpec((1, blk), lambda b, h, t: (0, h))
    mat_spec = pl.BlockSpec((None, blk, blk), lambda b, h, t: (h, 0, 0))
    return pl.pallas_call(
        functools.partial(_lru_kernel, tc=tc),
        grid=(bsz, heads, seq // tc),
        in_specs=[
            pl.BlockSpec((None, tc, blk), lambda b, h, t: (b, t, h)),
            pl.BlockSpec((None, tc, blk), lambda b, h, t: (b, t, heads + h)),
            pl.BlockSpec((CONV_WIDTH, blk), lambda b, h, t: (0, h)),
            vec_spec, mat_spec, vec_spec, mat_spec, vec_spec, vec_spec,
        ],
        out_specs=pl.BlockSpec((None, tc, blk), lambda b, h, t: (b, t, h)),
        out_shape=jax.ShapeDtypeStruct((bsz, seq, width), BF16),
        scratch_shapes=[pltpu.VMEM((tc + SUBLANES, blk), F32), pltpu.VMEM((1, blk), F32)],
        compiler_params=_cparams(("parallel", "parallel", "arbitrary")),
        name="rglru",
    )(z, z, conv_w, row(conv_b), w_a.astype(BF16), row(b_a), w_x.astype(BF16), row(b_x), row(lam))


def _rwkv_kernel(r_ref, k_ref, v_ref, lora_ref, mu_r, mu_k, mu_v, mu_l, w0_ref, w2_ref, a0_ref, a2_ref,
                 g2_ref, kk_ref, ka_ref, rk_ref, lnw_ref, lnb_ref, o_ref,
                 rbuf, kbuf, vbuf, lbuf, state, *, tc, pp, segs):
    L = RWKV_CHUNK
    P2 = 2 * L
    nc = tc // L
    assert P2 == LANES and RWKV_HEAD == L

    @pl.when(pl.program_id(2) == 0)
    def _():
        for buf in (rbuf, kbuf, vbuf, lbuf):
            buf[...] = jnp.zeros(buf.shape, F32)
        state[...] = jnp.zeros(state.shape, F32)

    def token_shift(ref, buf, mu_ref):
        cur = ref[...].astype(F32)
        n_tiles = tc // SUBLANES
        rolled = pltpu.roll(cur.reshape(n_tiles, SUBLANES, cur.shape[1]), 1, 1)
        before = jnp.concatenate([buf[...][None], rolled[:n_tiles - 1]], axis=0)
        sub = lax.broadcasted_iota(jnp.int32, rolled.shape, 1)
        prev = jnp.where(sub == 0, before, rolled).reshape(cur.shape)
        buf[...] = rolled[n_tiles - 1]
        return cur + (prev - cur) * mu_ref[...]

    r = token_shift(r_ref, rbuf, mu_r)
    k = token_shift(k_ref, kbuf, mu_k)
    v = token_shift(v_ref, vbuf, mu_v)
    lora = token_shift(lora_ref, lbuf, mu_l)

    lora_w, lora_a, lora_g = (lora[:, lo:hi] for lo, hi in segs)
    dw = _dot(jnp.tanh(lora_w).astype(BF16), w2_ref[...])
    logw = -DECAY_SCALE * _sigmoid(w0_ref[...] + dw)
    a = _sigmoid(a0_ref[...] + _dot(lora_a.astype(BF16), a2_ref[...]))
    g = _dot(_sigmoid(lora_g).astype(BF16), g2_ref[...])

    ri = lax.broadcasted_iota(jnp.int32, (P2, P2), 0)
    ci = lax.broadcasted_iota(jnp.int32, (P2, P2), 1)
    blk = jnp.where(ri >= L, 1, 0) == jnp.where(ci >= L, 1, 0)
    head_ones = jnp.where(blk, 1.0, 0.0).astype(BF16)

    def to_rows(x):
        return jnp.concatenate([x[:, j * LANES:(j + 1) * LANES] for j in range(pp)], axis=0)

    def from_rows(x):
        return jnp.concatenate([x[j * tc:(j + 1) * tc, :] for j in range(pp)], axis=1)

    def head_sum(x):
        return from_rows(_dot(to_rows(x).astype(BF16), head_ones))

    kk = k * kk_ref[...]
    kk = kk * lax.rsqrt(jnp.maximum(head_sum(kk * kk), 1e-24))
    kmod = k * (1.0 + (a - 1.0) * ka_ref[...])
    kka = kk * a
    bonus = head_sum(r * kmod * rk_ref[...]) * v

    rt = lax.broadcasted_iota(jnp.int32, (tc, tc), 0)
    ct = lax.broadcasted_iota(jnp.int32, (tc, tc), 1)
    shift = L.bit_length() - 1
    same_chunk = lax.shift_right_logical(rt, shift) == lax.shift_right_logical(ct, shift)
    tril_chunk = jnp.where(same_chunk & (ct <= rt), 1.0, 0.0).astype(BF16)
    logw_hi = logw.astype(BF16)
    logw_lo = (logw - logw_hi.astype(F32)).astype(BF16)
    cum = _dot(tril_chunk, logw_hi) + _dot(tril_chunk, logw_lo)
    p_inv = jnp.exp(-cum)
    r_d = r * jnp.exp(cum)
    kk_d = kk * jnp.exp(cum - logw)
    k_d = kmod * p_inv
    a_d = kka * p_inv
    p_last = [jnp.exp(cum[(c + 1) * L - 1:(c + 1) * L, :]) for c in range(nc)]
    p_tail = p_inv * jnp.concatenate([jnp.broadcast_to(p, (L, cum.shape[1])) for p in p_last], axis=0)
    k_t = kmod * p_tail
    a_t = kka * p_tail

    strict = (blk & (ci < ri))[None]
    incl = (blk & (ci <= ri))[None]
    eye = jnp.where(ri == ci, 1.0, 0.0).astype(F32)[None]
    lane_row = lax.broadcasted_iota(jnp.int32, (1, LANES), 1)
    m0 = jnp.where(lane_row < RWKV_HEAD, 1.0, 0.0).astype(BF16)
    m1 = jnp.where(lane_row < RWKV_HEAD, 0.0, 1.0).astype(BF16)

    def batch(x, zero_other_head=True):
        x = x.astype(BF16)
        out = []
        for c in range(nc):
            for j in range(pp):
                xs = x[c * L:(c + 1) * L, j * LANES:(j + 1) * LANES]
                out.append(jnp.concatenate([xs * m0, xs * m1] if zero_other_head else [xs, xs], axis=0))
        return jnp.stack(out)

    r_b, kk_b, v_b, kt_b, at_b = (batch(x) for x in (r_d, kk_d, v, k_t, a_t))
    k_b, a_b = batch(k_d, False), batch(a_d, False)
    kkr_b = jnp.concatenate([kk_b, r_b], axis=1)

    gram = _bdot(kkr_b, jnp.concatenate([k_b, a_b], axis=1), _BNT)
    a_kk_k = jnp.where(strict, gram[:, 0:P2, 0:P2], 0.0)
    n_mat = jnp.where(strict, gram[:, 0:P2, P2:], 0.0)
    a_r_k = jnp.where(incl, gram[:, P2:, 0:P2], 0.0)
    a_r_a = jnp.where(incl, gram[:, P2:, P2:], 0.0)

    inv = eye - n_mat
    pw = _bdot(n_mat, n_mat, _BNN)
    steps = 2
    while 2 * steps < L:
        both = _bdot(jnp.concatenate([inv, pw], axis=1), pw, _BNN)
        inv = inv + both[:, 0:P2]
        pw = both[:, P2:]
        steps *= 2
    inv = inv + _bdot(inv, pw, _BNN)
    av = _bdot(jnp.concatenate([a_kk_k, a_r_k], axis=1), v_b, _BNN)

    s = state[...]
    y_chunks = []
    for c in range(nc):
        sl = slice(c * pp, (c + 1) * pp)
        from_state = _bdot(kkr_b[sl], s, _BNT) + av[sl]
        sa = _bdot(inv[sl], from_state[:, 0:P2], _BNN)
        y_s = from_state[:, P2:] - _bdot(a_r_a[sl], sa, _BNN)
        y_c = y_s[:, 0:L, :] + y_s[:, L:, :]
        y_chunks.append(jnp.concatenate([y_c[j] for j in range(pp)], axis=1))
        decay = jnp.stack([p_last[c][:, j * LANES:(j + 1) * LANES] for j in range(pp)])
        s = s * decay + _bdot(jnp.concatenate([v_b[sl], sa], axis=1),
                              jnp.concatenate([kt_b[sl], -at_b[sl]], axis=1), _BTN)
    state[...] = s

    y = jnp.concatenate(y_chunks, axis=0)
    inv_n = 1.0 / RWKV_HEAD
    mean = head_sum(y) * inv_n
    yc = y - mean
    var = head_sum(yc * yc) * inv_n
    yn = yc * lax.rsqrt(var + GN_EPS) * lnw_ref[...] + lnb_ref[...]
    o_ref[...] = ((yn + bonus) * g).astype(o_ref.dtype)


def _rwkv_branch(z, col0, z_lora, mu, w0, w2, a0, a2, g2, k_k, k_a, r_k, ln_w, ln_b, *, tc, pp):
    bsz, seq, _ = z.shape
    width = w0.shape[0]
    bw = pp * LANES
    segs = _lane_segments((w2.shape[0], a2.shape[0], g2.shape[0]))
    lora_w = segs[-1][1]
    assert width % bw == 0 and col0 % bw == 0 and z_lora.shape[2] == lora_w
    assert seq % tc == 0 and tc % RWKV_CHUNK == 0 and tc % SUBLANES == 0
    groups = width // bw
    cb = col0 // bw

    def pad_rows(wm, seg):
        return jnp.pad(wm, ((0, seg[1] - seg[0] - wm.shape[0]), (0, 0))).astype(BF16)

    w2p, a2p, g2p = (pad_rows(wm, seg) for wm, seg in zip((w2, a2, g2), segs))
    mu_r, mu_k, mu_v = (mu[i * width:(i + 1) * width].reshape(1, width) for i in range(3))
    mu_l = _pad_segments(mu[3 * width:].reshape(1, -1), (w2.shape[0], a2.shape[0], g2.shape[0]), axis=1)
    row = lambda x: x.reshape(1, width)

    act = lambda off: pl.BlockSpec((None, tc, bw), lambda b, p, t: (b, t, cb + off * groups + p))
    vec = pl.BlockSpec((1, bw), lambda b, p, t: (0, p))
    lmat = lambda seg: pl.BlockSpec((seg[1] - seg[0], bw), lambda b, p, t: (0, p))
    return pl.pallas_call(
        functools.partial(_rwkv_kernel, tc=tc, pp=pp, segs=segs),
        grid=(bsz, groups, seq // tc),
        in_specs=[
            act(0), act(1), act(2),
            pl.BlockSpec((None, tc, lora_w), lambda b, p, t: (b, t, 0)),
            vec, vec, vec,
            pl.BlockSpec((1, lora_w), lambda b, p, t: (0, 0)),
            vec, lmat(segs[0]), vec, lmat(segs[1]), lmat(segs[2]), vec, vec, vec, vec, vec,
        ],
        out_specs=pl.BlockSpec((None, tc, bw), lambda b, p, t: (b, t, p)),
        out_shape=jax.ShapeDtypeStruct((bsz, seq, width), BF16),
        scratch_shapes=[
            pltpu.VMEM((SUBLANES, bw), F32), pltpu.VMEM((SUBLANES, bw), F32),
            pltpu.VMEM((SUBLANES, bw), F32), pltpu.VMEM((SUBLANES, lora_w), F32),
            pltpu.VMEM((pp, LANES, LANES), F32),
        ],
        compiler_params=_cparams(("parallel", "parallel", "arbitrary")),
        name="rwkv7",
    )(z, z, z, z_lora, mu_r, mu_k, mu_v, mu_l, row(w0), w2p, row(a0), a2p, g2p, row(k_k), row(k_a),
      row(r_k), row(ln_w), row(ln_b))


def _mlstm_kernel(q_ref, k_ref, v_ref, zo_ref, gcol_ref, grow_ref, bcol_ref, brow_ref, nw_ref, o_ref,
                  c_st, n_st, m_st, *, heads, dqk, dv):
    L = q_ref.shape[0]

    @pl.when(pl.program_id(1) == 0)
    def _():
        c_st[...] = jnp.zeros(c_st.shape, F32)
        n_st[...] = jnp.zeros(n_st.shape, F32)
        m_st[...] = jnp.zeros(m_st.shape, F32)

    ri = lax.broadcasted_iota(jnp.int32, (L, L), 0)
    ci = lax.broadcasted_iota(jnp.int32, (L, L), 1)
    causal = ci <= ri
    tril_ones = jnp.where(causal, 1.0, 0.0).astype(F32)
    triu_ones = jnp.where(ci >= ri, 1.0, 0.0).astype(F32)

    gc = gcol_ref[...] + bcol_ref[...]
    gr = grow_ref[...] + brow_ref[...]
    b_c = _dot(tril_ones, -_softplus(-gc), precision=HIGHEST)
    b_r = _dot(-_softplus(-gr), triu_ones, precision=HIGHEST)

    per_head = lambda f: jnp.stack([f(h) for h in range(heads)])
    q = per_head(lambda h: q_ref[:, h * dqk:(h + 1) * dqk])
    q = (q.astype(F32) * dqk ** -0.5).astype(BF16)
    k = per_head(lambda h: k_ref[:, h * dqk:(h + 1) * dqk])
    v = per_head(lambda h: v_ref[:, h * dv:(h + 1) * dv])
    bc = per_head(lambda h: b_c[:, heads + h:heads + h + 1])
    lic = per_head(lambda h: gc[:, h:h + 1])
    br = per_head(lambda h: b_r[heads + h:heads + h + 1, :])
    lir = per_head(lambda h: gr[h:h + 1, :])
    m_prev = m_st[...][:, 0:1, 0:1]
    c_mat = c_st[...]
    n_vec = n_st[...]

    log_d = jnp.where(causal[None], bc - br + lir, -jnp.inf)
    log_inter = bc + m_prev
    m_t = jnp.maximum(jnp.max(log_d, axis=-1, keepdims=True), log_inter)
    p = _bdot(q, k, _BNT) * jnp.exp(log_d - m_t)
    inter = jnp.exp(log_inter - m_t)
    num = _bdot(p, v, _BNN) + inter * _bdot(q, c_mat, _BNN)
    den = (jnp.sum(p, axis=-1, keepdims=True)
           + inter * jnp.sum(q.astype(F32) * n_vec, axis=-1, keepdims=True))
    inv_den = 1.0 / jnp.maximum(jnp.abs(den), jnp.exp(-m_t))

    b_last = bc[:, L - 1:L, :]
    log_g = b_last - bc + lic
    m_new = jnp.maximum(b_last + m_prev, jnp.max(log_g, axis=1, keepdims=True))
    carry = jnp.exp(b_last + m_prev - m_new)
    wk = k.astype(F32) * jnp.exp(log_g - m_new)
    c_st[...] = carry * c_mat + _bdot(wk, v, _BTN)
    n_st[...] = carry * n_vec + jnp.sum(wk, axis=1, keepdims=True)
    m_st[...] = jnp.broadcast_to(m_new, m_st.shape)

    msq = jnp.mean(num * num, axis=-1, keepdims=True)
    hs = num * (inv_den * lax.rsqrt(inv_den * inv_den * msq + NORM_EPS))
    for h in range(heads):
        sl = slice(h * dv, (h + 1) * dv)
        o_ref[:, sl] = (hs[h] * nw_ref[:, sl] * _sigmoid(zo_ref[:, sl].astype(F32))).astype(o_ref.dtype)


def _mlstm_branch(z, gates, b_i, b_f, norm_w, *, heads, qk, vdim):
    bsz, seq, _ = z.shape
    L = ML_CHUNK_ROWS
    dqk, dv = qk // heads, vdim // heads
    assert seq % L == 0 and vdim % qk == 0 and qk % LANES == 0 and (2 * qk) % vdim == 0
    vblk = 2 * qk // vdim
    bias = jnp.concatenate([b_i, b_f]).astype(F32)
    bias_col = jnp.zeros((1, LANES), F32).at[0, :2 * heads].set(bias)
    gates_row = jnp.swapaxes(gates[:, :, :2 * heads], 1, 2)
    return pl.pallas_call(
        functools.partial(_mlstm_kernel, heads=heads, dqk=dqk, dv=dv),
        grid=(bsz, seq // L),
        in_specs=[
            pl.BlockSpec((None, L, qk), lambda b, c: (b, c, 0)),
            pl.BlockSpec((None, L, qk), lambda b, c: (b, c, 1)),
            pl.BlockSpec((None, L, vdim), lambda b, c: (b, c, vblk)),
            pl.BlockSpec((None, L, vdim), lambda b, c: (b, c, vblk + 1)),
            pl.BlockSpec((None, L, LANES), lambda b, c: (b, c, 0)),
            pl.BlockSpec((None, 2 * heads, L), lambda b, c: (b, 0, c)),
            pl.BlockSpec((1, LANES), lambda b, c: (0, 0)),
            pl.BlockSpec((2 * heads, 1), lambda b, c: (0, 0)),
            pl.BlockSpec((1, vdim), lambda b, c: (0, 0)),
        ],
        out_specs=pl.BlockSpec((None, L, vdim), lambda b, c: (b, c, 0)),
        out_shape=jax.ShapeDtypeStruct((bsz, seq, vdim), BF16),
        scratch_shapes=[
            pltpu.VMEM((heads, dqk, dv), F32),
            pltpu.VMEM((heads, 1, dqk), F32),
            pltpu.VMEM((heads, SUBLANES, LANES), F32),
        ],
        compiler_params=_cparams(("parallel", "arbitrary")),
        name="mlstm",
    )(z, z, z, z, gates, gates_row, bias_col, bias.reshape(-1, 1), norm_w.reshape(1, vdim))


def _col_tile(n, cap):
    for step in (MXU_WIDTH, LANES):
        fits = [c for c in range(step, cap + 1, step) if n % c == 0]
        if fits:
            return fits[-1]
    raise ValueError(f"no lane-aligned column tile divides {n}")


def _lane_segments(widths):
    segs, lo = [], 0
    for w in widths:
        hi = lo + -(-w // LANES) * LANES
        segs.append((lo, hi))
        lo = hi
    return segs


def _pad_segments(x, widths, axis):
    assert x.shape[axis] == sum(widths)
    parts, start = [], 0
    for w, (lo, hi) in zip(widths, _lane_segments(widths)):
        part = lax.slice_in_dim(x, start, start + w, axis=axis)
        pad = [(0, 0)] * x.ndim
        pad[axis] = (0, hi - lo - w)
        parts.append(jnp.pad(part, pad))
        start += w
    return jnp.concatenate(parts, axis=axis)


def kernel(x, norm_mix, norm_mlp, norm_final, mlp_up, mlp_down, hy_in, lru_conv_w, lru_conv_b, lru_wa, lru_ba, lru_wx, lru_bx, lru_lam, rwkv_mu, rwkv_w0, rwkv_w2, rwkv_a0, rwkv_a2, rwkv_g2, rwkv_kk, rwkv_ka, rwkv_rk, rwkv_ln_w, rwkv_ln_b, hy_out, ml_in, ml_bi, ml_bf, ml_norm, ml_out):
    bsz, seq, d = x.shape
    t = bsz * seq
    depth = norm_mix.shape[0]
    xf = x.reshape(t, d)

    lru_w = lru_lam.shape[1]
    rw_w = rwkv_w0.shape[1]
    hy_main = 2 * lru_w + 3 * rw_w
    heads = ml_bi.shape[1]
    vdim = ml_norm.shape[1]
    qk = (ml_in.shape[2] - 2 * vdim - 2 * heads) // 2
    ml_main = 2 * qk + 2 * vdim

    hy_in_t, ml_in_t = jnp.swapaxes(hy_in, 1, 2), jnp.swapaxes(ml_in, 1, 2)
    lora_widths = (rwkv_w2.shape[1], rwkv_a2.shape[1], rwkv_g2.shape[1])
    hy_lora_t = _pad_segments(hy_in_t[:, hy_main:, :], lora_widths, axis=1)
    ml_gate_t = _pad_segments(ml_in_t[:, ml_main:, :], (2 * heads,), axis=1)
    tn = lambda n: _col_tile(n, COL_TILE)
    tk_out = min(K_TILE_OUT, lru_w, vdim)

    for layer in range(depth):
        hn = _norm_cast(xf, norm_mix[layer], BF16)
        if layer % 2 == 0:
            e = layer // 2
            z = _matmul([hn], hy_in_t, e, hy_main, tk=d, tn=tn(hy_main), out_dtype=BF16, w_t=True)
            z = z.reshape(bsz, seq, hy_main)
            n_lora = hy_lora_t.shape[1]
            z_lora = _matmul([hn], hy_lora_t, e, n_lora, tk=d, tn=tn(n_lora), w_t=True,
                             tm=ROW_TILE // 2, lhs_double_buffer=True)
            z_lora = z_lora.reshape(bsz, seq, n_lora)
            ya = _lru_branch(z, lru_conv_w[e], lru_conv_b[e], lru_wa[e], lru_ba[e], lru_wx[e], lru_bx[e],
                             lru_lam[e], tc=min(LRU_ROWS, seq))
            yb = _rwkv_branch(z, 2 * lru_w, z_lora, rwkv_mu[e], rwkv_w0[e], rwkv_w2[e], rwkv_a0[e], rwkv_a2[e],
                              rwkv_g2[e], rwkv_kk[e], rwkv_ka[e], rwkv_rk[e], rwkv_ln_w[e], rwkv_ln_b[e],
                              tc=min(RWKV_ROWS, seq), pp=min(RWKV_PAIRS, rw_w // LANES))
            lhs = [ya.reshape(t, lru_w), yb.reshape(t, rw_w)]
            assert lru_w == rw_w
            xf = _matmul(lhs, hy_out, e, d, tk=tk_out, tn=tn(d), res=xf)
        else:
            o = layer // 2
            z = _matmul([hn], ml_in_t, o, ml_main, tk=d, tn=tn(ml_main), out_dtype=BF16, w_t=True)
            z = z.reshape(bsz, seq, ml_main)
            n_gate = ml_gate_t.shape[1]
            gates = _matmul([hn], ml_gate_t, o, n_gate, tk=d, tn=tn(n_gate), w_t=True,
                            tm=ROW_TILE // 2, lhs_double_buffer=True)
            gates = gates.reshape(bsz, seq, n_gate)
            hs = _mlstm_branch(z, gates, ml_bi[o], ml_bf[o], ml_norm[o], heads=heads, qk=qk, vdim=vdim)
            xf = _matmul([hs.reshape(t, vdim)], ml_out, o, d, tk=vdim, tn=tn(d), res=xf)
        hn = _norm_cast(xf, norm_mlp[layer], BF16)
        u = _matmul([hn], mlp_up, layer, mlp_up.shape[2], tk=d, tn=tn(mlp_up.shape[2]), out_dtype=BF16,
                    relu2=True)
        xf = _matmul([u], mlp_down, layer, d, tk=min(K_TILE_DOWN, mlp_down.shape[1]),
                     tn=_col_tile(d, DOWN_COL_TILE), res=xf)
    return _norm_cast(xf, norm_final, F32).reshape(bsz, seq, d)
```

```python
import functools

import jax
import jax.numpy as jnp
from jax import lax
from jax.experimental import pallas as pl
from jax.experimental.pallas import tpu as pltpu

F32 = jnp.float32
BF16 = jnp.bfloat16
HIGHEST = lax.Precision.HIGHEST

NORM_EPS = 1e-6
GN_EPS = 64e-5
LRU_C = 8.0
CONV_WIDTH = 4
RWKV_HEAD = 64
DECAY_SCALE = 0.6065306597126334
LANES = 128
SUBLANES = 8
MXU_WIDTH = 256
VMEM_LIMIT_BYTES = 62 * 1024 * 1024
RWKV_CHUNK = 64
RWKV_ROWS = 128
RWKV_PAIRS = 16
LRU_ROWS = 2048
ML_CHUNK_ROWS = 256
NORM_ROWS = 512
ROW_TILE = 2048
COL_TILE = 512
K_TILE_OUT = 2048
DOWN_COL_TILE = 1024
K_TILE_DOWN = 2048


def _cparams(semantics):
    return pltpu.CompilerParams(dimension_semantics=semantics, vmem_limit_bytes=VMEM_LIMIT_BYTES)


def _softplus(x):
    return jnp.maximum(x, 0.0) + jnp.log1p(jnp.exp(-jnp.abs(x)))


def _sigmoid(x):
    return 1.0 / (1.0 + jnp.exp(-x))


_NN = (((1,), (0,)), ((), ()))
_NT = (((1,), (1,)), ((), ()))
_TN = (((0,), (0,)), ((), ()))
_BNN = (((2,), (1,)), ((0,), (0,)))
_BNT = (((2,), (2,)), ((0,), (0,)))
_BTN = (((1,), (1,)), ((0,), (0,)))


def _dot(a, b, dims=_NN, precision=None):
    return lax.dot_general(a, b, dims, precision=precision, preferred_element_type=F32)


def _bdot(a, b, dims=_NN):
    return lax.dot_general(a.astype(BF16), b.astype(BF16), dims, preferred_element_type=F32)


def _norm_cast_kernel(x_ref, g_ref, o_ref):
    x = x_ref[...]
    ms = jnp.mean(x * x, axis=-1, keepdims=True)
    o_ref[...] = (x * lax.rsqrt(ms + NORM_EPS) * g_ref[...]).astype(o_ref.dtype)


def _norm_cast(x, g, out_dtype):
    t, d = x.shape
    tm = min(NORM_ROWS, t)
    return pl.pallas_call(
        _norm_cast_kernel,
        grid=(t // tm,),
        in_specs=[pl.BlockSpec((tm, d), lambda i: (i, 0)), pl.BlockSpec((1, d), lambda i: (0, 0))],
        out_specs=pl.BlockSpec((tm, d), lambda i: (i, 0)),
        out_shape=jax.ShapeDtypeStruct((t, d), out_dtype),
        compiler_params=_cparams(("parallel",)),
        name="rmsnorm",
    )(x, g.reshape(1, d))


def _matmul_kernel(*refs, n_lhs, nk, has_res, relu2, w_t, res_dma):
    a_refs, w_ref = refs[:n_lhs], refs[n_lhs]
    r_ref = refs[n_lhs + 1] if has_res else None
    o_ref = refs[n_lhs + 1 + has_res]
    acc_ref = refs[-1] if (not has_res and nk > 1) else None
    res_sem = refs[-1] if res_dma else None
    kk = pl.program_id(2)

    def finish(acc):
        if relu2:
            acc = jnp.maximum(acc, 0.0)
            acc = acc * acc
        return acc.astype(o_ref.dtype)

    def accumulate(a_ref):
        def part():
            return lax.dot_general(a_ref[...], w_ref[...].astype(BF16), _NT if w_t else _NN,
                                   preferred_element_type=F32)

        if res_dma:
            @pl.when(kk == 0)
            def _():
                tm, tn = o_ref.shape
                rows = pl.ds(pl.multiple_of(pl.program_id(0) * tm, tm), tm)
                cols = pl.ds(pl.multiple_of(pl.program_id(1) * tn, tn), tn)
                copy = pltpu.make_async_copy(r_ref.at[rows, cols], o_ref, res_sem)
                copy.start()
                first = part()
                copy.wait()
                o_ref[...] += first

            @pl.when(kk > 0)
            def _():
                o_ref[...] += part()
        elif has_res:
            @pl.when(kk == 0)
            def _():
                o_ref[...] = r_ref[...] + part()

            @pl.when(kk > 0)
            def _():
                o_ref[...] += part()
        elif nk == 1:
            o_ref[...] = finish(part())
        else:
            @pl.when(kk == 0)
            def _():
                acc_ref[...] = part()

            @pl.when((kk > 0) & (kk < nk - 1))
            def _():
                acc_ref[...] += part()

            @pl.when(kk == nk - 1)
            def _():
                o_ref[...] = finish(acc_ref[...] + part())

    if n_lhs == 1:
        accumulate(a_refs[0])
    else:
        per = nk // n_lhs
        for idx, a_ref in enumerate(a_refs):
            pl.when((kk >= idx * per) & (kk < (idx + 1) * per))(functools.partial(accumulate, a_ref))


def _matmul(lhs, w, layer, n, *, tk, tn, tm=ROW_TILE, out_dtype=F32, res=None, relu2=False, w_t=False,
            lhs_double_buffer=False, res_dma=False):
    t = lhs[0].shape[0]
    k = w.shape[2 if w_t else 1]
    tm = min(tm, t)
    nk = k // tk
    assert t % tm == 0 and k % tk == 0 and n % tn == 0
    per = nk // len(lhs)
    assert all(a.shape[1] == per * tk for a in lhs) and per * len(lhs) == nk
    once = dict(pipeline_mode=pl.Buffered(1)) if (per == 1 and not lhs_double_buffer) else {}

    def lhs_spec(idx):
        return pl.BlockSpec((tm, tk), lambda i, j, kk: (i, jnp.clip(kk - idx * per, 0, per - 1)), **once)

    a_specs = [lhs_spec(idx) for idx in range(len(lhs))]
    has_res = res is not None
    assert not (has_res and (relu2 or out_dtype != F32))
    tile = pl.BlockSpec((tm, tn), lambda i, j, kk: (i, j))
    scratch = [pltpu.VMEM((tm, tn), F32)] if (not has_res and nk > 1) else []
    if w_t:
        w_spec = pl.BlockSpec((None, tn, tk), lambda i, j, kk: (layer, j, kk))
    else:
        w_spec = pl.BlockSpec((None, tk, tn), lambda i, j, kk: (layer, kk, j))
    assert not res_dma or (has_res and len(lhs) == 1)
    if res_dma:
        scratch = scratch + [pltpu.SemaphoreType.DMA(())]
    res_spec = pl.BlockSpec(memory_space=pl.ANY) if res_dma else tile
    return pl.pallas_call(
        functools.partial(_matmul_kernel, n_lhs=len(lhs), nk=nk, has_res=has_res, relu2=relu2, w_t=w_t,
                          res_dma=res_dma),
        grid=(t // tm, n // tn, nk),
        in_specs=a_specs + [w_spec] + ([res_spec] if has_res else []),
        out_specs=tile,
        out_shape=jax.ShapeDtypeStruct((t, n), out_dtype),
        scratch_shapes=scratch,
        compiler_params=_cparams(("parallel", "parallel", "arbitrary")),
        name="matmul",
    )(*lhs, w, *([res] if has_res else []))


def _lru_kernel(u_ref, gate_ref, cw_ref, cb_ref, wa_ref, ba_ref, wx_ref, bx_ref, lam_ref,
                o_ref, xbuf, hcar, *, tc):
    @pl.when(pl.program_id(2) == 0)
    def _():
        xbuf[0:SUBLANES, :] = jnp.zeros((SUBLANES, xbuf.shape[1]), F32)
        hcar[...] = jnp.zeros(hcar.shape, F32)

    u = u_ref[...].astype(F32)
    xbuf[SUBLANES:, :] = u
    cw = cw_ref[...]
    conv = cb_ref[...] + cw[CONV_WIDTH - 1:CONV_WIDTH] * u
    for j in range(CONV_WIDTH - 1):
        back = CONV_WIDTH - 1 - j
        conv = conv + cw[j:j + 1] * xbuf[SUBLANES - back:SUBLANES - back + tc, :]
    xbuf[0:SUBLANES, :] = u[tc - SUBLANES:, :]

    cbf = conv.astype(BF16)
    r = _sigmoid(jnp.dot(cbf, wa_ref[...], preferred_element_type=F32) + ba_ref[...])
    i = _sigmoid(jnp.dot(cbf, wx_ref[...], preferred_element_type=F32) + bx_ref[...])
    log_a = (-LRU_C) * r * _softplus(-lam_ref[...])
    a = jnp.exp(log_a)
    b = jnp.sqrt(-jnp.tanh(log_a) * (a * a + 1.0)) * (i * conv)

    n_tiles = tc // SUBLANES
    a = a.reshape(n_tiles, SUBLANES, a.shape[1])
    b = b.reshape(n_tiles, SUBLANES, b.shape[1])
    sub = lax.broadcasted_iota(jnp.int32, a.shape, 1)
    d = 1
    while d < SUBLANES:
        a_sh = pltpu.roll(a, d, 1)
        b_sh = pltpu.roll(b, d, 1)
        keep = sub >= d
        b = jnp.where(keep, a * b_sh + b, b)
        a = jnp.where(keep, a * a_sh, a)
        d *= 2
    carry = hcar[...]
    tiles = []
    for i in range(n_tiles):
        tiles.append(a[i] * carry + b[i])
        carry = tiles[-1][SUBLANES - 1:SUBLANES, :]
    h = jnp.concatenate(tiles, axis=0)
    hcar[...] = carry
    o_ref[...] = (h * jax.nn.gelu(gate_ref[...].astype(F32))).astype(o_ref.dtype)


def _lru_branch(z, conv_w, conv_b, w_a, b_a, w_x, b_x, lam, *, tc):
    bsz, seq, _ = z.shape
    heads, blk, _ = w_a.shape
    width = heads * blk
    assert blk % LANES == 0 and seq % tc == 0
    row = lambda v: v.reshape(1, width)
    vec_spec = pl.BlockSpec((1, blk), lambda b, h, t: (0, h))
    mat_spec = pl.BlockSpec((None, blk, blk), lambda b, h, t: (h, 0, 0))
    return pl.pallas_call(
        functools.partial(_lru_kernel, tc=tc),
        grid=(bsz, heads, seq // tc),
        in_specs=[
            pl.BlockSpec((None, tc, blk), lambda b, h, t: (b, t, h)),
            pl.BlockSpec((None, tc, blk), lambda b, h, t: (b, t, heads + h)),
            pl.BlockSpec((CONV_WIDTH, blk), lambda b, h, t: (0, h)),
            vec_spec, mat_spec, vec_spec, mat_spec, vec_spec, vec_spec,
        ],
        out_specs=pl.BlockSpec((None, tc, blk), lambda b, h, t: (b, t, h)),
        out_shape=jax.ShapeDtypeStruct((bsz, seq, width), BF16),
        scratch_shapes=[pltpu.VMEM((tc + SUBLANES, blk), F32), pltpu.VMEM((1, blk), F32)],
        compiler_params=_cparams(("parallel", "parallel", "arbitrary")),
        name="rglru",
    )(z, z, conv_w, row(conv_b), w_a.astype(BF16), row(b_a), w_x.astype(BF16), row(b_x), row(lam))


def _rwkv_kernel(r_ref, k_ref, v_ref, lora_ref, mu_r, mu_k, mu_v, mu_l, w0_ref, w2_ref, a0_ref, a2_ref,
                 g2_ref, kk_ref, ka_ref, rk_ref, lnw_ref, lnb_ref, o_ref,
                 rbuf, kbuf, vbuf, lbuf, state, *, tc, pp, segs):
    L = RWKV_CHUNK
    P2 = 2 * L
    nc = tc // L
    assert P2 == LANES and RWKV_HEAD == L

    @pl.when(pl.program_id(2) == 0)
    def _():
        for buf in (rbuf, kbuf, vbuf, lbuf):
            buf[...] = jnp.zeros(buf.shape, F32)
        state[...] = jnp.zeros(state.shape, F32)

    def token_shift(ref, buf, mu_ref):
        cur = ref[...].astype(F32)
        n_tiles = tc // SUBLANES
        rolled = pltpu.roll(cur.reshape(n_tiles, SUBLANES, cur.shape[1]), 1, 1)
        before = jnp.concatenate([buf[...][None], rolled[:n_tiles - 1]], axis=0)
        sub = lax.broadcasted_iota(jnp.int32, rolled.shape, 1)
        prev = jnp.where(sub == 0, before, rolled).reshape(cur.shape)
        buf[...] = rolled[n_tiles - 1]
        return cur + (prev - cur) * mu_ref[...]

    r = token_shift(r_ref, rbuf, mu_r)
    k = token_shift(k_ref, kbuf, mu_k)
    v = token_shift(v_ref, vbuf, mu_v)
    lora = token_shift(lora_ref, lbuf, mu_l)

    lora_w, lora_a, lora_g = (lora[:, lo:hi] for lo, hi in segs)
    dw = _dot(jnp.tanh(lora_w).astype(BF16), w2_ref[...])
    logw = -DECAY_SCALE * _sigmoid(w0_ref[...] + dw)
    a = _sigmoid(a0_ref[...] + _dot(lora_a.astype(BF16), a2_ref[...]))
    g = _dot(_sigmoid(lora_g).astype(BF16), g2_ref[...])

    ri = lax.broadcasted_iota(jnp.int32, (P2, P2), 0)
    ci = lax.broadcasted_iota(jnp.int32, (P2, P2), 1)
    blk = jnp.where(ri >= L, 1, 0) == jnp.where(ci >= L, 1, 0)
    head_ones = jnp.where(blk, 1.0, 0.0).astype(BF16)

    def to_rows(x):
        return jnp.concatenate([x[:, j * LANES:(j + 1) * LANES] for j in range(pp)], axis=0)

    def from_rows(x):
        return jnp.concatenate([x[j * tc:(j + 1) * tc, :] for j in range(pp)], axis=1)

    def head_sum(x):
        return from_rows(_dot(to_rows(x).astype(BF16), head_ones))

    kk = k * kk_ref[...]
    kk = kk * lax.rsqrt(jnp.maximum(head_sum(kk * kk), 1e-24))
    kmod = k * (1.0 + (a - 1.0) * ka_ref[...])
    kka = kk * a
    bonus = head_sum(r * kmod * rk_ref[...]) * v

    rt = lax.broadcasted_iota(jnp.int32, (tc, tc), 0)
    ct = lax.broadcasted_iota(jnp.int32, (tc, tc), 1)
    shift = L.bit_length() - 1
    same_chunk = lax.shift_right_logical(rt, shift) == lax.shift_right_logical(ct, shift)
    tril_chunk = jnp.where(same_chunk & (ct <= rt), 1.0, 0.0).astype(BF16)
    logw_hi = logw.astype(BF16)
    logw_lo = (logw - logw_hi.astype(F32)).astype(BF16)
    cum = _dot(tril_chunk, logw_hi) + _dot(tril_chunk, logw_lo)
    p_inv = jnp.exp(-cum)
    r_d = r * jnp.exp(cum)
    kk_d = kk * jnp.exp(cum - logw)
    k_d = kmod * p_inv
    a_d = kka * p_inv
    p_last = [jnp.exp(cum[(c + 1) * L - 1:(c + 1) * L, :]) for c in range(nc)]
    p_tail = p_inv * jnp.concatenate([jnp.broadcast_to(p, (L, cum.shape[1])) for p in p_last], axis=0)
    k_t = kmod * p_tail
    a_t = kka * p_tail

    strict = (blk & (ci < ri))[None]
    incl = (blk & (ci <= ri))[None]
    eye = jnp.where(ri == ci, 1.0, 0.0).astype(F32)[None]
    lane_row = lax.broadcasted_iota(jnp.int32, (1, LANES), 1)
    m0 = jnp.where(lane_row < RWKV_HEAD, 1.0, 0.0).astype(F32)
    m1 = 1.0 - m0

    def batch(x, zero_other_head=True):
        out = []
        for c in range(nc):
            for j in range(pp):
                xs = x[c * L:(c + 1) * L, j * LANES:(j + 1) * LANES]
                out.append(jnp.concatenate([xs * m0, xs * m1] if zero_other_head else [xs, xs], axis=0))
        return jnp.stack(out)

    r_b, kk_b, v_b, kt_b, at_b = (batch(x) for x in (r_d, kk_d, v, k_t, a_t))
    k_b, a_b = batch(k_d, False), batch(a_d, False)
    kkr_b = jnp.concatenate([kk_b, r_b], axis=1)

    gram = _bdot(kkr_b, jnp.concatenate([k_b, a_b], axis=1), _BNT)
    a_kk_k = jnp.where(strict, gram[:, 0:P2, 0:P2], 0.0)
    n_mat = jnp.where(strict, gram[:, 0:P2, P2:], 0.0)
    a_r_k = jnp.where(incl, gram[:, P2:, 0:P2], 0.0)
    a_r_a = jnp.where(incl, gram[:, P2:, P2:], 0.0)

    inv = eye - n_mat
    pw = _bdot(n_mat, n_mat, _BNN)
    steps = 2
    while 2 * steps < L:
        both = _bdot(jnp.concatenate([inv, pw], axis=1), pw, _BNN)
        inv = inv + both[:, 0:P2]
        pw = both[:, P2:]
        steps *= 2
    inv = inv + _bdot(inv, pw, _BNN)
    av = _bdot(jnp.concatenate([a_kk_k, a_r_k], axis=1), v_b, _BNN)

    s = state[...]
    y_chunks = []
    for c in range(nc):
        sl = slice(c * pp, (c + 1) * pp)
        from_state = _bdot(kkr_b[sl], s, _BNT) + av[sl]
        sa = _bdot(inv[sl], from_state[:, 0:P2], _BNN)
        y_s = from_state[:, P2:] - _bdot(a_r_a[sl], sa, _BNN)
        y_c = y_s[:, 0:L, :] + y_s[:, L:, :]
        y_chunks.append(jnp.concatenate([y_c[j] for j in range(pp)], axis=1))
        decay = jnp.stack([p_last[c][:, j * LANES:(j + 1) * LANES] for j in range(pp)])
        s = s * decay + _bdot(jnp.concatenate([v_b[sl], sa], axis=1),
                              jnp.concatenate([kt_b[sl], -at_b[sl]], axis=1), _BTN)
    state[...] = s

    y = jnp.concatenate(y_chunks, axis=0)
    inv_n = 1.0 / RWKV_HEAD
    mean = head_sum(y) * inv_n
    yc = y - mean
    var = head_sum(yc * yc) * inv_n
    yn = yc * lax.rsqrt(var + GN_EPS) * lnw_ref[...] + lnb_ref[...]
    o_ref[...] = ((yn + bonus) * g).astype(o_ref.dtype)


def _rwkv_branch(z, col0, z_lora, mu, w0, w2, a0, a2, g2, k_k, k_a, r_k, ln_w, ln_b, *, tc, pp):
    bsz, seq, _ = z.shape
    width = w0.shape[0]
    bw = pp * LANES
    segs = _lane_segments((w2.shape[0], a2.shape[0], g2.shape[0]))
    lora_w = segs[-1][1]
    assert width % bw == 0 and col0 % bw == 0 and z_lora.shape[2] == lora_w
    assert seq % tc == 0 and tc % RWKV_CHUNK == 0 and tc % SUBLANES == 0
    groups = width // bw
    cb = col0 // bw

    def pad_rows(wm, seg):
        return jnp.pad(wm, ((0, seg[1] - seg[0] - wm.shape[0]), (0, 0))).astype(BF16)

    w2p, a2p, g2p = (pad_rows(wm, seg) for wm, seg in zip((w2, a2, g2), segs))
    mu_r, mu_k, mu_v = (mu[i * width:(i + 1) * width].reshape(1, width) for i in range(3))
    mu_l = _pad_segments(mu[3 * width:].reshape(1, -1), (w2.shape[0], a2.shape[0], g2.shape[0]), axis=1)
    row = lambda x: x.reshape(1, width)

    act = lambda off: pl.BlockSpec((None, tc, bw), lambda b, p, t: (b, t, cb + off * groups + p))
    vec = pl.BlockSpec((1, bw), lambda b, p, t: (0, p))
    lmat = lambda seg: pl.BlockSpec((seg[1] - seg[0], bw), lambda b, p, t: (0, p))
    return pl.pallas_call(
        functools.partial(_rwkv_kernel, tc=tc, pp=pp, segs=segs),
        grid=(bsz, groups, seq // tc),
        in_specs=[
            act(0), act(1), act(2),
            pl.BlockSpec((None, tc, lora_w), lambda b, p, t: (b, t, 0)),
            vec, vec, vec,
            pl.BlockSpec((1, lora_w), lambda b, p, t: (0, 0)),
            vec, lmat(segs[0]), vec, lmat(segs[1]), lmat(segs[2]), vec, vec, vec, vec, vec,
        ],
        out_specs=pl.BlockSpec((None, tc, bw), lambda b, p, t: (b, t, p)),
        out_shape=jax.ShapeDtypeStruct((bsz, seq, width), BF16),
        scratch_shapes=[
            pltpu.VMEM((SUBLANES, bw), F32), pltpu.VMEM((SUBLANES, bw), F32),
            pltpu.VMEM((SUBLANES, bw), F32), pltpu.VMEM((SUBLANES, lora_w), F32),
            pltpu.VMEM((pp, LANES, LANES), F32),
        ],
        compiler_params=_cparams(("parallel", "parallel", "arbitrary")),
        name="rwkv7",
    )(z, z, z, z_lora, mu_r, mu_k, mu_v, mu_l, row(w0), w2p, row(a0), a2p, g2p, row(k_k), row(k_a),
      row(r_k), row(ln_w), row(ln_b))


def _mlstm_kernel(q_ref, k_ref, v_ref, zo_ref, gcol_ref, grow_ref, bcol_ref, brow_ref, nw_ref, o_ref,
                  c_st, n_st, m_st, *, heads, dqk, dv):
    L = q_ref.shape[0]

    @pl.when(pl.program_id(1) == 0)
    def _():
        c_st[...] = jnp.zeros(c_st.shape, F32)
        n_st[...] = jnp.zeros(n_st.shape, F32)
        m_st[...] = jnp.zeros(m_st.shape, F32)

    ri = lax.broadcasted_iota(jnp.int32, (L, L), 0)
    ci = lax.broadcasted_iota(jnp.int32, (L, L), 1)
    causal = ci <= ri
    tril_ones = jnp.where(causal, 1.0, 0.0).astype(F32)
    triu_ones = jnp.where(ci >= ri, 1.0, 0.0).astype(F32)

    gc = gcol_ref[...] + bcol_ref[...]
    gr = grow_ref[...] + brow_ref[...]
    b_c = _dot(tril_ones, -_softplus(-gc), precision=HIGHEST)
    b_r = _dot(-_softplus(-gr), triu_ones, precision=HIGHEST)

    per_head = lambda f: jnp.stack([f(h) for h in range(heads)])
    q = per_head(lambda h: q_ref[:, h * dqk:(h + 1) * dqk])
    q = (q.astype(F32) * dqk ** -0.5).astype(BF16)
    k = per_head(lambda h: k_ref[:, h * dqk:(h + 1) * dqk])
    v = per_head(lambda h: v_ref[:, h * dv:(h + 1) * dv])
    bc = per_head(lambda h: b_c[:, heads + h:heads + h + 1])
    lic = per_head(lambda h: gc[:, h:h + 1])
    br = per_head(lambda h: b_r[heads + h:heads + h + 1, :])
    lir = per_head(lambda h: gr[h:h + 1, :])
    m_prev = m_st[...][:, 0:1, 0:1]
    c_mat = c_st[...]
    n_vec = n_st[...]

    log_d = jnp.where(causal[None], bc - br + lir, -jnp.inf)
    log_inter = bc + m_prev
    m_t = jnp.maximum(jnp.max(log_d, axis=-1, keepdims=True), log_inter)
    p = _bdot(q, k, _BNT) * jnp.exp(log_d - m_t)
    inter = jnp.exp(log_inter - m_t)
    num = _bdot(p, v, _BNN) + inter * _bdot(q, c_mat, _BNN)
    den = (jnp.sum(p, axis=-1, keepdims=True)
           + inter * jnp.sum(q.astype(F32) * n_vec, axis=-1, keepdims=True))
    inv_den = 1.0 / jnp.maximum(jnp.abs(den), jnp.exp(-m_t))

    b_last = bc[:, L - 1:L, :]
    log_g = b_last - bc + lic
    m_new = jnp.maximum(b_last + m_prev, jnp.max(log_g, axis=1, keepdims=True))
    carry = jnp.exp(b_last + m_prev - m_new)
    wk = k.astype(F32) * jnp.exp(log_g - m_new)
    c_st[...] = carry * c_mat + _bdot(wk, v, _BTN)
    n_st[...] = carry * n_vec + jnp.sum(wk, axis=1, keepdims=True)
    m_st[...] = jnp.broadcast_to(m_new, m_st.shape)

    msq = jnp.mean(num * num, axis=-1, keepdims=True)
    hs = num * (inv_den * lax.rsqrt(inv_den * inv_den * msq + NORM_EPS))
    for h in range(heads):
        sl = slice(h * dv, (h + 1) * dv)
        o_ref[:, sl] = (hs[h] * nw_ref[:, sl] * _sigmoid(zo_ref[:, sl].astype(F32))).astype(o_ref.dtype)


def _mlstm_branch(z, gates, b_i, b_f, norm_w, *, heads, qk, vdim):
    bsz, seq, _ = z.shape
    L = ML_CHUNK_ROWS
    dqk, dv = qk // heads, vdim // heads
    assert seq % L == 0 and vdim % qk == 0 and qk % LANES == 0 and (2 * qk) % vdim == 0
    vblk = 2 * qk // vdim
    bias = jnp.concatenate([b_i, b_f]).astype(F32)
    bias_col = jnp.zeros((1, LANES), F32).at[0, :2 * heads].set(bias)
    gates_row = jnp.swapaxes(gates[:, :, :2 * heads], 1, 2)
    return pl.pallas_call(
        functools.partial(_mlstm_kernel, heads=heads, dqk=dqk, dv=dv),
        grid=(bsz, seq // L),
        in_specs=[
            pl.BlockSpec((None, L, qk), lambda b, c: (b, c, 0)),
            pl.BlockSpec((None, L, qk), lambda b, c: (b, c, 1)),
            pl.BlockSpec((None, L, vdim), lambda b, c: (b, c, vblk)),
            pl.BlockSpec((None, L, vdim), lambda b, c: (b, c, vblk + 1)),
            pl.BlockSpec((None, L, LANES), lambda b, c: (b, c, 0)),
            pl.BlockSpec((None, 2 * heads, L), lambda b, c: (b, 0, c)),
            pl.BlockSpec((1, LANES), lambda b, c: (0, 0)),
            pl.BlockSpec((2 * heads, 1), lambda b, c: (0, 0)),
            pl.BlockSpec((1, vdim), lambda b, c: (0, 0)),
        ],
        out_specs=pl.BlockSpec((None, L, vdim), lambda b, c: (b, c, 0)),
        out_shape=jax.ShapeDtypeStruct((bsz, seq, vdim), BF16),
        scratch_shapes=[
            pltpu.VMEM((heads, dqk, dv), F32),
            pltpu.VMEM((heads, 1, dqk), F32),
            pltpu.VMEM((heads, SUBLANES, LANES), F32),
        ],
        compiler_params=_cparams(("parallel", "arbitrary")),
        name="mlstm",
    )(z, z, z, z, gates, gates_row, bias_col, bias.reshape(-1, 1), norm_w.reshape(1, vdim))


def _col_tile(n, cap):
    for step in (MXU_WIDTH, LANES):
        fits = [c for c in range(step, cap + 1, step) if n % c == 0]
        if fits:
            return fits[-1]
    raise ValueError(f"no lane-aligned column tile divides {n}")


def _lane_segments(widths):
    segs, lo = [], 0
    for w in widths:
        hi = lo + -(-w // LANES) * LANES
        segs.append((lo, hi))
        lo = hi
    return segs


def _pad_segments(x, widths, axis):
    assert x.shape[axis] == sum(widths)
    parts, start = [], 0
    for w, (lo, hi) in zip(widths, _lane_segments(widths)):
        part = lax.slice_in_dim(x, start, start + w, axis=axis)
        pad = [(0, 0)] * x.ndim
        pad[axis] = (0, hi - lo - w)
        parts.append(jnp.pad(part, pad))
        start += w
    return jnp.concatenate(parts, axis=axis)


def kernel(x, norm_mix, norm_mlp, norm_final, mlp_up, mlp_down, hy_in, lru_conv_w, lru_conv_b, lru_wa, lru_ba, lru_wx, lru_bx, lru_lam, rwkv_mu, rwkv_w0, rwkv_w2, rwkv_a0, rwkv_a2, rwkv_g2, rwkv_kk, rwkv_ka, rwkv_rk, rwkv_ln_w, rwkv_ln_b, hy_out, ml_in, ml_bi, ml_bf, ml_norm, ml_out):
    bsz, seq, d = x.shape
    t = bsz * seq
    depth = norm_mix.shape[0]
    xf = x.reshape(t, d)

    lru_w = lru_lam.shape[1]
    rw_w = rwkv_w0.shape[1]
    hy_main = 2 * lru_w + 3 * rw_w
    heads = ml_bi.shape[1]
    vdim = ml_norm.shape[1]
    qk = (ml_in.shape[2] - 2 * vdim - 2 * heads) // 2
    ml_main = 2 * qk + 2 * vdim

    hy_in_t, ml_in_t = jnp.swapaxes(hy_in, 1, 2), jnp.swapaxes(ml_in, 1, 2)
    lora_widths = (rwkv_w2.shape[1], rwkv_a2.shape[1], rwkv_g2.shape[1])
    hy_lora_t = _pad_segments(hy_in_t[:, hy_main:, :], lora_widths, axis=1)
    ml_gate_t = _pad_segments(ml_in_t[:, ml_main:, :], (2 * heads,), axis=1)
    tn = lambda n: _col_tile(n, COL_TILE)
    tk_out = min(K_TILE_OUT, lru_w, vdim)

    for layer in range(depth):
        hn = _norm_cast(xf, norm_mix[layer], BF16)
        if layer % 2 == 0:
            e = layer // 2
            z = _matmul([hn], hy_in_t, e, hy_main, tk=d, tn=tn(hy_main), out_dtype=BF16, w_t=True)
            z = z.reshape(bsz, seq, hy_main)
            n_lora = hy_lora_t.shape[1]
            z_lora = _matmul([hn], hy_lora_t, e, n_lora, tk=d, tn=tn(n_lora), w_t=True,
                             tm=ROW_TILE // 2, lhs_double_buffer=True)
            z_lora = z_lora.reshape(bsz, seq, n_lora)
            ya = _lru_branch(z, lru_conv_w[e], lru_conv_b[e], lru_wa[e], lru_ba[e], lru_wx[e], lru_bx[e],
                             lru_lam[e], tc=min(LRU_ROWS, seq))
            yb = _rwkv_branch(z, 2 * lru_w, z_lora, rwkv_mu[e], rwkv_w0[e], rwkv_w2[e], rwkv_a0[e], rwkv_a2[e],
                              rwkv_g2[e], rwkv_kk[e], rwkv_ka[e], rwkv_rk[e], rwkv_ln_w[e], rwkv_ln_b[e],
                              tc=min(RWKV_ROWS, seq), pp=min(RWKV_PAIRS, rw_w // LANES))
            lhs = [ya.reshape(t, lru_w), yb.reshape(t, rw_w)]
            assert lru_w == rw_w
            xf = _matmul(lhs, hy_out, e, d, tk=tk_out, tn=tn(d), res=xf)
        else:
            o = layer // 2
            z = _matmul([hn], ml_in_t, o, ml_main, tk=d, tn=tn(ml_main), out_dtype=BF16, w_t=True)
            z = z.reshape(bsz, seq, ml_main)
            n_gate = ml_gate_t.shape[1]
            gates = _matmul([hn], ml_gate_t, o, n_gate, tk=d, tn=tn(n_gate), w_t=True,
                            tm=ROW_TILE // 2, lhs_double_buffer=True)
            gates = gates.reshape(bsz, seq, n_gate)
            hs = _mlstm_branch(z, gates, ml_bi[o], ml_bf[o], ml_norm[o], heads=heads, qk=qk, vdim=vdim)
            xf = _matmul([hs.reshape(t, vdim)], ml_out, o, d, tk=vdim, tn=tn(d), res=xf)
        hn = _norm_cast(xf, norm_mlp[layer], BF16)
        u = _matmul([hn], mlp_up, layer, mlp_up.shape[2], tk=d, tn=tn(mlp_up.shape[2]), out_dtype=BF16,
                    relu2=True)
        xf = _matmul([u], mlp_down, layer, d, tk=min(K_TILE_DOWN, mlp_down.shape[1]),
                     tn=_col_tile(d, DOWN_COL_TILE), res=xf, res_dma=True)
    return _norm_cast(xf, norm_final, F32).reshape(bsz, seq, d)
```

```python
import functools

import jax
import jax.numpy as jnp
from jax import lax
from jax.experimental import pallas as pl
from jax.experimental.pallas import tpu as pltpu

F32 = jnp.float32
BF16 = jnp.bfloat16
HIGHEST = lax.Precision.HIGHEST

NORM_EPS = 1e-6
GN_EPS = 64e-5
LRU_C = 8.0
CONV_WIDTH = 4
RWKV_HEAD = 64
DECAY_SCALE = 0.6065306597126334
LANES = 128
SUBLANES = 8
MXU_WIDTH = 256
VMEM_LIMIT_BYTES = 62 * 1024 * 1024
RWKV_CHUNK = 64
RWKV_ROWS = 128
RWKV_PAIRS = 16
LRU_ROWS = 2048
ML_CHUNK_ROWS = 256
NORM_ROWS = 512
ROW_TILE = 2048
COL_TILE = 512
K_TILE_OUT = 2048
DOWN_COL_TILE = 1024
K_TILE_DOWN = 2048


def _cparams(semantics):
    return pltpu.CompilerParams(dimension_semantics=semantics, vmem_limit_bytes=VMEM_LIMIT_BYTES)


def _softplus(x):
    return jnp.maximum(x, 0.0) + jnp.log1p(jnp.exp(-jnp.abs(x)))


def _sigmoid(x):
    return 1.0 / (1.0 + jnp.exp(-x))


_NN = (((1,), (0,)), ((), ()))
_NT = (((1,), (1,)), ((), ()))
_TN = (((0,), (0,)), ((), ()))
_BNN = (((2,), (1,)), ((0,), (0,)))
_BNT = (((2,), (2,)), ((0,), (0,)))
_BTN = (((1,), (1,)), ((0,), (0,)))


def _dot(a, b, dims=_NN, precision=None):
    return lax.dot_general(a, b, dims, precision=precision, preferred_element_type=F32)


def _bdot(a, b, dims=_NN):
    return lax.dot_general(a.astype(BF16), b.astype(BF16), dims, preferred_element_type=F32)


def _norm_cast_kernel(x_ref, g_ref, o_ref):
    x = x_ref[...]
    ms = jnp.mean(x * x, axis=-1, keepdims=True)
    o_ref[...] = (x * lax.rsqrt(ms + NORM_EPS) * g_ref[...]).astype(o_ref.dtype)


def _norm_cast(x, g, out_dtype):
    t, d = x.shape
    tm = min(NORM_ROWS, t)
    return pl.pallas_call(
        _norm_cast_kernel,
        grid=(t // tm,),
        in_specs=[pl.BlockSpec((tm, d), lambda i: (i, 0)), pl.BlockSpec((1, d), lambda i: (0, 0))],
        out_specs=pl.BlockSpec((tm, d), lambda i: (i, 0)),
        out_shape=jax.ShapeDtypeStruct((t, d), out_dtype),
        compiler_params=_cparams(("parallel",)),
        name="rmsnorm",
    )(x, g.reshape(1, d))


def _matmul_kernel(*refs, n_lhs, nk, has_res, relu2, w_t, res_dma):
    a_refs, w_ref = refs[:n_lhs], refs[n_lhs]
    r_ref = refs[n_lhs + 1] if has_res else None
    o_ref = refs[n_lhs + 1 + has_res]
    acc_ref = refs[-1] if (not has_res and nk > 1) else None
    res_sem = refs[-1] if res_dma else None
    kk = pl.program_id(2)

    def finish(acc):
        if relu2:
            acc = jnp.maximum(acc, 0.0)
            acc = acc * acc
        return acc.astype(o_ref.dtype)

    def accumulate(a_ref):
        def part():
            return lax.dot_general(a_ref[...], w_ref[...].astype(BF16), _NT if w_t else _NN,
                                   preferred_element_type=F32)

        if res_dma:
            @pl.when(kk == 0)
            def _():
                tm, tn = o_ref.shape
                rows = pl.ds(pl.multiple_of(pl.program_id(0) * tm, tm), tm)
                cols = pl.ds(pl.multiple_of(pl.program_id(1) * tn, tn), tn)
                copy = pltpu.make_async_copy(r_ref.at[rows, cols], o_ref, res_sem)
                copy.start()
                first = part()
                copy.wait()
                o_ref[...] += first

            @pl.when(kk > 0)
            def _():
                o_ref[...] += part()
        elif has_res:
            @pl.when(kk == 0)
            def _():
                o_ref[...] = r_ref[...] + part()

            @pl.when(kk > 0)
            def _():
                o_ref[...] += part()
        elif nk == 1:
            o_ref[...] = finish(part())
        else:
            @pl.when(kk == 0)
            def _():
                acc_ref[...] = part()

            @pl.when((kk > 0) & (kk < nk - 1))
            def _():
                acc_ref[...] += part()

            @pl.when(kk == nk - 1)
            def _():
                o_ref[...] = finish(acc_ref[...] + part())

    if n_lhs == 1:
        accumulate(a_refs[0])
    else:
        per = nk // n_lhs
        for idx, a_ref in enumerate(a_refs):
            pl.when((kk >= idx * per) & (kk < (idx + 1) * per))(functools.partial(accumulate, a_ref))


def _matmul(lhs, w, layer, n, *, tk, tn, tm=ROW_TILE, out_dtype=F32, res=None, relu2=False, w_t=False,
            lhs_double_buffer=False, res_dma=False):
    t = lhs[0].shape[0]
    k = w.shape[2 if w_t else 1]
    tm = min(tm, t)
    nk = k // tk
    assert t % tm == 0 and k % tk == 0 and n % tn == 0
    per = nk // len(lhs)
    assert all(a.shape[1] == per * tk for a in lhs) and per * len(lhs) == nk
    once = dict(pipeline_mode=pl.Buffered(1)) if (per == 1 and not lhs_double_buffer) else {}

    def lhs_spec(idx):
        return pl.BlockSpec((tm, tk), lambda i, j, kk: (i, jnp.clip(kk - idx * per, 0, per - 1)), **once)

    a_specs = [lhs_spec(idx) for idx in range(len(lhs))]
    has_res = res is not None
    assert not (has_res and (relu2 or out_dtype != F32))
    tile = pl.BlockSpec((tm, tn), lambda i, j, kk: (i, j))
    scratch = [pltpu.VMEM((tm, tn), F32)] if (not has_res and nk > 1) else []
    if w_t:
        w_spec = pl.BlockSpec((None, tn, tk), lambda i, j, kk: (layer, j, kk))
    else:
        w_spec = pl.BlockSpec((None, tk, tn), lambda i, j, kk: (layer, kk, j))
    assert not res_dma or (has_res and len(lhs) == 1)
    if res_dma:
        scratch = scratch + [pltpu.SemaphoreType.DMA(())]
    res_spec = pl.BlockSpec(memory_space=pl.ANY) if res_dma else tile
    return pl.pallas_call(
        functools.partial(_matmul_kernel, n_lhs=len(lhs), nk=nk, has_res=has_res, relu2=relu2, w_t=w_t,
                          res_dma=res_dma),
        grid=(t // tm, n // tn, nk),
        in_specs=a_specs + [w_spec] + ([res_spec] if has_res else []),
        out_specs=tile,
        out_shape=jax.ShapeDtypeStruct((t, n), out_dtype),
        scratch_shapes=scratch,
        compiler_params=_cparams(("parallel", "parallel", "arbitrary")),
        name="matmul",
    )(*lhs, w, *([res] if has_res else []))


def _lru_kernel(u_ref, gate_ref, cw_ref, cb_ref, wa_ref, ba_ref, wx_ref, bx_ref, lam_ref,
                o_ref, xbuf, hcar, *, tc):
    @pl.when(pl.program_id(2) == 0)
    def _():
        xbuf[0:SUBLANES, :] = jnp.zeros((SUBLANES, xbuf.shape[1]), F32)
        hcar[...] = jnp.zeros(hcar.shape, F32)

    u = u_ref[...].astype(F32)
    xbuf[SUBLANES:, :] = u
    cw = cw_ref[...]
    conv = cb_ref[...] + cw[CONV_WIDTH - 1:CONV_WIDTH] * u
    for j in range(CONV_WIDTH - 1):
        back = CONV_WIDTH - 1 - j
        conv = conv + cw[j:j + 1] * xbuf[SUBLANES - back:SUBLANES - back + tc, :]
    xbuf[0:SUBLANES, :] = u[tc - SUBLANES:, :]

    cbf = conv.astype(BF16)
    r = _sigmoid(jnp.dot(cbf, wa_ref[...], preferred_element_type=F32) + ba_ref[...])
    i = _sigmoid(jnp.dot(cbf, wx_ref[...], preferred_element_type=F32) + bx_ref[...])
    log_a = (-LRU_C) * r * _softplus(-lam_ref[...])
    a = jnp.exp(log_a)
    b = jnp.sqrt(-jnp.tanh(log_a) * (a * a + 1.0)) * (i * conv)

    n_tiles = tc // SUBLANES
    a = a.reshape(n_tiles, SUBLANES, a.shape[1])
    b = b.reshape(n_tiles, SUBLANES, b.shape[1])
    sub = lax.broadcasted_iota(jnp.int32, a.shape, 1)
    d = 1
    while d < SUBLANES:
        a_sh = pltpu.roll(a, d, 1)
        b_sh = pltpu.roll(b, d, 1)
        keep = sub >= d
        b = jnp.where(keep, a * b_sh + b, b)
        a = jnp.where(keep, a * a_sh, a)
        d *= 2
    carry = hcar[...]
    tiles = []
    for i in range(n_tiles):
        tiles.append(a[i] * carry + b[i])
        carry = tiles[-1][SUBLANES - 1:SUBLANES, :]
    h = jnp.concatenate(tiles, axis=0)
    hcar[...] = carry
    o_ref[...] = (h * jax.nn.gelu(gate_ref[...].astype(F32))).astype(o_ref.dtype)


def _lru_branch(z, conv_w, conv_b, w_a, b_a, w_x, b_x, lam, *, tc):
    bsz, seq, _ = z.shape
    heads, blk, _ = w_a.shape
    width = heads * blk
    assert blk % LANES == 0 and seq % tc == 0
    row = lambda v: v.reshape(1, width)
    vec_spec = pl.BlockSpec((1, blk), lambda b, h, t: (0, h))
    mat_spec = pl.BlockSpec((None, blk, blk), lambda b, h, t: (h, 0, 0))
    return pl.pallas_call(
        functools.partial(_lru_kernel, tc=tc),
        grid=(bsz, heads, seq // tc),
        in_specs=[
            pl.BlockSpec((None, tc, blk), lambda b, h, t: (b, t, h)),
            pl.BlockSpec((None, tc, blk), lambda b, h, t: (b, t, heads + h)),
            pl.BlockSpec((CONV_WIDTH, blk), lambda b, h, t: (0, h)),
            vec_spec, mat_spec, vec_spec, mat_spec, vec_spec, vec_spec,
        ],
        out_specs=pl.BlockSpec((None, tc, blk), lambda b, h, t: (b, t, h)),
        out_shape=jax.ShapeDtypeStruct((bsz, seq, width), BF16),
        scratch_shapes=[pltpu.VMEM((tc + SUBLANES, blk), F32), pltpu.VMEM((1, blk), F32)],
        compiler_params=_cparams(("parallel", "parallel", "arbitrary")),
        name="rglru",
    )(z, z, conv_w, row(conv_b), w_a.astype(BF16), row(b_a), w_x.astype(BF16), row(b_x), row(lam))


def _rwkv_kernel(r_ref, k_ref, v_ref, lora_ref, mu_r, mu_k, mu_v, mu_l, w0_ref, w2_ref, a0_ref, a2_ref,
                 g2_ref, kk_ref, ka_ref, rk_ref, lnw_ref, lnb_ref, o_ref,
                 rbuf, kbuf, vbuf, lbuf, state, *, tc, pp, segs):
    L = RWKV_CHUNK
    P2 = 2 * L
    nc = tc // L
    assert P2 == LANES and RWKV_HEAD == L

    @pl.when(pl.program_id(2) == 0)
    def _():
        for buf in (rbuf, kbuf, vbuf, lbuf):
            buf[...] = jnp.zeros(buf.shape, F32)
        state[...] = jnp.zeros(state.shape, F32)

    def token_shift(ref, buf, mu_ref):
        cur = ref[...].astype(F32)
        n_tiles = tc // SUBLANES
        rolled = pltpu.roll(cur.reshape(n_tiles, SUBLANES, cur.shape[1]), 1, 1)
        before = jnp.concatenate([buf[...][None], rolled[:n_tiles - 1]], axis=0)
        sub = lax.broadcasted_iota(jnp.int32, rolled.shape, 1)
        prev = jnp.where(sub == 0, before, rolled).reshape(cur.shape)
        buf[...] = rolled[n_tiles - 1]
        return cur + (prev - cur) * mu_ref[...]

    r = token_shift(r_ref, rbuf, mu_r)
    k = token_shift(k_ref, kbuf, mu_k)
    v = token_shift(v_ref, vbuf, mu_v)
    lora = token_shift(lora_ref, lbuf, mu_l)

    lora_w, lora_a, lora_g = (lora[:, lo:hi] for lo, hi in segs)
    dw = _dot(jnp.tanh(lora_w).astype(BF16), w2_ref[...])
    logw = -DECAY_SCALE * _sigmoid(w0_ref[...] + dw)
    a = _sigmoid(a0_ref[...] + _dot(lora_a.astype(BF16), a2_ref[...]))
    g = _dot(_sigmoid(lora_g).astype(BF16), g2_ref[...])

    ri = lax.broadcasted_iota(jnp.int32, (P2, P2), 0)
    ci = lax.broadcasted_iota(jnp.int32, (P2, P2), 1)
    blk = jnp.where(ri >= L, 1, 0) == jnp.where(ci >= L, 1, 0)
    head_ones = jnp.where(blk, 1.0, 0.0).astype(BF16)

    def to_rows(x):
        return jnp.concatenate([x[:, j * LANES:(j + 1) * LANES] for j in range(pp)], axis=0)

    def from_rows(x):
        return jnp.concatenate([x[j * tc:(j + 1) * tc, :] for j in range(pp)], axis=1)

    def head_sum(x):
        return from_rows(_dot(to_rows(x).astype(BF16), head_ones))

    kk = k * kk_ref[...]
    kk = kk * lax.rsqrt(jnp.maximum(head_sum(kk * kk), 1e-24))
    kmod = k * (1.0 + (a - 1.0) * ka_ref[...])
    kka = kk * a
    bonus = head_sum(r * kmod * rk_ref[...]) * v

    rt = lax.broadcasted_iota(jnp.int32, (tc, tc), 0)
    ct = lax.broadcasted_iota(jnp.int32, (tc, tc), 1)
    shift = L.bit_length() - 1
    same_chunk = lax.shift_right_logical(rt, shift) == lax.shift_right_logical(ct, shift)
    tril_chunk = jnp.where(same_chunk & (ct <= rt), 1.0, 0.0).astype(BF16)
    logw_hi = logw.astype(BF16)
    logw_lo = (logw - logw_hi.astype(F32)).astype(BF16)
    cum = _dot(tril_chunk, logw_hi) + _dot(tril_chunk, logw_lo)
    p_inv = jnp.exp(-cum)
    r_d = r * jnp.exp(cum)
    kk_d = kk * jnp.exp(cum - logw)
    k_d = kmod * p_inv
    a_d = kka * p_inv
    p_last = [jnp.exp(cum[(c + 1) * L - 1:(c + 1) * L, :]) for c in range(nc)]
    p_tail = p_inv * jnp.concatenate([jnp.broadcast_to(p, (L, cum.shape[1])) for p in p_last], axis=0)
    k_t = kmod * p_tail
    a_t = kka * p_tail

    strict = (blk & (ci < ri))[None]
    incl = (blk & (ci <= ri))[None]
    eye = jnp.where(ri == ci, 1.0, 0.0).astype(F32)[None]
    lane_row = lax.broadcasted_iota(jnp.int32, (1, LANES), 1)
    m0 = jnp.where(lane_row < RWKV_HEAD, 1.0, 0.0).astype(BF16)
    m1 = jnp.where(lane_row < RWKV_HEAD, 0.0, 1.0).astype(BF16)

    def batch(x, zero_other_head=True):
        x = x.astype(BF16)
        out = []
        for c in range(nc):
            for j in range(pp):
                xs = x[c * L:(c + 1) * L, j * LANES:(j + 1) * LANES]
                out.append(jnp.concatenate([xs * m0, xs * m1] if zero_other_head else [xs, xs], axis=0))
        return jnp.stack(out)

    r_b, kk_b, v_b, kt_b, at_b = (batch(x) for x in (r_d, kk_d, v, k_t, a_t))
    k_b, a_b = batch(k_d, False), batch(a_d, False)
    kkr_b = jnp.concatenate([kk_b, r_b], axis=1)

    gram = _bdot(kkr_b, jnp.concatenate([k_b, a_b], axis=1), _BNT)
    a_kk_k = jnp.where(strict, gram[:, 0:P2, 0:P2], 0.0)
    n_mat = jnp.where(strict, gram[:, 0:P2, P2:], 0.0)
    a_r_k = jnp.where(incl, gram[:, P2:, 0:P2], 0.0)
    a_r_a = jnp.where(incl, gram[:, P2:, P2:], 0.0)

    inv = eye - n_mat
    pw = _bdot(n_mat, n_mat, _BNN)
    steps = 2
    while 2 * steps < L:
        both = _bdot(jnp.concatenate([inv, pw], axis=1), pw, _BNN)
        inv = inv + both[:, 0:P2]
        pw = both[:, P2:]
        steps *= 2
    inv = inv + _bdot(inv, pw, _BNN)
    av = _bdot(jnp.concatenate([a_kk_k, a_r_k], axis=1), v_b, _BNN)

    s = state[...]
    y_chunks = []
    for c in range(nc):
        sl = slice(c * pp, (c + 1) * pp)
        from_state = _bdot(kkr_b[sl], s, _BNT) + av[sl]
        sa = _bdot(inv[sl], from_state[:, 0:P2], _BNN)
        y_s = from_state[:, P2:] - _bdot(a_r_a[sl], sa, _BNN)
        y_c = y_s[:, 0:L, :] + y_s[:, L:, :]
        y_chunks.append(jnp.concatenate([y_c[j] for j in range(pp)], axis=1))
        decay = jnp.stack([p_last[c][:, j * LANES:(j + 1) * LANES] for j in range(pp)])
        s = s * decay + _bdot(jnp.concatenate([v_b[sl], sa], axis=1),
                              jnp.concatenate([kt_b[sl], -at_b[sl]], axis=1), _BTN)
    state[...] = s

    y = jnp.concatenate(y_chunks, axis=0)
    inv_n = 1.0 / RWKV_HEAD
    mean = head_sum(y) * inv_n
    yc = y - mean
    var = head_sum(yc * yc) * inv_n
    yn = yc * lax.rsqrt(var + GN_EPS) * lnw_ref[...] + lnb_ref[...]
    o_ref[...] = ((yn + bonus) * g).astype(o_ref.dtype)


def _rwkv_branch(z, col0, z_lora, mu, w0, w2, a0, a2, g2, k_k, k_a, r_k, ln_w, ln_b, *, tc, pp):
    bsz, seq, _ = z.shape
    width = w0.shape[0]
    bw = pp * LANES
    segs = _lane_segments((w2.shape[0], a2.shape[0], g2.shape[0]))
    lora_w = segs[-1][1]
    assert width % bw == 0 and col0 % bw == 0 and z_lora.shape[2] == lora_w
    assert seq % tc == 0 and tc % RWKV_CHUNK == 0 and tc % SUBLANES == 0
    groups = width // bw
    cb = col0 // bw

    def pad_rows(wm, seg):
        return jnp.pad(wm, ((0, seg[1] - seg[0] - wm.shape[0]), (0, 0))).astype(BF16)

    w2p, a2p, g2p = (pad_rows(wm, seg) for wm, seg in zip((w2, a2, g2), segs))
    mu_r, mu_k, mu_v = (mu[i * width:(i + 1) * width].reshape(1, width) for i in range(3))
    mu_l = _pad_segments(mu[3 * width:].reshape(1, -1), (w2.shape[0], a2.shape[0], g2.shape[0]), axis=1)
    row = lambda x: x.reshape(1, width)

    act = lambda off: pl.BlockSpec((None, tc, bw), lambda b, p, t: (b, t, cb + off * groups + p))
    vec = pl.BlockSpec((1, bw), lambda b, p, t: (0, p))
    lmat = lambda seg: pl.BlockSpec((seg[1] - seg[0], bw), lambda b, p, t: (0, p))
    return pl.pallas_call(
        functools.partial(_rwkv_kernel, tc=tc, pp=pp, segs=segs),
        grid=(bsz, groups, seq // tc),
        in_specs=[
            act(0), act(1), act(2),
            pl.BlockSpec((None, tc, lora_w), lambda b, p, t: (b, t, 0)),
            vec, vec, vec,
            pl.BlockSpec((1, lora_w), lambda b, p, t: (0, 0)),
            vec, lmat(segs[0]), vec, lmat(segs[1]), lmat(segs[2]), vec, vec, vec, vec, vec,
        ],
        out_specs=pl.BlockSpec((None, tc, bw), lambda b, p, t: (b, t, p)),
        out_shape=jax.ShapeDtypeStruct((bsz, seq, width), BF16),
        scratch_shapes=[
            pltpu.VMEM((SUBLANES, bw), F32), pltpu.VMEM((SUBLANES, bw), F32),
            pltpu.VMEM((SUBLANES, bw), F32), pltpu.VMEM((SUBLANES, lora_w), F32),
            pltpu.VMEM((pp, LANES, LANES), F32),
        ],
        compiler_params=_cparams(("parallel", "parallel", "arbitrary")),
        name="rwkv7",
    )(z, z, z, z_lora, mu_r, mu_k, mu_v, mu_l, row(w0), w2p, row(a0), a2p, g2p, row(k_k), row(k_a),
      row(r_k), row(ln_w), row(ln_b))


def _mlstm_kernel(q_ref, k_ref, v_ref, zo_ref, gcol_ref, grow_ref, bcol_ref, brow_ref, nw_ref, o_ref,
                  c_st, n_st, m_st, *, heads, dqk, dv):
    L = q_ref.shape[0]

    @pl.when(pl.program_id(1) == 0)
    def _():
        c_st[...] = jnp.zeros(c_st.shape, F32)
        n_st[...] = jnp.zeros(n_st.shape, F32)
        m_st[...] = jnp.zeros(m_st.shape, F32)

    ri = lax.broadcasted_iota(jnp.int32, (L, L), 0)
    ci = lax.broadcasted_iota(jnp.int32, (L, L), 1)
    causal = ci <= ri
    tril_ones = jnp.where(causal, 1.0, 0.0).astype(F32)
    triu_ones = jnp.where(ci >= ri, 1.0, 0.0).astype(F32)

    gc = gcol_ref[...] + bcol_ref[...]
    gr = grow_ref[...] + brow_ref[...]
    b_c = _dot(tril_ones, -_softplus(-gc), precision=HIGHEST)
    b_r = _dot(-_softplus(-gr), triu_ones, precision=HIGHEST)

    per_head = lambda f: jnp.stack([f(h) for h in range(heads)])
    q = per_head(lambda h: q_ref[:, h * dqk:(h + 1) * dqk])
    q = (q.astype(F32) * dqk ** -0.5).astype(BF16)
    k = per_head(lambda h: k_ref[:, h * dqk:(h + 1) * dqk])
    v = per_head(lambda h: v_ref[:, h * dv:(h + 1) * dv])
    bc = per_head(lambda h: b_c[:, heads + h:heads + h + 1])
    lic = per_head(lambda h: gc[:, h:h + 1])
    br = per_head(lambda h: b_r[heads + h:heads + h + 1, :])
    lir = per_head(lambda h: gr[h:h + 1, :])
    m_prev = m_st[...][:, 0:1, 0:1]
    c_mat = c_st[...]
    n_vec = n_st[...]

    log_d = jnp.where(causal[None], bc - br + lir, -jnp.inf)
    log_inter = bc + m_prev
    m_t = jnp.maximum(jnp.max(log_d, axis=-1, keepdims=True), log_inter)
    p = _bdot(q, k, _BNT) * jnp.exp(log_d - m_t)
    inter = jnp.exp(log_inter - m_t)
    num = _bdot(p, v, _BNN) + inter * _bdot(q, c_mat, _BNN)
    den = (jnp.sum(p, axis=-1, keepdims=True)
           + inter * jnp.sum(q.astype(F32) * n_vec, axis=-1, keepdims=True))
    inv_den = 1.0 / jnp.maximum(jnp.abs(den), jnp.exp(-m_t))

    b_last = bc[:, L - 1:L, :]
    log_g = b_last - bc + lic
    m_new = jnp.maximum(b_last + m_prev, jnp.max(log_g, axis=1, keepdims=True))
    carry = jnp.exp(b_last + m_prev - m_new)
    wk = k.astype(F32) * jnp.exp(log_g - m_new)
    c_st[...] = carry * c_mat + _bdot(wk, v, _BTN)
    n_st[...] = carry * n_vec + jnp.sum(wk, axis=1, keepdims=True)
    m_st[...] = jnp.broadcast_to(m_new, m_st.shape)

    msq = jnp.mean(num * num, axis=-1, keepdims=True)
    hs = num * (inv_den * lax.rsqrt(inv_den * inv_den * msq + NORM_EPS))
    for h in range(heads):
        sl = slice(h * dv, (h + 1) * dv)
        o_ref[:, sl] = (hs[h] * nw_ref[:, sl] * _sigmoid(zo_ref[:, sl].astype(F32))).astype(o_ref.dtype)


def _mlstm_branch(z, gates, b_i, b_f, norm_w, *, heads, qk, vdim):
    bsz, seq, _ = z.shape
    L = ML_CHUNK_ROWS
    dqk, dv = qk // heads, vdim // heads
    assert seq % L == 0 and vdim % qk == 0 and qk % LANES == 0 and (2 * qk) % vdim == 0
    vblk = 2 * qk // vdim
    bias = jnp.concatenate([b_i, b_f]).astype(F32)
    bias_col = jnp.zeros((1, LANES), F32).at[0, :2 * heads].set(bias)
    gates_row = jnp.swapaxes(gates[:, :, :2 * heads], 1, 2)
    return pl.pallas_call(
        functools.partial(_mlstm_kernel, heads=heads, dqk=dqk, dv=dv),
        grid=(bsz, seq // L),
        in_specs=[
            pl.BlockSpec((None, L, qk), lambda b, c: (b, c, 0)),
            pl.BlockSpec((None, L, qk), lambda b, c: (b, c, 1)),
            pl.BlockSpec((None, L, vdim), lambda b, c: (b, c, vblk)),
            pl.BlockSpec((None, L, vdim), lambda b, c: (b, c, vblk + 1)),
            pl.BlockSpec((None, L, LANES), lambda b, c: (b, c, 0)),
            pl.BlockSpec((None, 2 * heads, L), lambda b, c: (b, 0, c)),
            pl.BlockSpec((1, LANES), lambda b, c: (0, 0)),
            pl.BlockSpec((2 * heads, 1), lambda b, c: (0, 0)),
            pl.BlockSpec((1, vdim), lambda b, c: (0, 0)),
        ],
        out_specs=pl.BlockSpec((None, L, vdim), lambda b, c: (b, c, 0)),
        out_shape=jax.ShapeDtypeStruct((bsz, seq, vdim), BF16),
        scratch_shapes=[
            pltpu.VMEM((heads, dqk, dv), F32),
            pltpu.VMEM((heads, 1, dqk), F32),
            pltpu.VMEM((heads, SUBLANES, LANES), F32),
        ],
        compiler_params=_cparams(("parallel", "arbitrary")),
        name="mlstm",
    )(z, z, z, z, gates, gates_row, bias_col, bias.reshape(-1, 1), norm_w.reshape(1, vdim))


def _col_tile(n, cap):
    for step in (MXU_WIDTH, LANES):
        fits = [c for c in range(step, cap + 1, step) if n % c == 0]
        if fits:
            return fits[-1]
    raise ValueError(f"no lane-aligned column tile divides {n}")


def _lane_segments(widths):
    segs, lo = [], 0
    for w in widths:
        hi = lo + -(-w // LANES) * LANES
        segs.append((lo, hi))
        lo = hi
    return segs


def _pad_segments(x, widths, axis):
    assert x.shape[axis] == sum(widths)
    parts, start = [], 0
    for w, (lo, hi) in zip(widths, _lane_segments(widths)):
        part = lax.slice_in_dim(x, start, start + w, axis=axis)
        pad = [(0, 0)] * x.ndim
        pad[axis] = (0, hi - lo - w)
        parts.append(jnp.pad(part, pad))
        start += w
    return jnp.concatenate(parts, axis=axis)


def kernel(x, norm_mix, norm_mlp, norm_final, mlp_up, mlp_down, hy_in, lru_conv_w, lru_conv_b, lru_wa, lru_ba, lru_wx, lru_bx, lru_lam, rwkv_mu, rwkv_w0, rwkv_w2, rwkv_a0, rwkv_a2, rwkv_g2, rwkv_kk, rwkv_ka, rwkv_rk, rwkv_ln_w, rwkv_ln_b, hy_out, ml_in, ml_bi, ml_bf, ml_norm, ml_out):
    bsz, seq, d = x.shape
    t = bsz * seq
    depth = norm_mix.shape[0]
    xf = x.reshape(t, d)

    lru_w = lru_lam.shape[1]
    rw_w = rwkv_w0.shape[1]
    hy_main = 2 * lru_w + 3 * rw_w
    heads = ml_bi.shape[1]
    vdim = ml_norm.shape[1]
    qk = (ml_in.shape[2] - 2 * vdim - 2 * heads) // 2
    ml_main = 2 * qk + 2 * vdim

    hy_in_t, ml_in_t = jnp.swapaxes(hy_in, 1, 2), jnp.swapaxes(ml_in, 1, 2)
    lora_widths = (rwkv_w2.shape[1], rwkv_a2.shape[1], rwkv_g2.shape[1])
    hy_lora_t = _pad_segments(hy_in_t[:, hy_main:, :], lora_widths, axis=1)
    ml_gate_t = _pad_segments(ml_in_t[:, ml_main:, :], (2 * heads,), axis=1)
    tn = lambda n: _col_tile(n, COL_TILE)
    tk_out = min(K_TILE_OUT, lru_w, vdim)

    for layer in range(depth):
        hn = _norm_cast(xf, norm_mix[layer], BF16)
        if layer % 2 == 0:
            e = layer // 2
            z = _matmul([hn], hy_in_t, e, hy_main, tk=d, tn=tn(hy_main), out_dtype=BF16, w_t=True)
            z = z.reshape(bsz, seq, hy_main)
            n_lora = hy_lora_t.shape[1]
            z_lora = _matmul([hn], hy_lora_t, e, n_lora, tk=d, tn=tn(n_lora), w_t=True,
                             tm=ROW_TILE // 2, lhs_double_buffer=True)
            z_lora = z_lora.reshape(bsz, seq, n_lora)
            ya = _lru_branch(z, lru_conv_w[e], lru_conv_b[e], lru_wa[e], lru_ba[e], lru_wx[e], lru_bx[e],
                             lru_lam[e], tc=min(LRU_ROWS, seq))
            yb = _rwkv_branch(z, 2 * lru_w, z_lora, rwkv_mu[e], rwkv_w0[e], rwkv_w2[e], rwkv_a0[e], rwkv_a2[e],
                              rwkv_g2[e], rwkv_kk[e], rwkv_ka[e], rwkv_rk[e], rwkv_ln_w[e], rwkv_ln_b[e],
                              tc=min(RWKV_ROWS, seq), pp=min(RWKV_PAIRS, rw_w // LANES))
            lhs = [ya.reshape(t, lru_w), yb.reshape(t, rw_w)]
            assert lru_w == rw_w
            xf = _matmul(lhs, hy_out, e, d, tk=tk_out, tn=tn(d), res=xf)
        else:
            o = layer // 2
            z = _matmul([hn], ml_in_t, o, ml_main, tk=d, tn=tn(ml_main), out_dtype=BF16, w_t=True)
            z = z.reshape(bsz, seq, ml_main)
            n_gate = ml_gate_t.shape[1]
            gates = _matmul([hn], ml_gate_t, o, n_gate, tk=d, tn=tn(n_gate), w_t=True,
                            tm=ROW_TILE // 2, lhs_double_buffer=True)
            gates = gates.reshape(bsz, seq, n_gate)
            hs = _mlstm_branch(z, gates, ml_bi[o], ml_bf[o], ml_norm[o], heads=heads, qk=qk, vdim=vdim)
            xf = _matmul([hs.reshape(t, vdim)], ml_out, o, d, tk=vdim, tn=tn(d), res=xf)
        hn = _norm_cast(xf, norm_mlp[layer], BF16)
        u = _matmul([hn], mlp_up, layer, mlp_up.shape[2], tk=d, tn=tn(mlp_up.shape[2]), out_dtype=BF16,
                    relu2=True)
        xf = _matmul([u], mlp_down, layer, d, tk=min(K_TILE_DOWN, mlp_down.shape[1]),
                     tn=_col_tile(d, DOWN_COL_TILE), res=xf, res_dma=True)
    return _norm_cast(xf, norm_final, F32).reshape(bsz, seq, d)
```

```python
import functools

import jax
import jax.numpy as jnp
from jax import lax
from jax.experimental import pallas as pl
from jax.experimental.pallas import tpu as pltpu

F32 = jnp.float32
BF16 = jnp.bfloat16
HIGHEST = lax.Precision.HIGHEST

NORM_EPS = 1e-6
GN_EPS = 64e-5
LRU_C = 8.0
CONV_WIDTH = 4
RWKV_HEAD = 64
DECAY_SCALE = 0.6065306597126334
LANES = 128
SUBLANES = 8
MXU_WIDTH = 256
VMEM_LIMIT_BYTES = 62 * 1024 * 1024
RWKV_CHUNK = 64
RWKV_ROWS = 128
RWKV_PAIRS = 16
LRU_ROWS = 2048
ML_CHUNK_ROWS = 256
NORM_ROWS = 512
ROW_TILE = 2048
COL_TILE = 512
K_TILE_OUT = 2048
DOWN_COL_TILE = 1024
K_TILE_DOWN = 2048


def _cparams(semantics):
    return pltpu.CompilerParams(dimension_semantics=semantics, vmem_limit_bytes=VMEM_LIMIT_BYTES)


def _softplus(x):
    return jnp.maximum(x, 0.0) + jnp.log1p(jnp.exp(-jnp.abs(x)))


def _sigmoid(x):
    return 1.0 / (1.0 + jnp.exp(-x))


_NN = (((1,), (0,)), ((), ()))
_NT = (((1,), (1,)), ((), ()))
_TN = (((0,), (0,)), ((), ()))
_BNN = (((2,), (1,)), ((0,), (0,)))
_BNT = (((2,), (2,)), ((0,), (0,)))
_BTN = (((1,), (1,)), ((0,), (0,)))


def _dot(a, b, dims=_NN, precision=None):
    return lax.dot_general(a, b, dims, precision=precision, preferred_element_type=F32)


def _bdot(a, b, dims=_NN):
    return lax.dot_general(a.astype(BF16), b.astype(BF16), dims, preferred_element_type=F32)


def _norm_cast_kernel(x_ref, g_ref, o_ref):
    x = x_ref[...]
    ms = jnp.mean(x * x, axis=-1, keepdims=True)
    o_ref[...] = (x * lax.rsqrt(ms + NORM_EPS) * g_ref[...]).astype(o_ref.dtype)


def _norm_cast(x, g, out_dtype):
    t, d = x.shape
    tm = min(NORM_ROWS, t)
    return pl.pallas_call(
        _norm_cast_kernel,
        grid=(t // tm,),
        in_specs=[pl.BlockSpec((tm, d), lambda i: (i, 0)), pl.BlockSpec((1, d), lambda i: (0, 0))],
        out_specs=pl.BlockSpec((tm, d), lambda i: (i, 0)),
        out_shape=jax.ShapeDtypeStruct((t, d), out_dtype),
        compiler_params=_cparams(("parallel",)),
        name="rmsnorm",
    )(x, g.reshape(1, d))


def _matmul_kernel(*refs, n_lhs, nk, has_res, relu2, w_t, res_dma):
    a_refs, w_ref = refs[:n_lhs], refs[n_lhs]
    r_ref = refs[n_lhs + 1] if has_res else None
    o_ref = refs[n_lhs + 1 + has_res]
    acc_ref = refs[-1] if (not has_res and nk > 1) else None
    res_sem = refs[-1] if res_dma else None
    kk = pl.program_id(2)

    def finish(acc):
        if relu2:
            acc = jnp.maximum(acc, 0.0)
            acc = acc * acc
        return acc.astype(o_ref.dtype)

    joint = n_lhs > 1 and nk == 1

    def accumulate(a_ref):
        def part():
            if joint:
                total, off = None, 0
                for ref in a_refs:
                    rows = w_ref[off:off + ref.shape[1], :].astype(BF16)
                    p = jnp.dot(ref[...], rows, preferred_element_type=F32)
                    total = p if total is None else total + p
                    off += ref.shape[1]
                return total
            return lax.dot_general(a_ref[...], w_ref[...].astype(BF16), _NT if w_t else _NN,
                                   preferred_element_type=F32)

        if res_dma:
            @pl.when(kk == 0)
            def _():
                tm, tn = o_ref.shape
                rows = pl.ds(pl.multiple_of(pl.program_id(0) * tm, tm), tm)
                cols = pl.ds(pl.multiple_of(pl.program_id(1) * tn, tn), tn)
                copy = pltpu.make_async_copy(r_ref.at[rows, cols], o_ref, res_sem)
                copy.start()
                first = part()
                copy.wait()
                o_ref[...] += first

            @pl.when(kk > 0)
            def _():
                o_ref[...] += part()
        elif has_res:
            @pl.when(kk == 0)
            def _():
                o_ref[...] = r_ref[...] + part()

            @pl.when(kk > 0)
            def _():
                o_ref[...] += part()
        elif nk == 1:
            o_ref[...] = finish(part())
        else:
            @pl.when(kk == 0)
            def _():
                acc_ref[...] = part()

            @pl.when((kk > 0) & (kk < nk - 1))
            def _():
                acc_ref[...] += part()

            @pl.when(kk == nk - 1)
            def _():
                o_ref[...] = finish(acc_ref[...] + part())

    if n_lhs == 1 or joint:
        accumulate(a_refs[0])
    else:
        per = nk // n_lhs
        for idx, a_ref in enumerate(a_refs):
            pl.when((kk >= idx * per) & (kk < (idx + 1) * per))(functools.partial(accumulate, a_ref))


def _matmul(lhs, w, layer, n, *, tk, tn, tm=ROW_TILE, out_dtype=F32, res=None, relu2=False, w_t=False,
            lhs_double_buffer=False, res_dma=False):
    t = lhs[0].shape[0]
    k = w.shape[2 if w_t else 1]
    tm = min(tm, t)
    nk = k // tk
    assert t % tm == 0 and k % tk == 0 and n % tn == 0
    joint = len(lhs) > 1 and nk == 1
    per = 1 if joint else nk // len(lhs)
    if joint:
        assert sum(a.shape[1] for a in lhs) == k and not w_t
    else:
        assert all(a.shape[1] == per * tk for a in lhs) and per * len(lhs) == nk
    once = dict(pipeline_mode=pl.Buffered(1)) if (per == 1 and not lhs_double_buffer) else {}

    def lhs_spec(idx):
        if joint:
            return pl.BlockSpec((tm, lhs[idx].shape[1]), lambda i, j, kk: (i, 0), **once)
        return pl.BlockSpec((tm, tk), lambda i, j, kk: (i, jnp.clip(kk - idx * per, 0, per - 1)), **once)

    a_specs = [lhs_spec(idx) for idx in range(len(lhs))]
    has_res = res is not None
    assert not (has_res and (relu2 or out_dtype != F32))
    tile = pl.BlockSpec((tm, tn), lambda i, j, kk: (i, j))
    scratch = [pltpu.VMEM((tm, tn), F32)] if (not has_res and nk > 1) else []
    if w_t:
        w_spec = pl.BlockSpec((None, tn, tk), lambda i, j, kk: (layer, j, kk))
    else:
        w_spec = pl.BlockSpec((None, tk, tn), lambda i, j, kk: (layer, kk, j))
    assert not res_dma or (has_res and len(lhs) == 1)
    if res_dma:
        scratch = scratch + [pltpu.SemaphoreType.DMA(())]
    res_spec = pl.BlockSpec(memory_space=pl.ANY) if res_dma else tile
    return pl.pallas_call(
        functools.partial(_matmul_kernel, n_lhs=len(lhs), nk=nk, has_res=has_res, relu2=relu2, w_t=w_t,
                          res_dma=res_dma),
        grid=(t // tm, n // tn, nk),
        in_specs=a_specs + [w_spec] + ([res_spec] if has_res else []),
        out_specs=tile,
        out_shape=jax.ShapeDtypeStruct((t, n), out_dtype),
        scratch_shapes=scratch,
        compiler_params=_cparams(("parallel", "parallel", "arbitrary")),
        name="matmul",
    )(*lhs, w, *([res] if has_res else []))


def _lru_kernel(u_ref, gate_ref, cw_ref, cb_ref, wa_ref, ba_ref, wx_ref, bx_ref, lam_ref,
                o_ref, xbuf, hcar, *, tc):
    @pl.when(pl.program_id(2) == 0)
    def _():
        xbuf[0:SUBLANES, :] = jnp.zeros((SUBLANES, xbuf.shape[1]), F32)
        hcar[...] = jnp.zeros(hcar.shape, F32)

    u = u_ref[...].astype(F32)
    xbuf[SUBLANES:, :] = u
    cw = cw_ref[...]
    conv = cb_ref[...] + cw[CONV_WIDTH - 1:CONV_WIDTH] * u
    for j in range(CONV_WIDTH - 1):
        back = CONV_WIDTH - 1 - j
        conv = conv + cw[j:j + 1] * xbuf[SUBLANES - back:SUBLANES - back + tc, :]
    xbuf[0:SUBLANES, :] = u[tc - SUBLANES:, :]

    cbf = conv.astype(BF16)
    r = _sigmoid(jnp.dot(cbf, wa_ref[...], preferred_element_type=F32) + ba_ref[...])
    i = _sigmoid(jnp.dot(cbf, wx_ref[...], preferred_element_type=F32) + bx_ref[...])
    log_a = (-LRU_C) * r * _softplus(-lam_ref[...])
    a = jnp.exp(log_a)
    b = jnp.sqrt(-jnp.tanh(log_a) * (a * a + 1.0)) * (i * conv)

    n_tiles = tc // SUBLANES
    a = a.reshape(n_tiles, SUBLANES, a.shape[1])
    b = b.reshape(n_tiles, SUBLANES, b.shape[1])
    sub = lax.broadcasted_iota(jnp.int32, a.shape, 1)
    d = 1
    while d < SUBLANES:
        a_sh = pltpu.roll(a, d, 1)
        b_sh = pltpu.roll(b, d, 1)
        keep = sub >= d
        b = jnp.where(keep, a * b_sh + b, b)
        a = jnp.where(keep, a * a_sh, a)
        d *= 2
    carry = hcar[...]
    tiles = []
    for i in range(n_tiles):
        tiles.append(a[i] * carry + b[i])
        carry = tiles[-1][SUBLANES - 1:SUBLANES, :]
    h = jnp.concatenate(tiles, axis=0)
    hcar[...] = carry
    o_ref[...] = (h * jax.nn.gelu(gate_ref[...].astype(F32))).astype(o_ref.dtype)


def _lru_branch(z, conv_w, conv_b, w_a, b_a, w_x, b_x, lam, *, tc):
    bsz, seq, _ = z.shape
    heads, blk, _ = w_a.shape
    width = heads * blk
    assert blk % LANES == 0 and seq % tc == 0
    row = lambda v: v.reshape(1, width)
    vec_spec = pl.BlockSpec((1, blk), lambda b, h, t: (0, h))
    mat_spec = pl.BlockSpec((None, blk, blk), lambda b, h, t: (h, 0, 0))
    return pl.pallas_call(
        functools.partial(_lru_kernel, tc=tc),
        grid=(bsz, heads, seq // tc),
        in_specs=[
            pl.BlockSpec((None, tc, blk), lambda b, h, t: (b, t, h)),
            pl.BlockSpec((None, tc, blk), lambda b, h, t: (b, t, heads + h)),
            pl.BlockSpec((CONV_WIDTH, blk), lambda b, h, t: (0, h)),
            vec_spec, mat_spec, vec_spec, mat_spec, vec_spec, vec_spec,
        ],
        out_specs=pl.BlockSpec((None, tc, blk), lambda b, h, t: (b, t, h)),
        out_shape=jax.ShapeDtypeStruct((bsz, seq, width), BF16),
        scratch_shapes=[pltpu.VMEM((tc + SUBLANES, blk), F32), pltpu.VMEM((1, blk), F32)],
        compiler_params=_cparams(("parallel", "parallel", "arbitrary")),
        name="rglru",
    )(z, z, conv_w, row(conv_b), w_a.astype(BF16), row(b_a), w_x.astype(BF16), row(b_x), row(lam))


def _rwkv_kernel(r_ref, k_ref, v_ref, lora_ref, mu_r, mu_k, mu_v, mu_l, w0_ref, w2_ref, a0_ref, a2_ref,
                 g2_ref, kk_ref, ka_ref, rk_ref, lnw_ref, lnb_ref, o_ref,
                 rbuf, kbuf, vbuf, lbuf, state, *, tc, pp, segs):
    L = RWKV_CHUNK
    P2 = 2 * L
    nc = tc // L
    assert P2 == LANES and RWKV_HEAD == L

    @pl.when(pl.program_id(2) == 0)
    def _():
        for buf in (rbuf, kbuf, vbuf, lbuf):
            buf[...] = jnp.zeros(buf.shape, F32)
        state[...] = jnp.zeros(state.shape, F32)

    def token_shift(ref, buf, mu_ref):
        cur = ref[...].astype(F32)
        n_tiles = tc // SUBLANES
        rolled = pltpu.roll(cur.reshape(n_tiles, SUBLANES, cur.shape[1]), 1, 1)
        before = jnp.concatenate([buf[...][None], rolled[:n_tiles - 1]], axis=0)
        sub = lax.broadcasted_iota(jnp.int32, rolled.shape, 1)
        prev = jnp.where(sub == 0, before, rolled).reshape(cur.shape)
        buf[...] = rolled[n_tiles - 1]
        return cur + (prev - cur) * mu_ref[...]

    r = token_shift(r_ref, rbuf, mu_r)
    k = token_shift(k_ref, kbuf, mu_k)
    v = token_shift(v_ref, vbuf, mu_v)
    lora = token_shift(lora_ref, lbuf, mu_l)

    lora_w, lora_a, lora_g = (lora[:, lo:hi] for lo, hi in segs)
    dw = _dot(jnp.tanh(lora_w).astype(BF16), w2_ref[...])
    logw = -DECAY_SCALE * _sigmoid(w0_ref[...] + dw)
    a = _sigmoid(a0_ref[...] + _dot(lora_a.astype(BF16), a2_ref[...]))
    g = _dot(_sigmoid(lora_g).astype(BF16), g2_ref[...])

    ri = lax.broadcasted_iota(jnp.int32, (P2, P2), 0)
    ci = lax.broadcasted_iota(jnp.int32, (P2, P2), 1)
    blk = jnp.where(ri >= L, 1, 0) == jnp.where(ci >= L, 1, 0)
    head_ones = jnp.where(blk, 1.0, 0.0).astype(BF16)

    def to_rows(x):
        return jnp.concatenate([x[:, j * LANES:(j + 1) * LANES] for j in range(pp)], axis=0)

    def from_rows(x):
        return jnp.concatenate([x[j * tc:(j + 1) * tc, :] for j in range(pp)], axis=1)

    def head_sum(x):
        return from_rows(_dot(to_rows(x).astype(BF16), head_ones))

    kk = k * kk_ref[...]
    kk = kk * lax.rsqrt(jnp.maximum(head_sum(kk * kk), 1e-24))
    kmod = k * (1.0 + (a - 1.0) * ka_ref[...])
    kka = kk * a
    bonus = head_sum(r * kmod * rk_ref[...]) * v

    rt = lax.broadcasted_iota(jnp.int32, (tc, tc), 0)
    ct = lax.broadcasted_iota(jnp.int32, (tc, tc), 1)
    shift = L.bit_length() - 1
    same_chunk = lax.shift_right_logical(rt, shift) == lax.shift_right_logical(ct, shift)
    tril_chunk = jnp.where(same_chunk & (ct <= rt), 1.0, 0.0).astype(BF16)
    logw_hi = logw.astype(BF16)
    logw_lo = (logw - logw_hi.astype(F32)).astype(BF16)
    cum = _dot(tril_chunk, logw_hi) + _dot(tril_chunk, logw_lo)
    p_inv = jnp.exp(-cum)
    r_d = r * jnp.exp(cum)
    kk_d = kk * jnp.exp(cum - logw)
    k_d = kmod * p_inv
    a_d = kka * p_inv
    p_last = [jnp.exp(cum[(c + 1) * L - 1:(c + 1) * L, :]) for c in range(nc)]
    p_tail = p_inv * jnp.concatenate([jnp.broadcast_to(p, (L, cum.shape[1])) for p in p_last], axis=0)
    k_t = kmod * p_tail
    a_t = kka * p_tail

    strict = (blk & (ci < ri))[None]
    incl = (blk & (ci <= ri))[None]
    eye = jnp.where(ri == ci, 1.0, 0.0).astype(F32)[None]
    lane_row = lax.broadcasted_iota(jnp.int32, (1, LANES), 1)
    m0 = jnp.where(lane_row < RWKV_HEAD, 1.0, 0.0).astype(BF16)
    m1 = jnp.where(lane_row < RWKV_HEAD, 0.0, 1.0).astype(BF16)

    def batch(x, zero_other_head=True):
        x = x.astype(BF16)
        out = []
        for c in range(nc):
            for j in range(pp):
                xs = x[c * L:(c + 1) * L, j * LANES:(j + 1) * LANES]
                out.append(jnp.concatenate([xs * m0, xs * m1] if zero_other_head else [xs, xs], axis=0))
        return jnp.stack(out)

    r_b, kk_b, v_b, kt_b, at_b = (batch(x) for x in (r_d, kk_d, v, k_t, a_t))
    k_b, a_b = batch(k_d, False), batch(a_d, False)
    kkr_b = jnp.concatenate([kk_b, r_b], axis=1)

    gram = _bdot(kkr_b, jnp.concatenate([k_b, a_b], axis=1), _BNT)
    a_kk_k = jnp.where(strict, gram[:, 0:P2, 0:P2], 0.0)
    n_mat = jnp.where(strict, gram[:, 0:P2, P2:], 0.0)
    a_r_k = jnp.where(incl, gram[:, P2:, 0:P2], 0.0)
    a_r_a = jnp.where(incl, gram[:, P2:, P2:], 0.0)

    inv = eye - n_mat
    pw = _bdot(n_mat, n_mat, _BNN)
    steps = 2
    while 2 * steps < L:
        both = _bdot(jnp.concatenate([inv, pw], axis=1), pw, _BNN)
        inv = inv + both[:, 0:P2]
        pw = both[:, P2:]
        steps *= 2
    inv = inv + _bdot(inv, pw, _BNN)
    av = _bdot(jnp.concatenate([a_kk_k, a_r_k], axis=1), v_b, _BNN)

    s = state[...]
    y_chunks = []
    for c in range(nc):
        sl = slice(c * pp, (c + 1) * pp)
        from_state = _bdot(kkr_b[sl], s, _BNT) + av[sl]
        sa = _bdot(inv[sl], from_state[:, 0:P2], _BNN)
        y_s = from_state[:, P2:] - _bdot(a_r_a[sl], sa, _BNN)
        y_c = y_s[:, 0:L, :] + y_s[:, L:, :]
        y_chunks.append(jnp.concatenate([y_c[j] for j in range(pp)], axis=1))
        decay = jnp.stack([p_last[c][:, j * LANES:(j + 1) * LANES] for j in range(pp)])
        s = s * decay + _bdot(jnp.concatenate([v_b[sl], sa], axis=1),
                              jnp.concatenate([kt_b[sl], -at_b[sl]], axis=1), _BTN)
    state[...] = s

    y = jnp.concatenate(y_chunks, axis=0)
    inv_n = 1.0 / RWKV_HEAD
    mean = head_sum(y) * inv_n
    yc = y - mean
    var = head_sum(yc * yc) * inv_n
    yn = yc * lax.rsqrt(var + GN_EPS) * lnw_ref[...] + lnb_ref[...]
    o_ref[...] = ((yn + bonus) * g).astype(o_ref.dtype)


def _rwkv_branch(z, col0, z_lora, mu, w0, w2, a0, a2, g2, k_k, k_a, r_k, ln_w, ln_b, *, tc, pp):
    bsz, seq, _ = z.shape
    width = w0.shape[0]
    bw = pp * LANES
    segs = _lane_segments((w2.shape[0], a2.shape[0], g2.shape[0]))
    lora_w = segs[-1][1]
    assert width % bw == 0 and col0 % bw == 0 and z_lora.shape[2] == lora_w
    assert seq % tc == 0 and tc % RWKV_CHUNK == 0 and tc % SUBLANES == 0
    groups = width // bw
    cb = col0 // bw

    def pad_rows(wm, seg):
        return jnp.pad(wm, ((0, seg[1] - seg[0] - wm.shape[0]), (0, 0))).astype(BF16)

    w2p, a2p, g2p = (pad_rows(wm, seg) for wm, seg in zip((w2, a2, g2), segs))
    mu_r, mu_k, mu_v = (mu[i * width:(i + 1) * width].reshape(1, width) for i in range(3))
    mu_l = _pad_segments(mu[3 * width:].reshape(1, -1), (w2.shape[0], a2.shape[0], g2.shape[0]), axis=1)
    row = lambda x: x.reshape(1, width)

    act = lambda off: pl.BlockSpec((None, tc, bw), lambda b, p, t: (b, t, cb + off * groups + p))
    vec = pl.BlockSpec((1, bw), lambda b, p, t: (0, p))
    lmat = lambda seg: pl.BlockSpec((seg[1] - seg[0], bw), lambda b, p, t: (0, p))
    return pl.pallas_call(
        functools.partial(_rwkv_kernel, tc=tc, pp=pp, segs=segs),
        grid=(bsz, groups, seq // tc),
        in_specs=[
            act(0), act(1), act(2),
            pl.BlockSpec((None, tc, lora_w), lambda b, p, t: (b, t, 0)),
            vec, vec, vec,
            pl.BlockSpec((1, lora_w), lambda b, p, t: (0, 0)),
            vec, lmat(segs[0]), vec, lmat(segs[1]), lmat(segs[2]), vec, vec, vec, vec, vec,
        ],
        out_specs=pl.BlockSpec((None, tc, bw), lambda b, p, t: (b, t, p)),
        out_shape=jax.ShapeDtypeStruct((bsz, seq, width), BF16),
        scratch_shapes=[
            pltpu.VMEM((SUBLANES, bw), F32), pltpu.VMEM((SUBLANES, bw), F32),
            pltpu.VMEM((SUBLANES, bw), F32), pltpu.VMEM((SUBLANES, lora_w), F32),
            pltpu.VMEM((pp, LANES, LANES), F32),
        ],
        compiler_params=_cparams(("parallel", "parallel", "arbitrary")),
        name="rwkv7",
    )(z, z, z, z_lora, mu_r, mu_k, mu_v, mu_l, row(w0), w2p, row(a0), a2p, g2p, row(k_k), row(k_a),
      row(r_k), row(ln_w), row(ln_b))


def _mlstm_kernel(q_ref, k_ref, v_ref, zo_ref, gcol_ref, grow_ref, bcol_ref, brow_ref, nw_ref, o_ref,
                  c_st, n_st, m_st, *, heads, dqk, dv):
    L = q_ref.shape[0]

    @pl.when(pl.program_id(1) == 0)
    def _():
        c_st[...] = jnp.zeros(c_st.shape, F32)
        n_st[...] = jnp.zeros(n_st.shape, F32)
        m_st[...] = jnp.zeros(m_st.shape, F32)

    ri = lax.broadcasted_iota(jnp.int32, (L, L), 0)
    ci = lax.broadcasted_iota(jnp.int32, (L, L), 1)
    causal = ci <= ri
    tril_ones = jnp.where(causal, 1.0, 0.0).astype(F32)
    triu_ones = jnp.where(ci >= ri, 1.0, 0.0).astype(F32)

    gc = gcol_ref[...] + bcol_ref[...]
    gr = grow_ref[...] + brow_ref[...]
    b_c = _dot(tril_ones, -_softplus(-gc), precision=HIGHEST)
    b_r = _dot(-_softplus(-gr), triu_ones, precision=HIGHEST)

    per_head = lambda f: jnp.stack([f(h) for h in range(heads)])
    q = per_head(lambda h: q_ref[:, h * dqk:(h + 1) * dqk])
    q = (q.astype(F32) * dqk ** -0.5).astype(BF16)
    k = per_head(lambda h: k_ref[:, h * dqk:(h + 1) * dqk])
    v = per_head(lambda h: v_ref[:, h * dv:(h + 1) * dv])
    bc = per_head(lambda h: b_c[:, heads + h:heads + h + 1])
    lic = per_head(lambda h: gc[:, h:h + 1])
    br = per_head(lambda h: b_r[heads + h:heads + h + 1, :])
    lir = per_head(lambda h: gr[h:h + 1, :])
    m_prev = m_st[...][:, 0:1, 0:1]
    c_mat = c_st[...]
    n_vec = n_st[...]

    log_d = jnp.where(causal[None], bc - br + lir, -jnp.inf)
    log_inter = bc + m_prev
    m_t = jnp.maximum(jnp.max(log_d, axis=-1, keepdims=True), log_inter)
    p = _bdot(q, k, _BNT) * jnp.exp(log_d - m_t)
    inter = jnp.exp(log_inter - m_t)
    num = _bdot(p, v, _BNN) + inter * _bdot(q, c_mat, _BNN)
    den = (jnp.sum(p, axis=-1, keepdims=True)
           + inter * jnp.sum(q.astype(F32) * n_vec, axis=-1, keepdims=True))
    inv_den = 1.0 / jnp.maximum(jnp.abs(den), jnp.exp(-m_t))

    b_last = bc[:, L - 1:L, :]
    log_g = b_last - bc + lic
    m_new = jnp.maximum(b_last + m_prev, jnp.max(log_g, axis=1, keepdims=True))
    carry = jnp.exp(b_last + m_prev - m_new)
    wk = k.astype(F32) * jnp.exp(log_g - m_new)
    c_st[...] = carry * c_mat + _bdot(wk, v, _BTN)
    n_st[...] = carry * n_vec + jnp.sum(wk, axis=1, keepdims=True)
    m_st[...] = jnp.broadcast_to(m_new, m_st.shape)

    msq = jnp.mean(num * num, axis=-1, keepdims=True)
    hs = num * (inv_den * lax.rsqrt(inv_den * inv_den * msq + NORM_EPS))
    for h in range(heads):
        sl = slice(h * dv, (h + 1) * dv)
        o_ref[:, sl] = (hs[h] * nw_ref[:, sl] * _sigmoid(zo_ref[:, sl].astype(F32))).astype(o_ref.dtype)


def _mlstm_branch(z, gates, b_i, b_f, norm_w, *, heads, qk, vdim):
    bsz, seq, _ = z.shape
    L = ML_CHUNK_ROWS
    dqk, dv = qk // heads, vdim // heads
    assert seq % L == 0 and vdim % qk == 0 and qk % LANES == 0 and (2 * qk) % vdim == 0
    vblk = 2 * qk // vdim
    bias = jnp.concatenate([b_i, b_f]).astype(F32)
    bias_col = jnp.zeros((1, LANES), F32).at[0, :2 * heads].set(bias)
    gates_row = jnp.swapaxes(gates[:, :, :2 * heads], 1, 2)
    return pl.pallas_call(
        functools.partial(_mlstm_kernel, heads=heads, dqk=dqk, dv=dv),
        grid=(bsz, seq // L),
        in_specs=[
            pl.BlockSpec((None, L, qk), lambda b, c: (b, c, 0)),
            pl.BlockSpec((None, L, qk), lambda b, c: (b, c, 1)),
            pl.BlockSpec((None, L, vdim), lambda b, c: (b, c, vblk)),
            pl.BlockSpec((None, L, vdim), lambda b, c: (b, c, vblk + 1)),
            pl.BlockSpec((None, L, LANES), lambda b, c: (b, c, 0)),
            pl.BlockSpec((None, 2 * heads, L), lambda b, c: (b, 0, c)),
            pl.BlockSpec((1, LANES), lambda b, c: (0, 0)),
            pl.BlockSpec((2 * heads, 1), lambda b, c: (0, 0)),
            pl.BlockSpec((1, vdim), lambda b, c: (0, 0)),
        ],
        out_specs=pl.BlockSpec((None, L, vdim), lambda b, c: (b, c, 0)),
        out_shape=jax.ShapeDtypeStruct((bsz, seq, vdim), BF16),
        scratch_shapes=[
            pltpu.VMEM((heads, dqk, dv), F32),
            pltpu.VMEM((heads, 1, dqk), F32),
            pltpu.VMEM((heads, SUBLANES, LANES), F32),
        ],
        compiler_params=_cparams(("parallel", "arbitrary")),
        name="mlstm",
    )(z, z, z, z, gates, gates_row, bias_col, bias.reshape(-1, 1), norm_w.reshape(1, vdim))


def _col_tile(n, cap):
    for step in (MXU_WIDTH, LANES):
        fits = [c for c in range(step, cap + 1, step) if n % c == 0]
        if fits:
            return fits[-1]
    raise ValueError(f"no lane-aligned column tile divides {n}")


def _lane_segments(widths):
    segs, lo = [], 0
    for w in widths:
        hi = lo + -(-w // LANES) * LANES
        segs.append((lo, hi))
        lo = hi
    return segs


def _pad_segments(x, widths, axis):
    assert x.shape[axis] == sum(widths)
    parts, start = [], 0
    for w, (lo, hi) in zip(widths, _lane_segments(widths)):
        part = lax.slice_in_dim(x, start, start + w, axis=axis)
        pad = [(0, 0)] * x.ndim
        pad[axis] = (0, hi - lo - w)
        parts.append(jnp.pad(part, pad))
        start += w
    return jnp.concatenate(parts, axis=axis)


def kernel(x, norm_mix, norm_mlp, norm_final, mlp_up, mlp_down, hy_in, lru_conv_w, lru_conv_b, lru_wa, lru_ba, lru_wx, lru_bx, lru_lam, rwkv_mu, rwkv_w0, rwkv_w2, rwkv_a0, rwkv_a2, rwkv_g2, rwkv_kk, rwkv_ka, rwkv_rk, rwkv_ln_w, rwkv_ln_b, hy_out, ml_in, ml_bi, ml_bf, ml_norm, ml_out):
    bsz, seq, d = x.shape
    t = bsz * seq
    depth = norm_mix.shape[0]
    xf = x.reshape(t, d)

    lru_w = lru_lam.shape[1]
    rw_w = rwkv_w0.shape[1]
    hy_main = 2 * lru_w + 3 * rw_w
    heads = ml_bi.shape[1]
    vdim = ml_norm.shape[1]
    qk = (ml_in.shape[2] - 2 * vdim - 2 * heads) // 2
    ml_main = 2 * qk + 2 * vdim

    hy_in_t, ml_in_t = jnp.swapaxes(hy_in, 1, 2), jnp.swapaxes(ml_in, 1, 2)
    lora_widths = (rwkv_w2.shape[1], rwkv_a2.shape[1], rwkv_g2.shape[1])
    hy_lora_t = _pad_segments(hy_in_t[:, hy_main:, :], lora_widths, axis=1)
    ml_gate_t = _pad_segments(ml_in_t[:, ml_main:, :], (2 * heads,), axis=1)
    tn = lambda n: _col_tile(n, COL_TILE)
    tk_out = min(K_TILE_OUT, lru_w, vdim)

    for layer in range(depth):
        hn = _norm_cast(xf, norm_mix[layer], BF16)
        if layer % 2 == 0:
            e = layer // 2
            z = _matmul([hn], hy_in_t, e, hy_main, tk=d, tn=tn(hy_main), out_dtype=BF16, w_t=True)
            z = z.reshape(bsz, seq, hy_main)
            n_lora = hy_lora_t.shape[1]
            z_lora = _matmul([hn], hy_lora_t, e, n_lora, tk=d, tn=tn(n_lora), w_t=True,
                             tm=ROW_TILE // 2, lhs_double_buffer=True)
            z_lora = z_lora.reshape(bsz, seq, n_lora)
            ya = _lru_branch(z, lru_conv_w[e], lru_conv_b[e], lru_wa[e], lru_ba[e], lru_wx[e], lru_bx[e],
                             lru_lam[e], tc=min(LRU_ROWS, seq))
            yb = _rwkv_branch(z, 2 * lru_w, z_lora, rwkv_mu[e], rwkv_w0[e], rwkv_w2[e], rwkv_a0[e], rwkv_a2[e],
                              rwkv_g2[e], rwkv_kk[e], rwkv_ka[e], rwkv_rk[e], rwkv_ln_w[e], rwkv_ln_b[e],
                              tc=min(RWKV_ROWS, seq), pp=min(RWKV_PAIRS, rw_w // LANES))
            lhs = [ya.reshape(t, lru_w), yb.reshape(t, rw_w)]
            assert lru_w == rw_w
            xf = _matmul(lhs, hy_out, e, d, tk=lru_w + rw_w, tn=tn(d), res=xf)
        else:
            o = layer // 2
            z = _matmul([hn], ml_in_t, o, ml_main, tk=d, tn=tn(ml_main), out_dtype=BF16, w_t=True)
            z = z.reshape(bsz, seq, ml_main)
            n_gate = ml_gate_t.shape[1]
            gates = _matmul([hn], ml_gate_t, o, n_gate, tk=d, tn=tn(n_gate), w_t=True,
                            tm=ROW_TILE // 2, lhs_double_buffer=True)
            gates = gates.reshape(bsz, seq, n_gate)
            hs = _mlstm_branch(z, gates, ml_bi[o], ml_bf[o], ml_norm[o], heads=heads, qk=qk, vdim=vdim)
            xf = _matmul([hs.reshape(t, vdim)], ml_out, o, d, tk=vdim, tn=tn(d), res=xf)
        hn = _norm_cast(xf, norm_mlp[layer], BF16)
        u = _matmul([hn], mlp_up, layer, mlp_up.shape[2], tk=d, tn=tn(mlp_up.shape[2]), out_dtype=BF16,
                    relu2=True)
        xf = _matmul([u], mlp_down, layer, d, tk=min(K_TILE_DOWN, mlp_down.shape[1]),
                     tn=_col_tile(d, DOWN_COL_TILE), res=xf, res_dma=True)
    return _norm_cast(xf, norm_final, F32).reshape(bsz, seq, d)
```
